```python
import math
import jax, jax.numpy as jnp
from jax import lax
import numpy as np

D_MODEL = 2048
BATCH = 4
SEQ = 2048
DEPTH = 2
DEC_BATCH = 128
DEC_SEQ = 8
PAST_LEN = 16384
PAGE_SIZE = 128

RWKV_HEADS = 12
RWKV_HEAD_DIM = 64
RWKV_WIDTH = RWKV_HEADS * RWKV_HEAD_DIM
RWKV_DECAY_RANK = 64
RWKV_AAA_RANK = 64
RWKV_GATE_RANK = 128
RWKV_COLS = 3 * RWKV_WIDTH + RWKV_DECAY_RANK + RWKV_AAA_RANK + RWKV_GATE_RANK
RWKV_SPLITS = (RWKV_WIDTH, 2 * RWKV_WIDTH, 3 * RWKV_WIDTH,
               3 * RWKV_WIDTH + RWKV_DECAY_RANK,
               3 * RWKV_WIDTH + RWKV_DECAY_RANK + RWKV_AAA_RANK)
RWKV_LN_EPS = 64e-5
HGRN_HEADS = 6
HGRN_EXPAND = 128
HGRN_HEAD_DIM = 128
HGRN_WIDTH = HGRN_HEADS * HGRN_HEAD_DIM
HGRN_COLS = 4 * HGRN_WIDTH
HGRN_CHUNK = 16
LRU_BLOCKS = 8
LRU_BLOCK_DIM = 64
LRU_WIDTH = LRU_BLOCKS * LRU_BLOCK_DIM
LRU_CONV = 4
LRU_C = 8.0
LRU_COLS = 2 * LRU_WIDTH
MIX_WIDTH = RWKV_WIDTH + HGRN_WIDTH + LRU_WIDTH
IN_COLS = RWKV_COLS + HGRN_COLS + LRU_COLS
D_FF = 5632
FFN_CONV = 3
NORM_EPS = 1e-6

kernel_name = 'hybrid_rwkv7_hgrn2_rglru_convffn_step'


def rms_norm(x, w):
    xf = x.astype(jnp.float32)
    y = xf * lax.rsqrt(jnp.mean(xf * xf, axis=-1, keepdims=True) + NORM_EPS)
    return (y * w.astype(jnp.float32)).astype(x.dtype)


def causal_dwconv(x, buf, w, b):
    K = w.shape[0]
    T = x.shape[1]
    xp = jnp.concatenate([buf.astype(x.dtype), x], axis=1)
    y = b
    for j in range(K):
        y = y + w[j] * xp[:, j:j + T]
    return y, xp[:, T:]


def rwkv7_scan(S0, r, w, k, v, a, b):
    def step(S, inp):
        r_t, w_t, k_t, v_t, a_t, b_t = inp
        sa = jnp.einsum('bhvk,bhk->bhv', S, a_t)
        S = (S * w_t[:, :, None, :] + sa[..., None] * b_t[:, :, None, :]
             + v_t[..., None] * k_t[:, :, None, :])
        return S, jnp.einsum('bhvk,bhk->bhv', S, r_t)
    xs = tuple(jnp.moveaxis(t, 1, 0) for t in (r, w, k, v, a, b))
    S, ys = lax.scan(step, S0, xs)
    return jnp.moveaxis(ys, 0, 1), S


def rwkv7_mix(p, shift_prev, S0, lp):
    B, T, _ = p.shape
    H, N = RWKV_HEADS, RWKV_HEAD_DIM
    f32 = jnp.float32
    prev = jnp.concatenate([shift_prev[:, None].astype(p.dtype), p[:, :-1]], axis=1)
    xs = p + (prev - p) * lp['rwkv_mu']
    r, k, v, xw, xa, xg = jnp.split(xs, RWKV_SPLITS, axis=-1)
    w_log = -jax.nn.softplus(-(lp['rwkv_w0'] + jnp.tanh(xw) @ lp['rwkv_w2']).astype(f32)) - 0.5
    decay = jnp.exp(-jnp.exp(w_log))
    a = jax.nn.sigmoid((lp['rwkv_a0'] + xa @ lp['rwkv_a2']).astype(f32))
    g = jax.nn.sigmoid(xg) @ lp['rwkv_g2']
    kf = k.astype(f32)
    kk = (kf * lp['rwkv_k_k']).reshape(B, T, H, N)
    kk = kk / jnp.maximum(jnp.sqrt(jnp.sum(kk * kk, axis=-1, keepdims=True)), 1e-12)
    kf = kf * (1.0 + (a - 1.0) * lp['rwkv_k_a'])
    hd = lambda t: t.astype(f32).reshape(B, T, H, N)
    r_h, k_h, v_h, a_h, w_h = hd(r), hd(kf), hd(v), hd(a), hd(decay)
    y, S = rwkv7_scan(S0.astype(f32), r_h, w_h, k_h, v_h, -kk, kk * a_h)
    mean = jnp.mean(y, axis=-1, keepdims=True)
    var = jnp.mean(jnp.square(y - mean), axis=-1, keepdims=True)
    y = (y - mean) * lax.rsqrt(var + RWKV_LN_EPS)
    y = y.reshape(B, T, RWKV_WIDTH) * lp['rwkv_ln_w'] + lp['rwkv_ln_b']
    bonus = jnp.sum(r_h * k_h * lp['rwkv_r_k'], axis=-1, keepdims=True) * v_h
    y = (y + bonus.reshape(B, T, RWKV_WIDTH)) * g
    return y.astype(p.dtype), S.astype(p.dtype), p[:, -1]


def hgrn2_chunked(S0, q, k, v, log_f):
    B, T, H, DK = q.shape
    DV = v.shape[-1]
    C = math.gcd(T, HGRN_CHUNK)
    N = T // C
    to_chunks = lambda t: jnp.moveaxis(t.reshape(B, N, C, H, t.shape[-1]), 1, 0)
    causal = jnp.tril(jnp.ones((C, C), dtype=bool))[None, :, :, None, None]

    def step(S, inp):
        q_c, k_c, v_c, lf_c = inp
        b = jnp.cumsum(lf_c, axis=1)
        o_inter = jnp.einsum('bthk,bhkv->bthv', q_c * jnp.exp(b), S)
        dec = jnp.exp(jnp.where(causal, b[:, :, None] - b[:, None, :], -jnp.inf))
        A = jnp.einsum('bthk,btshk,bshk->btsh', q_c, dec, k_c)
        o = o_inter + jnp.einsum('btsh,bshv->bthv', A, v_c)
        b_last = b[:, -1]
        S = (jnp.exp(b_last)[..., None] * S
             + jnp.einsum('bshk,bshv->bhkv', k_c * jnp.exp(b_last[:, None] - b), v_c))
        return S, o

    S, o = lax.scan(step, S0, (to_chunks(q), to_chunks(k), to_chunks(v), to_chunks(log_f)))
    return jnp.moveaxis(o, 0, 1).reshape(B, T, H, DV), S


def hgrn2_mix(p, S0, lp):
    B, T, _ = p.shape
    f32 = jnp.float32
    q, f, i, g = jnp.split(p, 4, axis=-1)
    q = jax.nn.silu(q.astype(f32))
    lb = lp['hgrn_lb']
    fg = lb + (1.0 - lb) * jax.nn.sigmoid(f.astype(f32))
    log_f = jnp.log(fg)
    k = 1.0 - fg
    hd = lambda t, d: t.astype(f32).reshape(B, T, HGRN_HEADS, d)
    o, S = hgrn2_chunked(S0.astype(f32), hd(q, HGRN_EXPAND), hd(k, HGRN_EXPAND),
                         hd(i, HGRN_HEAD_DIM), hd(log_f, HGRN_EXPAND))
    o = rms_norm(o, lp['hgrn_norm_w'].reshape(HGRN_HEADS, HGRN_HEAD_DIM)).reshape(B, T, HGRN_WIDTH)
    o = o * jax.nn.silu(g.astype(f32))
    return o.astype(p.dtype), S.astype(p.dtype)


def rglru_mix(p, conv_buf, h0, lp):
    B, T, _ = p.shape
    f32 = jnp.float32
    xb, gate = jnp.split(p, 2, axis=-1)
    xc, new_buf = causal_dwconv(xb, conv_buf, lp['rglru_conv_w'], lp['rglru_conv_b'])
    xh = xc.reshape(B, T, LRU_BLOCKS, LRU_BLOCK_DIM)
    gr = jnp.einsum('bthi,hij->bthj', xh, lp['rglru_wa']).reshape(B, T, LRU_WIDTH) + lp['rglru_ba']
    gi = jnp.einsum('bthi,hij->bthj', xh, lp['rglru_wx']).reshape(B, T, LRU_WIDTH) + lp['rglru_bx']
    r = jax.nn.sigmoid(gr.astype(f32))
    i = jax.nn.sigmoid(gi.astype(f32))
    log_a = -LRU_C * r * jax.nn.softplus(-lp['rglru_lambda'].astype(f32))
    a = jnp.exp(log_a)
    u = jnp.sqrt(-jnp.expm1(2.0 * log_a)) * (i * xc.astype(f32))

    def combine(lhs, rhs):
        a1, b1 = lhs
        a2, b2 = rhs
        return a1 * a2, a2 * b1 + b2

    a_cum, h = lax.associative_scan(combine, (a, u), axis=1)
    h = h + a_cum * h0.astype(f32)[:, None]
    y = h * jax.nn.gelu(gate.astype(f32))
    return y.astype(p.dtype), new_buf, h[:, -1].astype(p.dtype)


def trunk_layer(x, st, lp):
    S_rw, shift_rw, S_hg, h_lru, buf_lru, buf_ffn = st
    hn = rms_norm(x, lp['norm_mix'])
    proj = hn @ lp['w_in']
    p_rw, p_hg, p_lru = jnp.split(proj, (RWKV_COLS, RWKV_COLS + HGRN_COLS), axis=-1)
    y_rw, S_rw, shift_rw = rwkv7_mix(p_rw, shift_rw, S_rw, lp)
    y_hg, S_hg = hgrn2_mix(p_hg, S_hg, lp)
    y_lru, buf_lru, h_lru = rglru_mix(p_lru, buf_lru, h_lru, lp)
    x = x + jnp.concatenate([y_rw, y_hg, y_lru], axis=-1) @ lp['w_out']
    hn = rms_norm(x, lp['norm_ffn'])
    gte, val = jnp.split(hn @ lp['ffn_w_up'], 2, axis=-1)
    gc, buf_ffn = causal_dwconv(gte, buf_ffn, lp['ffn_conv_w'], lp['ffn_conv_b'])
    x = x + (jax.nn.silu(gc) * val) @ lp['ffn_w_down']
    return x, (S_rw, shift_rw, S_hg, h_lru, buf_lru, buf_ffn)


def zero_states(batch, dtype):
    return (jnp.zeros((batch, RWKV_HEADS, RWKV_HEAD_DIM, RWKV_HEAD_DIM), dtype),
            jnp.zeros((batch, RWKV_COLS), dtype),
            jnp.zeros((batch, HGRN_HEADS, HGRN_EXPAND, HGRN_HEAD_DIM), dtype),
            jnp.zeros((batch, LRU_WIDTH), dtype),
            jnp.zeros((batch, LRU_CONV - 1, LRU_WIDTH), dtype),
            jnp.zeros((batch, FFN_CONV - 1, D_FF), dtype))


def setup_inputs(seed: int = 0) -> dict:
    key = jax.random.key(seed)
    ks = iter(jax.random.split(key, 48))
    f32 = jnp.float32
    nrm = lambda shape, scale: scale * jax.random.normal(next(ks), shape, f32)
    uni = lambda shape, lo, hi: jax.random.uniform(next(ks), shape, f32, minval=lo, maxval=hi)
    lam_a = uni((DEPTH, LRU_WIDTH), 0.9, 0.999)
    lam_s = lam_a ** (1.0 / LRU_C)
    return {
        'x_prompt': nrm((BATCH, SEQ, D_MODEL), 1.0),
        'x_sample': nrm((DEC_BATCH, DEC_SEQ, D_MODEL), 1.0),
        'state_rwkv': nrm((DEPTH, DEC_BATCH, RWKV_HEADS, RWKV_HEAD_DIM, RWKV_HEAD_DIM), 0.5),
        'state_rwkv_shift': nrm((DEPTH, DEC_BATCH, RWKV_COLS), 1.0),
        'state_hgrn': nrm((DEPTH, DEC_BATCH, HGRN_HEADS, HGRN_EXPAND, HGRN_HEAD_DIM), 0.5),
        'state_rglru': nrm((DEPTH, DEC_BATCH, LRU_WIDTH), 0.5),
        'cache_rglru_conv': nrm((DEPTH, DEC_BATCH, LRU_CONV - 1, LRU_WIDTH), 1.0),
        'cache_ffn_conv': nrm((DEPTH, DEC_BATCH, FFN_CONV - 1, D_FF), 1.0),
        'norm_mix': 1.0 + nrm((DEPTH, D_MODEL), 0.02),
        'w_in': nrm((DEPTH, D_MODEL, IN_COLS), D_MODEL ** -0.5),
        'rwkv_mu': uni((DEPTH, RWKV_COLS), 0.0, 1.0),
        'rwkv_w0': uni((DEPTH, RWKV_WIDTH), -6.0, 0.0),
        'rwkv_w2': nrm((DEPTH, RWKV_DECAY_RANK, RWKV_WIDTH), 0.5 * RWKV_DECAY_RANK ** -0.5),
        'rwkv_a0': nrm((DEPTH, RWKV_WIDTH), 0.1),
        'rwkv_a2': nrm((DEPTH, RWKV_AAA_RANK, RWKV_WIDTH), RWKV_AAA_RANK ** -0.5),
        'rwkv_g2': nrm((DEPTH, RWKV_GATE_RANK, RWKV_WIDTH), RWKV_GATE_RANK ** -0.5),
        'rwkv_k_k': 0.85 + nrm((DEPTH, RWKV_WIDTH), 0.05),
        'rwkv_k_a': 1.0 + nrm((DEPTH, RWKV_WIDTH), 0.05),
        'rwkv_r_k': nrm((DEPTH, RWKV_HEADS, RWKV_HEAD_DIM), 0.1),
        'rwkv_ln_w': 1.0 + nrm((DEPTH, RWKV_WIDTH), 0.02),
        'rwkv_ln_b': nrm((DEPTH, RWKV_WIDTH), 0.02),
        'hgrn_lb_logits': nrm((DEPTH, HGRN_WIDTH), 0.5),
        'hgrn_norm_w': 1.0 + nrm((DEPTH, HGRN_WIDTH), 0.02),
        'rglru_conv_w': nrm((DEPTH, LRU_CONV, LRU_WIDTH), LRU_CONV ** -0.5),
        'rglru_conv_b': nrm((DEPTH, LRU_WIDTH), 0.02),
        'rglru_wa': nrm((DEPTH, LRU_BLOCKS, LRU_BLOCK_DIM, LRU_BLOCK_DIM), LRU_BLOCK_DIM ** -0.5),
        'rglru_ba': nrm((DEPTH, LRU_WIDTH), 0.02),
        'rglru_wx': nrm((DEPTH, LRU_BLOCKS, LRU_BLOCK_DIM, LRU_BLOCK_DIM), LRU_BLOCK_DIM ** -0.5),
        'rglru_bx': nrm((DEPTH, LRU_WIDTH), 0.02),
        'rglru_lambda': jnp.log(lam_s) - jnp.log1p(-lam_s),
        'w_out': nrm((DEPTH, MIX_WIDTH, D_MODEL), MIX_WIDTH ** -0.5),
        'norm_ffn': 1.0 + nrm((DEPTH, D_MODEL), 0.02),
        'ffn_w_up': nrm((DEPTH, D_MODEL, 2 * D_FF), D_MODEL ** -0.5),
        'ffn_conv_w': nrm((DEPTH, FFN_CONV, D_FF), FFN_CONV ** -0.5),
        'ffn_conv_b': nrm((DEPTH, D_FF), 0.02),
        'ffn_w_down': nrm((DEPTH, D_FF, D_MODEL), D_FF ** -0.5),
        'norm_final': 1.0 + nrm((D_MODEL,), 0.02),
    }


def reference(x_prompt, x_sample, state_rwkv, state_rwkv_shift, state_hgrn, state_rglru,
              cache_rglru_conv, cache_ffn_conv, norm_mix, w_in, rwkv_mu, rwkv_w0, rwkv_w2,
              rwkv_a0, rwkv_a2, rwkv_g2, rwkv_k_k, rwkv_k_a, rwkv_r_k, rwkv_ln_w, rwkv_ln_b,
              hgrn_lb_logits, hgrn_norm_w, rglru_conv_w, rglru_conv_b, rglru_wa, rglru_ba,
              rglru_wx, rglru_bx, rglru_lambda, w_out, norm_ffn, ffn_w_up, ffn_conv_w,
              ffn_conv_b, ffn_w_down, norm_final):
    gam = jax.nn.softmax(hgrn_lb_logits.astype(jnp.float32), axis=0)
    lower_bounds = jnp.cumsum(gam, axis=0) - gam[0]
    x_p, x_s = x_prompt, x_sample
    new_p, new_s = [], []
    for l in range(DEPTH):
        lp = {
            'norm_mix': norm_mix[l], 'w_in': w_in[l], 'rwkv_mu': rwkv_mu[l],
            'rwkv_w0': rwkv_w0[l], 'rwkv_w2': rwkv_w2[l], 'rwkv_a0': rwkv_a0[l],
            'rwkv_a2': rwkv_a2[l], 'rwkv_g2': rwkv_g2[l], 'rwkv_k_k': rwkv_k_k[l],
            'rwkv_k_a': rwkv_k_a[l], 'rwkv_r_k': rwkv_r_k[l], 'rwkv_ln_w': rwkv_ln_w[l],
            'rwkv_ln_b': rwkv_ln_b[l], 'hgrn_lb': lower_bounds[l], 'hgrn_norm_w': hgrn_norm_w[l],
            'rglru_conv_w': rglru_conv_w[l], 'rglru_conv_b': rglru_conv_b[l],
            'rglru_wa': rglru_wa[l], 'rglru_ba': rglru_ba[l], 'rglru_wx': rglru_wx[l],
            'rglru_bx': rglru_bx[l], 'rglru_lambda': rglru_lambda[l], 'w_out': w_out[l],
            'norm_ffn': norm_ffn[l], 'ffn_w_up': ffn_w_up[l], 'ffn_conv_w': ffn_conv_w[l],
            'ffn_conv_b': ffn_conv_b[l], 'ffn_w_down': ffn_w_down[l],
        }
        x_p, sp = trunk_layer(x_p, zero_states(x_p.shape[0], x_p.dtype), lp)
        x_s, ss = trunk_layer(x_s, (state_rwkv[l], state_rwkv_shift[l], state_hgrn[l],
                                    state_rglru[l], cache_rglru_conv[l], cache_ffn_conv[l]), lp)
        new_p.append(sp)
        new_s.append(ss)
    y_prompt = rms_norm(x_p, norm_final)
    y_sample = rms_norm(x_s, norm_final)
    p_rwkv, p_rwkv_shift, p_hgrn, p_rglru, p_rglru_conv, p_ffn_conv = [
        jnp.stack(s, axis=0) for s in zip(*new_p)]
    s_rwkv, s_rwkv_shift, s_hgrn, s_rglru, s_rglru_conv, s_ffn_conv = [
        jnp.stack(s, axis=0) for s in zip(*new_s)]
    return (y_prompt, y_sample,
            p_rwkv, p_rwkv_shift, p_hgrn, p_rglru, p_rglru_conv, p_ffn_conv,
            s_rwkv, s_rwkv_shift, s_hgrn, s_rglru, s_rglru_conv, s_ffn_conv)
```

```python
import functools

import numpy as np
import jax
import jax.numpy as jnp
from jax import lax
from jax.experimental import pallas as pl
from jax.experimental.pallas import tpu as pltpu

F32 = jnp.float32
BF16 = jnp.bfloat16

D_MODEL = 2048
RWKV_HEADS = 12
RWKV_N = 64
RWKV_W = RWKV_HEADS * RWKV_N
RWKV_COLS = 2560
RWKV_LN_EPS = 64e-5
HGRN_HEADS = 6
HGRN_D = 128
HGRN_W = HGRN_HEADS * HGRN_D
HGRN_COL0 = RWKV_COLS
LRU_W = 512
LRU_COL0 = RWKV_COLS + 4 * HGRN_W
LRU_C = 8.0
IN_COLS = 6656
D_FF = 5632
NORM_EPS = 1e-6

SUBLANES = 8
LANES = 128
RWKV_CHUNK = 64
HGRN_CHUNK = 128
HGRN_LEVELS = 7
VMEM_LIMIT = 48 * 1024 * 1024


def _cp(n):
    return pltpu.CompilerParams(dimension_semantics=("arbitrary",) * n,
                                vmem_limit_bytes=VMEM_LIMIT)


def _dot(a, b):
    return jnp.dot(a, b, preferred_element_type=F32)


def _dot_nt(a, b):
    return lax.dot_general(a, b, (((1,), (1,)), ((), ())), preferred_element_type=F32)


def _split(x):
    hi = x.astype(BF16)
    lo = (x - hi.astype(F32)).astype(BF16)
    return hi, lo


def _split_dot(x, m):
    hi, lo = _split(x)
    return _dot(hi, m) + _dot(lo, m)


def _dot3(a, b, nt=False):
    f = _dot_nt if nt else _dot
    ah, al = _split(a)
    bh, bl = _split(b)
    return f(ah, bh) + (f(ah, bl) + f(al, bh))


def _softplus(z):
    return jnp.maximum(z, 0.0) + jnp.log1p(jnp.exp(-jnp.abs(z)))


def _expm1(z):
    return jnp.tanh(0.5 * z) * (jnp.exp(z) + 1.0)


def _prev_rows(x, hist, ks, long_seq):
    tm, c = x.shape
    if long_seq:
        ext = jnp.concatenate([hist, x], axis=0)
        return [pltpu.roll(ext, k, 0)[SUBLANES:] for k in ks]
    x3 = x.reshape(tm // SUBLANES, SUBLANES, c)
    h3 = hist.reshape(tm // SUBLANES, SUBLANES, c)
    pos = lax.broadcasted_iota(jnp.int32, x3.shape, 1)
    return [jnp.where(pos >= k, pltpu.roll(x3, k, 1), pltpu.roll(h3, k, 1)).reshape(tm, c)
            for k in ks]


def _in_proj_body(x_ref, nw_ref, w_ref, o_ref, xn_ref):
    @pl.when(pl.program_id(1) == 0)
    def _():
        x = x_ref[...]
        ms = jnp.mean(x * x, axis=-1, keepdims=True)
        xn_ref[...] = (x * lax.rsqrt(ms + NORM_EPS) * nw_ref[...]).astype(BF16)
    o_ref[...] = _dot(xn_ref[...], w_ref[...])


def _in_proj(x, nw, w, tm, tn):
    n, d = x.shape
    c = w.shape[1]
    return pl.pallas_call(
        _in_proj_body, grid=(n // tm, c // tn),
        in_specs=[pl.BlockSpec((tm, d), lambda i, j: (i, 0)),
                  pl.BlockSpec((1, d), lambda i, j: (0, 0)),
                  pl.BlockSpec((d, tn), lambda i, j: (0, j))],
        out_specs=pl.BlockSpec((tm, tn), lambda i, j: (i, j)),
        out_shape=jax.ShapeDtypeStruct((n, c), F32),
        scratch_shapes=[pltpu.VMEM((tm, d), BF16)],
        compiler_params=_cp(2), name="in_proj")(x, nw, w)


def _out_proj_body(x_ref, ya_ref, yb_ref, yc_ref, w_ref, o_ref, y_ref):
    @pl.when(pl.program_id(1) == 0)
    def _():
        y_ref[:, 0:RWKV_W] = ya_ref[...].astype(BF16)
        y_ref[:, RWKV_W:RWKV_W + HGRN_W] = yb_ref[...].astype(BF16)
        y_ref[:, RWKV_W + HGRN_W:] = yc_ref[...].astype(BF16)
    o_ref[...] = x_ref[...] + _dot(y_ref[...], w_ref[...])


def _out_proj(x, ya, yb, yc, w, tm, tn):
    n, d = x.shape
    return pl.pallas_call(
        _out_proj_body, grid=(n // tm, d // tn),
        in_specs=[pl.BlockSpec((tm, tn), lambda i, j: (i, j)),
                  pl.BlockSpec((tm, RWKV_W), lambda i, j: (i, 0)),
                  pl.BlockSpec((tm, HGRN_W), lambda i, j: (i, 0)),
                  pl.BlockSpec((tm, LRU_W), lambda i, j: (i, 0)),
                  pl.BlockSpec((d, tn), lambda i, j: (0, j))],
        out_specs=pl.BlockSpec((tm, tn), lambda i, j: (i, j)),
        out_shape=jax.ShapeDtypeStruct((n, d), F32),
        scratch_shapes=[pltpu.VMEM((tm, d), BF16)],
        compiler_params=_cp(2), name="out_proj")(x, ya, yb, yc, w)


def _ffn_body(x_ref, nw_ref, wg_ref, wv_ref, cw_ref, cb_ref, wd_ref, hist_ref,
              o_ref, st_ref, hn_ref, acc_ref, carry_ref, *, long_seq, tps):
    i = pl.program_id(0)
    j = pl.program_id(1)

    @pl.when(j == 0)
    def _():
        x = x_ref[...]
        ms = jnp.mean(x * x, axis=-1, keepdims=True)
        hn_ref[...] = (x * lax.rsqrt(ms + NORM_EPS) * nw_ref[...]).astype(BF16)
        acc_ref[...] = x

    hn = hn_ref[...]
    g = _dot(hn, wg_ref[...])
    v = _dot(hn, wv_ref[...])
    if long_seq:
        @pl.when(i % tps == 0)
        def _():
            carry_ref[j] = hist_ref[...]
        hist = carry_ref[j]
    else:
        hist = hist_ref[...]
    p1, p2 = _prev_rows(g, hist, (1, 2), long_seq)
    cw = cw_ref[...]
    gc = cb_ref[...] + cw[0:1] * p2 + cw[1:2] * p1 + cw[2:3] * g
    h = (gc * jax.nn.sigmoid(gc)) * v
    acc_ref[...] += _dot(h.astype(BF16), wd_ref[...])
    if long_seq:
        carry_ref[j] = g[g.shape[0] - SUBLANES:]
        st_ref[...] = g[g.shape[0] - SUBLANES:]
    else:
        st_ref[...] = g

    @pl.when(j == pl.num_programs(1) - 1)
    def _():
        o_ref[...] = acc_ref[...]


def _ffn(x, nw, w_up, cw, cb, w_down, hist, seq_len, tm, tf):
    n, d = x.shape
    nf = D_FF // tf
    long_seq = seq_len > SUBLANES
    tps = max(seq_len // tm, 1)
    if long_seq:
        hist_spec = pl.BlockSpec((SUBLANES, tf), lambda i, j: (i // tps, j))
        st_spec = pl.BlockSpec((SUBLANES, tf), lambda i, j: (i // tps, j))
    else:
        hist_spec = pl.BlockSpec((tm, tf), lambda i, j: (i, j))
        st_spec = pl.BlockSpec((tm, tf), lambda i, j: (i, j))
    body = functools.partial(_ffn_body, long_seq=long_seq, tps=tps)
    return pl.pallas_call(
        body, grid=(n // tm, nf),
        in_specs=[pl.BlockSpec((tm, d), lambda i, j: (i, 0)),
                  pl.BlockSpec((1, d), lambda i, j: (0, 0)),
                  pl.BlockSpec((d, tf), lambda i, j: (0, j)),
                  pl.BlockSpec((d, tf), lambda i, j: (0, j + nf)),
                  pl.BlockSpec((3, tf), lambda i, j: (0, j)),
                  pl.BlockSpec((1, tf), lambda i, j: (0, j)),
                  pl.BlockSpec((tf, d), lambda i, j: (j, 0)),
                  hist_spec],
        out_specs=[pl.BlockSpec((tm, d), lambda i, j: (i, 0)), st_spec],
        out_shape=[jax.ShapeDtypeStruct((n, d), F32),
                   jax.ShapeDtypeStruct(hist.shape, F32)],
        scratch_shapes=[pltpu.VMEM((tm, d), BF16), pltpu.VMEM((tm, d), F32),
                        pltpu.VMEM((nf, SUBLANES, tf), F32)],
        compiler_params=_cp(2), name="ffn")(x, nw, w_up, w_up, cw, cb, w_down, hist)


def _final_norm_body(x_ref, nw_ref, o_ref):
    x = x_ref[...]
    ms = jnp.mean(x * x, axis=-1, keepdims=True)
    o_ref[...] = x * lax.rsqrt(ms + NORM_EPS) * nw_ref[...]


def _final_norm(x, nw, tm):
    n, d = x.shape
    return pl.pallas_call(
        _final_norm_body, grid=(n // tm,),
        in_specs=[pl.BlockSpec((tm, d), lambda i: (i, 0)), pl.BlockSpec((1, d), lambda i: (0, 0))],
        out_specs=pl.BlockSpec((tm, d), lambda i: (i, 0)),
        out_shape=jax.ShapeDtypeStruct((n, d), F32),
        compiler_params=_cp(1), name="final_norm")(x, nw)


def _rwkv_pre_body(p_ref, hist_ref, mu_ref, w0_ref, w2_ref, a0_ref, a2_ref, g2_ref, kk_ref,
                   ka_ref, rk_ref, ones_ref,
                   r_ref, w_ref, k_ref, v_ref, na_ref, kb_ref, g_ref, bonus_ref, st_ref,
                   carry_ref, *, long_seq, tps):
    p = p_ref[...]
    if long_seq:
        @pl.when(pl.program_id(0) % tps == 0)
        def _():
            carry_ref[...] = hist_ref[...]
        hist = carry_ref[...]
    else:
        hist = hist_ref[...]
    (prev,) = _prev_rows(p, hist, (1,), long_seq)
    if long_seq:
        carry_ref[...] = p[p.shape[0] - SUBLANES:]
        st_ref[...] = p[p.shape[0] - SUBLANES:]
    else:
        st_ref[...] = p
    xs = p + (prev - p) * mu_ref[...]
    r = xs[:, 0:RWKV_W]
    k = xs[:, RWKV_W:2 * RWKV_W]
    v = xs[:, 2 * RWKV_W:3 * RWKV_W]
    xwa = xs[:, 3 * RWKV_W:3 * RWKV_W + 128]
    xg = xs[:, 3 * RWKV_W + 128:RWKV_COLS]
    zw = w0_ref[...] + _dot(jnp.tanh(xwa).astype(BF16), w2_ref[...])
    w_log = -_softplus(-zw) - 0.5
    decay = jnp.exp(-jnp.exp(w_log))
    a = jax.nn.sigmoid(a0_ref[...] + _dot(xwa.astype(BF16), a2_ref[...]))
    g = _dot(jax.nn.sigmoid(xg).astype(BF16), g2_ref[...])
    ones = ones_ref[...]
    kk = k * kk_ref[...]
    kk = kk / jnp.maximum(jnp.sqrt(_split_dot(kk * kk, ones)), 1e-12)
    kf = k * (1.0 + (a - 1.0) * ka_ref[...])
    r_ref[...] = r
    w_ref[...] = decay
    k_ref[...] = kf
    v_ref[...] = v
    na_ref[...] = -kk
    kb_ref[...] = kk * a
    g_ref[...] = g
    bonus_ref[...] = _split_dot(r * kf * rk_ref[...], ones) * v


def _rwkv_pre(proj, hist, lp, ones, seq_len, tm):
    n = proj.shape[0]
    long_seq = seq_len > SUBLANES
    tps = max(seq_len // tm, 1)
    row = lambda c: pl.BlockSpec((1, c), lambda i: (0, 0))
    full = lambda a, b: pl.BlockSpec((a, b), lambda i: (0, 0))
    if long_seq:
        hist_spec = pl.BlockSpec((SUBLANES, RWKV_COLS), lambda i: (i // tps, 0))
    else:
        hist_spec = pl.BlockSpec((tm, RWKV_COLS), lambda i: (i, 0))
    tile = pl.BlockSpec((tm, RWKV_W), lambda i: (i, 0))
    body = functools.partial(_rwkv_pre_body, long_seq=long_seq, tps=tps)
    outs = pl.pallas_call(
        body, grid=(n // tm,),
        in_specs=[pl.BlockSpec((tm, RWKV_COLS), lambda i: (i, 0)), hist_spec,
                  row(RWKV_COLS), row(RWKV_W), full(128, RWKV_W), row(RWKV_W),
                  full(128, RWKV_W), full(128, RWKV_W), row(RWKV_W), row(RWKV_W), row(RWKV_W),
                  full(RWKV_W, RWKV_W)],
        out_specs=[tile] * 8 + [hist_spec],
        out_shape=[jax.ShapeDtypeStruct((n, RWKV_W), F32)] * 8
                  + [jax.ShapeDtypeStruct(hist.shape, F32)],
        scratch_shapes=[pltpu.VMEM((SUBLANES, RWKV_COLS), F32)],
        compiler_params=_cp(1), name="rwkv_pre")(
            proj, hist, lp["mu"], lp["w0"], lp["w2p"], lp["a0"], lp["a2p"], lp["g2"],
            lp["k_k"], lp["k_a"], lp["r_k"], ones)
    return outs


def _rwkv_post_body(y_ref, bonus_ref, g_ref, lw_ref, lb_ref, ones_ref, o_ref):
    y = y_ref[...]
    ones = ones_ref[...]
    mean = _split_dot(y, ones) * (1.0 / RWKV_N)
    d = y - mean
    var = _split_dot(d * d, ones) * (1.0 / RWKV_N)
    yn = d * lax.rsqrt(var + RWKV_LN_EPS)
    o_ref[...] = (yn * lw_ref[...] + lb_ref[...] + bonus_ref[...]) * g_ref[...]


def _rwkv_post(y, bonus, g, lw, lb, ones, tm):
    n = y.shape[0]
    tile = pl.BlockSpec((tm, RWKV_W), lambda i: (i, 0))
    row = pl.BlockSpec((1, RWKV_W), lambda i: (0, 0))
    return pl.pallas_call(
        _rwkv_post_body, grid=(n // tm,),
        in_specs=[tile, tile, tile, row, row, pl.BlockSpec((RWKV_W, RWKV_W), lambda i: (0, 0))],
        out_specs=tile, out_shape=jax.ShapeDtypeStruct((n, RWKV_W), F32),
        compiler_params=_cp(1), name="rwkv_post")(y, bonus, g, lw, lb, ones)


def _rwkv_lanes_body(*refs, aug):
    r_ref, w_ref, k_ref, v_ref, a_ref, b_ref = refs[:6]
    if aug:
        ya_ref, yb_ref, sf_ref, s_ref, xt_ref, ya_buf, yb_buf = refs[6:]
        nrows = 2 * RWKV_N
    else:
        s0_ref, yb_ref, sf_ref, s_ref, xt_ref, yb_buf = refs[6:]
        nrows = RWKV_N
    tb = pl.program_id(1)
    ninst = r_ref.shape[1]

    @pl.when(tb == 0)
    def _():
        for hh in range(2):
            if aug:
                shp = (nrows, RWKV_N, ninst)
                eye = lax.broadcasted_iota(jnp.int32, shp, 0) == lax.broadcasted_iota(jnp.int32, shp, 1)
                s_ref[hh] = jnp.where(eye, 1.0, 0.0).astype(F32)
            else:
                s_ref[hh] = s0_ref[hh].T.reshape(nrows, RWKV_N, ninst)

    def step(tl, carry):
        for idx, ref in enumerate((r_ref, w_ref, k_ref, v_ref, a_ref, b_ref)):
            xt_ref[idx] = ref[tl].T
        for hh in range(2):
            base = hh * RWKV_N

            def row(rho, c, with_v, vbase, ybuf, ybase):
                s = s_ref[hh, rho]
                sa = jnp.sum(s * xt_ref[4, base:base + RWKV_N, :], axis=0, keepdims=True)
                s = s * xt_ref[1, base:base + RWKV_N, :] + sa * xt_ref[5, base:base + RWKV_N, :]
                if with_v:
                    vrow = xt_ref[3, pl.ds(vbase + rho, 1), :]
                    s = s + vrow * xt_ref[2, base:base + RWKV_N, :]
                s_ref[hh, rho] = s
                ybuf[pl.ds(ybase + rho, 1), :] = jnp.sum(
                    s * xt_ref[0, base:base + RWKV_N, :], axis=0, keepdims=True)
                return c

            if aug:
                lax.fori_loop(0, RWKV_N, functools.partial(
                    row, with_v=False, vbase=0, ybuf=ya_buf, ybase=base), 0, unroll=4)
                lax.fori_loop(RWKV_N, nrows, functools.partial(
                    row, with_v=True, vbase=base - RWKV_N, ybuf=yb_buf, ybase=base - RWKV_N),
                    0, unroll=4)
            else:
                lax.fori_loop(0, nrows, functools.partial(
                    row, with_v=True, vbase=base, ybuf=yb_buf, ybase=base), 0, unroll=4)
        if aug:
            ya_ref[tl] = ya_buf[...].T
        yb_ref[tl] = yb_buf[...].T
        return carry

    lax.fori_loop(0, SUBLANES, step, 0)

    @pl.when(tb == pl.num_programs(1) - 1)
    def _():
        for hh in range(2):
            sf_ref[hh] = s_ref[hh].reshape(nrows * RWKV_N, ninst).T


def _rwkv_lanes(xs, s0):
    steps, ninst, _ = xs[0].shape
    aug = s0 is None
    nrows = 2 * RWKV_N if aug else RWKV_N
    xspec = pl.BlockSpec((SUBLANES, ninst, 2 * RWKV_N), lambda h, t: (t, 0, h))
    sspec = pl.BlockSpec((2, ninst, nrows * RWKV_N), lambda h, t: (h, 0, 0))
    yshape = jax.ShapeDtypeStruct((steps, ninst, RWKV_W), F32)
    sshape = jax.ShapeDtypeStruct((RWKV_HEADS, ninst, nrows * RWKV_N), F32)
    scratch = [pltpu.VMEM((2, nrows, RWKV_N, ninst), F32),
               pltpu.VMEM((6, 2 * RWKV_N, ninst), F32)]
    if aug:
        in_specs, args = [xspec] * 6, xs
        out_specs, out_shape = [xspec, xspec, sspec], [yshape, yshape, sshape]
        scratch += [pltpu.VMEM((2 * RWKV_N, ninst), F32)] * 2
    else:
        in_specs, args = [xspec] * 6 + [sspec], list(xs) + [s0]
        out_specs, out_shape = [xspec, sspec], [yshape, sshape]
        scratch += [pltpu.VMEM((2 * RWKV_N, ninst), F32)]
    return pl.pallas_call(
        functools.partial(_rwkv_lanes_body, aug=aug),
        grid=(RWKV_HEADS // 2, steps // SUBLANES),
        in_specs=in_specs, out_specs=out_specs, out_shape=out_shape, scratch_shapes=scratch,
        compiler_params=_cp(2), name="rwkv_lanes_chunk" if aug else "rwkv_lanes_state")(*args)


def _rwkv_combine_body(m_ref, z_ref, ra_ref, yb_ref, y_ref, sf_ref, s_ref):
    c = pl.program_id(1)

    @pl.when(c == 0)
    def _():
        s_ref[...] = jnp.zeros_like(s_ref)

    for h in range(RWKV_HEADS):
        s = s_ref[h]
        y_ref[h] = _dot3(ra_ref[h], s, nt=True) + yb_ref[h]
        s_ref[h] = _dot3(s, m_ref[h]) + z_ref[h]

    @pl.when(c == pl.num_programs(1) - 1)
    def _():
        sf_ref[...] = s_ref[...]


def _rwkv_combine(mz, ra, yb, nseq):
    ninst = mz.shape[0]
    nch = ninst // nseq
    blk = (None, RWKV_HEADS, RWKV_N, RWKV_N)
    inst = lambda b, c: (b * nch + c, 0, 0, 0)
    return pl.pallas_call(
        _rwkv_combine_body, grid=(nseq, nch),
        in_specs=[pl.BlockSpec(blk, inst),
                  pl.BlockSpec(blk, lambda b, c: (b * nch + c, 0, 1, 0)),
                  pl.BlockSpec(blk, inst), pl.BlockSpec(blk, inst)],
        out_specs=[pl.BlockSpec(blk, inst), pl.BlockSpec(blk, lambda b, c: (b, 0, 0, 0))],
        out_shape=[jax.ShapeDtypeStruct((ninst, RWKV_HEADS, RWKV_CHUNK, RWKV_N), F32),
                   jax.ShapeDtypeStruct((nseq, RWKV_HEADS, RWKV_N, RWKV_N), F32)],
        scratch_shapes=[pltpu.VMEM((RWKV_HEADS, RWKV_N, RWKV_N), F32)],
        compiler_params=_cp(2), name="rwkv_combine")(mz, mz, ra, yb)


def _hgrn_lower_bound(logits, layer):
    m = jnp.max(logits, axis=0, keepdims=True)
    e = jnp.exp(logits - m)
    gam = e / jnp.sum(e, axis=0, keepdims=True)
    cs = gam[0:1]
    for i in range(1, layer + 1):
        cs = cs + gam[i:i + 1]
    return cs - gam[0:1]


def _hgrn_level_matrix():
    t = np.arange(HGRN_CHUNK)[:, None]
    s = np.arange(HGRN_CHUNK)[None, :]
    mats = []
    for upper in (False, True):
        for e in range(1, HGRN_LEVELS + 1):
            same = (t >> e) == (s >> e)
            mats.append(same & ((s > t) if upper else (s <= t)))
    return jnp.asarray(np.concatenate(mats, axis=0).astype(np.float32), dtype=BF16)


def _hgrn_prompt_body(q_ref, f_ref, i_ref, g_ref, lg_ref, nw_ref, cm_ref, o_ref, sf_ref, s_ref,
                      *, layer):
    c = pl.program_id(2)

    @pl.when(c == 0)
    def _():
        s_ref[...] = jnp.zeros_like(s_ref)

    n = HGRN_CHUNK
    t = lax.broadcasted_iota(jnp.int32, (n, n), 0)
    s_ = lax.broadcasted_iota(jnp.int32, (n, n), 1)
    lb_all = _hgrn_lower_bound(lg_ref[...], layer)
    cm = cm_ref[...]
    ones = jnp.ones((n, n), BF16)
    for hh in range(2):
        sl = slice(hh * HGRN_D, (hh + 1) * HGRN_D)
        lb = lb_all[:, sl]
        qr = q_ref[:, sl]
        q = qr * jax.nn.sigmoid(qr)
        fg = lb + (1.0 - lb) * jax.nn.sigmoid(f_ref[:, sl])
        lf = jnp.log(fg)
        k = 1.0 - fg
        v = i_ref[:, sl]
        vb = v.astype(BF16)
        cs = _split_dot_left(cm, lf)
        cl = lambda e: cs[(e - 1) * n:e * n]
        cu = lambda e: cs[(HGRN_LEVELS + e - 1) * n:(HGRN_LEVELS + e) * n]
        a = jnp.where(t == s_, _dot_nt(q.astype(BF16), k.astype(BF16)), 0.0)
        for e in range(HGRN_LEVELS):
            qe = q * (fg if e == 0 else jnp.exp(cl(e)))
            ke = k if e == 0 else k * jnp.exp(cu(e))
            mask = (((t >> e) & 1) == 1) & ((s_ >> e) == ((t >> e) - 1))
            a = a + jnp.where(mask, _dot_nt(qe.astype(BF16), ke.astype(BF16)), 0.0)
        qf = q * jnp.exp(cl(HGRN_LEVELS))
        kf = k * jnp.exp(cu(HGRN_LEVELS))
        st = s_ref[hh]
        o = _dot(a.astype(BF16), vb) + _dot(qf.astype(BF16), st.astype(BF16))
        dk = jnp.exp(_split_dot(lf.T, ones))
        s_ref[hh] = dk * st + _dot(kf.T.astype(BF16), vb)
        ms = jnp.mean(o * o, axis=-1, keepdims=True)
        gr = g_ref[:, sl]
        o_ref[:, sl] = (o * lax.rsqrt(ms + NORM_EPS) * nw_ref[:, sl]) * (gr * jax.nn.sigmoid(gr))

    @pl.when(c == pl.num_programs(2) - 1)
    def _():
        sf_ref[...] = s_ref[...]


def _split_dot_left(m, x):
    hi, lo = _split(x)
    return _dot(m, hi) + _dot(m, lo)


def _hgrn_prompt(proj, logits, nw, cmat, nseq, seq_len, layer):
    n = proj.shape[0]
    nch = seq_len // HGRN_CHUNK
    wb = 2 * HGRN_D
    col = lambda off: pl.BlockSpec((HGRN_CHUNK, wb),
                                   lambda p, b, c, off=off: (b * nch + c, off // wb + p))
    return pl.pallas_call(
        functools.partial(_hgrn_prompt_body, layer=layer),
        grid=(HGRN_HEADS // 2, nseq, nch),
        in_specs=[col(HGRN_COL0), col(HGRN_COL0 + HGRN_W), col(HGRN_COL0 + 2 * HGRN_W),
                  col(HGRN_COL0 + 3 * HGRN_W),
                  pl.BlockSpec((logits.shape[0], wb), lambda p, b, c: (0, p)),
                  pl.BlockSpec((1, wb), lambda p, b, c: (0, p)),
                  pl.BlockSpec(cmat.shape, lambda p, b, c: (0, 0))],
        out_specs=[pl.BlockSpec((HGRN_CHUNK, wb), lambda p, b, c: (b * nch + c, p)),
                   pl.BlockSpec((None, 2, HGRN_D, HGRN_D), lambda p, b, c: (b, p, 0, 0))],
        out_shape=[jax.ShapeDtypeStruct((n, HGRN_W), F32),
                   jax.ShapeDtypeStruct((nseq, HGRN_HEADS, HGRN_D, HGRN_D), F32)],
        scratch_shapes=[pltpu.VMEM((2, HGRN_D, HGRN_D), F32)],
        compiler_params=_cp(3), name="hgrn_prompt")(proj, proj, proj, proj, logits, nw, cmat)


HGRN_KT = 32


def _hgrn_lanes_body(q_ref, f_ref, i_ref, g_ref, lg_ref, nw_ref, s0_ref, o_ref, sf_ref,
                     s_ref, qt_ref, ft_ref, kt_ref, vt_ref, oacc_ref, *, layer):
    kt = pl.program_id(1)
    ninst = q_ref.shape[1]
    lb = _hgrn_lower_bound(lg_ref[...], layer)
    s_ref[...] = s0_ref[0].T.reshape(HGRN_KT, HGRN_D, ninst)

    @pl.when(kt == 0)
    def _():
        oacc_ref[...] = jnp.zeros_like(oacc_ref)

    for tl in range(SUBLANES):
        qr = q_ref[tl]
        fg = lb + (1.0 - lb) * jax.nn.sigmoid(f_ref[tl])
        qt_ref[tl] = (qr * jax.nn.sigmoid(qr)).T
        ft_ref[tl] = fg.T
        kt_ref[tl] = (1.0 - fg).T
        vt_ref[tl] = i_ref[tl].T

    def step(tl, carry):
        vt = vt_ref[tl]

        def krow(kk, o):
            row = kt * HGRN_KT + kk
            s = ft_ref[tl, pl.ds(row, 1), :] * s_ref[kk] + kt_ref[tl, pl.ds(row, 1), :] * vt
            s_ref[kk] = s
            return o + qt_ref[tl, pl.ds(row, 1), :] * s

        oacc_ref[tl] = lax.fori_loop(0, HGRN_KT, krow, oacc_ref[tl], unroll=2)
        return carry

    lax.fori_loop(0, SUBLANES, step, 0)
    sf_ref[0] = s_ref[...].reshape(HGRN_KT * HGRN_D, ninst).T

    @pl.when(kt == pl.num_programs(1) - 1)
    def _():
        for tl in range(SUBLANES):
            o = oacc_ref[tl].T
            ms = jnp.mean(o * o, axis=-1, keepdims=True)
            gr = g_ref[tl]
            o_ref[tl] = (o * lax.rsqrt(ms + NORM_EPS) * nw_ref[...]) * (gr * jax.nn.sigmoid(gr))


def _hgrn_lanes(proj_t, logits, nw, s0, layer):
    steps, ninst, _ = proj_t.shape
    col = lambda off: pl.BlockSpec((steps, ninst, HGRN_D),
                                   lambda h, k, off=off: (0, 0, off // HGRN_D + h))
    sspec = pl.BlockSpec((1, ninst, HGRN_KT * HGRN_D), lambda h, k: (h, 0, k))
    tbuf = pltpu.VMEM((steps, HGRN_D, ninst), F32)
    return pl.pallas_call(
        functools.partial(_hgrn_lanes_body, layer=layer),
        grid=(HGRN_HEADS, HGRN_D // HGRN_KT),
        in_specs=[col(HGRN_COL0), col(HGRN_COL0 + HGRN_W), col(HGRN_COL0 + 2 * HGRN_W),
                  col(HGRN_COL0 + 3 * HGRN_W),
                  pl.BlockSpec((logits.shape[0], HGRN_D), lambda h, k: (0, h)),
                  pl.BlockSpec((1, HGRN_D), lambda h, k: (0, h)), sspec],
        out_specs=[pl.BlockSpec((steps, ninst, HGRN_D), lambda h, k: (0, 0, h)), sspec],
        out_shape=[jax.ShapeDtypeStruct((steps, ninst, HGRN_W), F32),
                   jax.ShapeDtypeStruct(s0.shape, F32)],
        scratch_shapes=[pltpu.VMEM((HGRN_KT, HGRN_D, ninst), F32), tbuf, tbuf, tbuf, tbuf, tbuf],
        compiler_params=_cp(2), name="hgrn_lanes")(
            proj_t, proj_t, proj_t, proj_t, logits, nw, s0)


def _rglru_body(xb_ref, gate_ref, chist_ref, hinit_ref, cw_ref, cb_ref, wa_ref, ba_ref, wx_ref,
                bx_ref, lam_ref, y_ref, cst_ref, hst_ref, ccarry_ref, hcarry_ref,
                *, long_seq, tps):
    x = xb_ref[...]
    tm, c = x.shape
    if long_seq:
        @pl.when(pl.program_id(0) % tps == 0)
        def _():
            ccarry_ref[...] = chist_ref[...]
            hcarry_ref[...] = hinit_ref[...]
        hist = ccarry_ref[...]
    else:
        hist = chist_ref[...]
    p1, p2, p3 = _prev_rows(x, hist, (1, 2, 3), long_seq)
    cw = cw_ref[...]
    xc = cb_ref[...] + cw[0:1] * p3 + cw[1:2] * p2 + cw[2:3] * p1 + cw[3:4] * x
    xcb = xc.astype(BF16)
    r = jax.nn.sigmoid(_dot(xcb, wa_ref[...]) + ba_ref[...])
    ig = jax.nn.sigmoid(_dot(xcb, wx_ref[...]) + bx_ref[...])
    log_a = (-LRU_C) * r * _softplus(-lam_ref[...])
    a = jnp.exp(log_a)
    h = jnp.sqrt(-_expm1(2.0 * log_a)) * (ig * xc)
    if long_seq:
        pos = lax.broadcasted_iota(jnp.int32, (tm, c), 0)
        k = 1
        while k < tm:
            keep = pos >= k
            h = h + a * jnp.where(keep, pltpu.roll(h, k, 0), 0.0)
            a = a * jnp.where(keep, pltpu.roll(a, k, 0), 1.0)
            k *= 2
        h = h + a * hcarry_ref[SUBLANES - 1:SUBLANES, :]
        hcarry_ref[...] = h[tm - SUBLANES:]
        ccarry_ref[...] = x[tm - SUBLANES:]
        cst_ref[...] = x[tm - SUBLANES:]
        hst_ref[...] = h[tm - SUBLANES:]
    else:
        shp = (tm // SUBLANES, SUBLANES, c)
        h3, a3 = h.reshape(shp), a.reshape(shp)
        pos = lax.broadcasted_iota(jnp.int32, shp, 1)
        k = 1
        while k < SUBLANES:
            keep = pos >= k
            h3 = h3 + a3 * jnp.where(keep, pltpu.roll(h3, k, 1), 0.0)
            a3 = a3 * jnp.where(keep, pltpu.roll(a3, k, 1), 1.0)
            k *= 2
        h = (h3 + a3 * hinit_ref[...].reshape(shp)).reshape(tm, c)
        cst_ref[...] = x
        hst_ref[...] = h
    y_ref[...] = h * jax.nn.gelu(gate_ref[...])


def _rglru(proj, chist, hinit, lp, seq_len, tm):
    n = proj.shape[0]
    long_seq = seq_len > SUBLANES
    tps = max(seq_len // tm, 1)
    c0 = LRU_COL0 // LRU_W
    if long_seq:
        hs = pl.BlockSpec((SUBLANES, LRU_W), lambda i: (i // tps, 0))
    else:
        hs = pl.BlockSpec((tm, LRU_W), lambda i: (i, 0))
    row = pl.BlockSpec((1, LRU_W), lambda i: (0, 0))
    sq = pl.BlockSpec((LRU_W, LRU_W), lambda i: (0, 0))
    return pl.pallas_call(
        functools.partial(_rglru_body, long_seq=long_seq, tps=tps), grid=(n // tm,),
        in_specs=[pl.BlockSpec((tm, LRU_W), lambda i: (i, c0)),
                  pl.BlockSpec((tm, LRU_W), lambda i: (i, c0 + 1)), hs, hs,
                  pl.BlockSpec((4, LRU_W), lambda i: (0, 0)), row, sq, row, sq, row, row],
        out_specs=[pl.BlockSpec((tm, LRU_W), lambda i: (i, 0)), hs, hs],
        out_shape=[jax.ShapeDtypeStruct((n, LRU_W), F32),
                   jax.ShapeDtypeStruct(chist.shape, F32),
                   jax.ShapeDtypeStruct(chist.shape, F32)],
        scratch_shapes=[pltpu.VMEM((SUBLANES, LRU_W), F32), pltpu.VMEM((SUBLANES, LRU_W), F32)],
        compiler_params=_cp(1), name="rglru")(
            proj, proj, chist, hinit, lp["lru_cw"], lp["lru_cb"], lp["lru_wa"], lp["lru_ba"],
            lp["lru_wx"], lp["lru_bx"], lp["lru_lam"])


def _hist(state):
    nseq, k, c = state.shape
    return jnp.pad(state, ((0, 0), (SUBLANES - k, 0), (0, 0))).reshape(nseq * SUBLANES, c)


def _block_diag(w):
    h, a, b = w.shape
    eye = jnp.eye(h, dtype=w.dtype)
    return (eye[:, None, :, None] * w[:, :, None, :]).reshape(h * a, h * b)


def _layer(x3, st, lp, consts, layer, last):
    nseq, seq_len, d = x3.shape
    n = nseq * seq_len
    long_seq = seq_len > SUBLANES
    s_rw, shift_rw, s_hg, h_lru, buf_lru, buf_ffn = st
    x = x3.reshape(n, d)
    ones = consts["ones"]

    proj = _in_proj(x, lp["norm_mix"], lp["w_in"], tm=min(n, 1024), tn=512)

    tm_pre = min(n, seq_len if long_seq else n, 256)
    outs = _rwkv_pre(proj, _hist(shift_rw[:, None, :]), lp, ones, seq_len, tm_pre)
    r, w, k, v, na, kb, g, bonus, shift_out = outs
    new_shift = shift_out.reshape(nseq, SUBLANES, RWKV_COLS)[:, SUBLANES - 1]
    if long_seq:
        nch = seq_len // RWKV_CHUNK
        tmaj = lambda t: t.reshape(nseq * nch, RWKV_CHUNK, RWKV_W).transpose(1, 0, 2)
        ya, yb, sf = _rwkv_lanes([tmaj(t) for t in (r, w, k, v, na, kb)], None)
        heads = lambda t: t.reshape(RWKV_CHUNK, nseq * nch, RWKV_HEADS, RWKV_N).transpose(1, 2, 0, 3)
        mz = sf.reshape(RWKV_HEADS, nseq * nch, 2 * RWKV_N, RWKV_N).transpose(1, 0, 2, 3)
        y4, new_s_rw = _rwkv_combine(mz, heads(ya), heads(yb), nseq)
        y = y4.transpose(0, 2, 1, 3).reshape(n, RWKV_W)
    else:
        tmaj = lambda t: t.reshape(nseq, seq_len, RWKV_W).transpose(1, 0, 2)
        s0 = s_rw.reshape(nseq, RWKV_HEADS, RWKV_N * RWKV_N).transpose(1, 0, 2)
        yb, sf = _rwkv_lanes([tmaj(t) for t in (r, w, k, v, na, kb)], s0)
        y = yb.transpose(1, 0, 2).reshape(n, RWKV_W)
        new_s_rw = sf.transpose(1, 0, 2).reshape(nseq, RWKV_HEADS, RWKV_N, RWKV_N)
    y_rw = _rwkv_post(y, bonus, g, lp["ln_w"], lp["ln_b"], ones, tm=min(n, 512))

    if long_seq:
        y_hg, new_s_hg = _hgrn_prompt(proj, consts["lb_logits"], lp["hgrn_nw"], consts["cmat"],
                                      nseq, seq_len, layer)
    else:
        proj_t = proj.reshape(nseq, seq_len, IN_COLS).transpose(1, 0, 2)
        s0 = s_hg.reshape(nseq, HGRN_HEADS, HGRN_D * HGRN_D).transpose(1, 0, 2)
        o_t, sf = _hgrn_lanes(proj_t, consts["lb_logits"], lp["hgrn_nw"], s0, layer)
        y_hg = o_t.transpose(1, 0, 2).reshape(n, HGRN_W)
        new_s_hg = sf.transpose(1, 0, 2).reshape(nseq, HGRN_HEADS, HGRN_D, HGRN_D)

    if long_seq:
        hinit = _hist(h_lru[:, None, :])
        tm_lru = min(seq_len, 256)
    else:
        hinit = jnp.broadcast_to(h_lru[:, None, :], (nseq, seq_len, LRU_W)).reshape(n, LRU_W)
        tm_lru = min(n, 512)
    y_lru, cst, hst = _rglru(proj, _hist(buf_lru), hinit, lp, seq_len, tm_lru)
    new_buf_lru = cst.reshape(nseq, SUBLANES, LRU_W)[:, SUBLANES - 3:]
    new_h_lru = hst.reshape(nseq, SUBLANES, LRU_W)[:, SUBLANES - 1]

    x = _out_proj(x, y_rw, y_hg, y_lru, lp["w_out"], tm=min(n, 1024), tn=512)
    tm_ffn = min(n, seq_len if long_seq else n, 512)
    x, fst = _ffn(x, lp["norm_ffn"], lp["w_up"], lp["ffn_cw"], lp["ffn_cb"], lp["w_down"],
                  _hist(buf_ffn), seq_len, tm=tm_ffn, tf=512)
    new_buf_ffn = fst.reshape(nseq, SUBLANES, D_FF)[:, SUBLANES - 2:]
    return x.reshape(nseq, seq_len, d), (new_s_rw, new_shift, new_s_hg, new_h_lru, new_buf_lru,
                                         new_buf_ffn)


def kernel(x_prompt, x_sample, state_rwkv, state_rwkv_shift, state_hgrn, state_rglru, cache_rglru_conv, cache_ffn_conv, norm_mix, w_in, rwkv_mu, rwkv_w0, rwkv_w2, rwkv_a0, rwkv_a2, rwkv_g2, rwkv_k_k, rwkv_k_a, rwkv_r_k, rwkv_ln_w, rwkv_ln_b, hgrn_lb_logits, hgrn_norm_w, rglru_conv_w, rglru_conv_b, rglru_wa, rglru_ba, rglru_wx, rglru_bx, rglru_lambda, w_out, norm_ffn, ffn_w_up, ffn_conv_w, ffn_conv_b, ffn_w_down, norm_final):
    depth = w_in.shape[0]
    nb, nt, _ = x_prompt.shape
    db = x_sample.shape[0]
    consts = {
        "ones": jnp.asarray(np.kron(np.eye(RWKV_HEADS), np.ones((RWKV_N, RWKV_N))), dtype=BF16),
        "cmat": _hgrn_level_matrix(),
        "lb_logits": hgrn_lb_logits,
    }
    rowv = lambda a: a.reshape(1, -1)
    x_p, x_s = x_prompt, x_sample
    new_p, new_s = [], []
    for l in range(depth):
        lp = {
            "norm_mix": rowv(norm_mix[l]), "w_in": w_in[l].astype(BF16), "mu": rowv(rwkv_mu[l]),
            "w0": rowv(rwkv_w0[l]),
            "w2p": jnp.pad(rwkv_w2[l], ((0, 64), (0, 0))).astype(BF16),
            "a0": rowv(rwkv_a0[l]),
            "a2p": jnp.pad(rwkv_a2[l], ((64, 0), (0, 0))).astype(BF16),
            "g2": rwkv_g2[l].astype(BF16), "k_k": rowv(rwkv_k_k[l]), "k_a": rowv(rwkv_k_a[l]),
            "r_k": rowv(rwkv_r_k[l]), "ln_w": rowv(rwkv_ln_w[l]), "ln_b": rowv(rwkv_ln_b[l]),
            "hgrn_nw": rowv(hgrn_norm_w[l]),
            "lru_cw": rglru_conv_w[l], "lru_cb": rowv(rglru_conv_b[l]),
            "lru_wa": _block_diag(rglru_wa[l]).astype(BF16), "lru_ba": rowv(rglru_ba[l]),
            "lru_wx": _block_diag(rglru_wx[l]).astype(BF16), "lru_bx": rowv(rglru_bx[l]),
            "lru_lam": rowv(rglru_lambda[l]),
            "w_out": w_out[l].astype(BF16), "norm_ffn": rowv(norm_ffn[l]),
            "w_up": ffn_w_up[l].astype(BF16), "ffn_cw": ffn_conv_w[l],
            "ffn_cb": rowv(ffn_conv_b[l]), "w_down": ffn_w_down[l].astype(BF16),
        }
        zero = lambda *s: jnp.zeros(s, F32)
        st_p = (None, zero(nb, RWKV_COLS), None, zero(nb, LRU_W), zero(nb, 3, LRU_W),
                zero(nb, 2, D_FF))
        x_p, sp = _layer(x_p, st_p, lp, consts, l, l == depth - 1)
        st_s = (state_rwkv[l], state_rwkv_shift[l], state_hgrn[l], state_rglru[l],
                cache_rglru_conv[l], cache_ffn_conv[l])
        x_s, ss = _layer(x_s, st_s, lp, consts, l, l == depth - 1)
        new_p.append(sp)
        new_s.append(ss)
    nf = rowv(norm_final)
    y_prompt = _final_norm(x_p.reshape(nb * nt, D_MODEL), nf, min(nb * nt, 1024)).reshape(x_prompt.shape)
    y_sample = _final_norm(x_s.reshape(-1, D_MODEL), nf, min(x_s.shape[0] * x_s.shape[1], 1024)).reshape(x_sample.shape)
    stack = lambda sts: [jnp.stack(s, axis=0) for s in zip(*sts)]
    return (y_prompt, y_sample, *stack(new_p), *stack(new_s))
```

```python
import functools

import numpy as np
import jax
import jax.numpy as jnp
from jax import lax
from jax.experimental import pallas as pl
from jax.experimental.pallas import tpu as pltpu

F32 = jnp.float32
BF16 = jnp.bfloat16

D_MODEL = 2048
RWKV_HEADS = 12
RWKV_N = 64
RWKV_W = RWKV_HEADS * RWKV_N
RWKV_COLS = 2560
RWKV_LN_EPS = 64e-5
HGRN_HEADS = 6
HGRN_D = 128
HGRN_W = HGRN_HEADS * HGRN_D
HGRN_COL0 = RWKV_COLS
LRU_W = 512
LRU_COL0 = RWKV_COLS + 4 * HGRN_W
LRU_C = 8.0
IN_COLS = 6656
D_FF = 5632
NORM_EPS = 1e-6

SUBLANES = 8
LANES = 128
RWKV_CHUNK = 64
HGRN_CHUNK = 128
HGRN_LEVELS = 7
VMEM_LIMIT = 48 * 1024 * 1024


def _cp(n):
    return pltpu.CompilerParams(dimension_semantics=("arbitrary",) * n,
                                vmem_limit_bytes=VMEM_LIMIT)


def _dot(a, b):
    return jnp.dot(a, b, preferred_element_type=F32)


def _dot_nt(a, b):
    return lax.dot_general(a, b, (((1,), (1,)), ((), ())), preferred_element_type=F32)


def _split(x):
    hi = x.astype(BF16)
    lo = (x - hi.astype(F32)).astype(BF16)
    return hi, lo


def _split_dot(x, m):
    hi, lo = _split(x)
    return _dot(hi, m) + _dot(lo, m)


def _dot3(a, b, nt=False):
    f = _dot_nt if nt else _dot
    ah, al = _split(a)
    bh, bl = _split(b)
    return f(ah, bh) + (f(ah, bl) + f(al, bh))


def _softplus(z):
    return jnp.maximum(z, 0.0) + jnp.log1p(jnp.exp(-jnp.abs(z)))


def _expm1(z):
    return jnp.tanh(0.5 * z) * (jnp.exp(z) + 1.0)


def _prev_rows(x, hist, ks, long_seq):
    tm, c = x.shape
    if long_seq:
        ext = jnp.concatenate([hist, x], axis=0)
        return [pltpu.roll(ext, k, 0)[SUBLANES:] for k in ks]
    x3 = x.reshape(tm // SUBLANES, SUBLANES, c)
    h3 = hist.reshape(tm // SUBLANES, SUBLANES, c)
    pos = lax.broadcasted_iota(jnp.int32, x3.shape, 1)
    return [jnp.where(pos >= k, pltpu.roll(x3, k, 1), pltpu.roll(h3, k, 1)).reshape(tm, c)
            for k in ks]


def _in_proj_body(x_ref, nw_ref, w_ref, o_ref, xn_ref):
    @pl.when(pl.program_id(1) == 0)
    def _():
        x = x_ref[...]
        ms = jnp.mean(x * x, axis=-1, keepdims=True)
        xn_ref[...] = (x * lax.rsqrt(ms + NORM_EPS) * nw_ref[...]).astype(BF16)
    o_ref[...] = _dot(xn_ref[...], w_ref[...])


def _in_proj(x, nw, w, tm, tn):
    n, d = x.shape
    c = w.shape[1]
    return pl.pallas_call(
        _in_proj_body, grid=(n // tm, c // tn),
        in_specs=[pl.BlockSpec((tm, d), lambda i, j: (i, 0)),
                  pl.BlockSpec((1, d), lambda i, j: (0, 0)),
                  pl.BlockSpec((d, tn), lambda i, j: (0, j))],
        out_specs=pl.BlockSpec((tm, tn), lambda i, j: (i, j)),
        out_shape=jax.ShapeDtypeStruct((n, c), F32),
        scratch_shapes=[pltpu.VMEM((tm, d), BF16)],
        compiler_params=_cp(2), name="in_proj")(x, nw, w)


def _out_proj_body(x_ref, ya_ref, yb_ref, yc_ref, w_ref, o_ref, y_ref):
    @pl.when(pl.program_id(1) == 0)
    def _():
        y_ref[:, 0:RWKV_W] = ya_ref[...].astype(BF16)
        y_ref[:, RWKV_W:RWKV_W + HGRN_W] = yb_ref[...].astype(BF16)
        y_ref[:, RWKV_W + HGRN_W:] = yc_ref[...].astype(BF16)
    o_ref[...] = x_ref[...] + _dot(y_ref[...], w_ref[...])


def _out_proj(x, ya, yb, yc, w, tm, tn):
    n, d = x.shape
    return pl.pallas_call(
        _out_proj_body, grid=(n // tm, d // tn),
        in_specs=[pl.BlockSpec((tm, tn), lambda i, j: (i, j)),
                  pl.BlockSpec((tm, RWKV_W), lambda i, j: (i, 0)),
                  pl.BlockSpec((tm, HGRN_W), lambda i, j: (i, 0)),
                  pl.BlockSpec((tm, LRU_W), lambda i, j: (i, 0)),
                  pl.BlockSpec((d, tn), lambda i, j: (0, j))],
        out_specs=pl.BlockSpec((tm, tn), lambda i, j: (i, j)),
        out_shape=jax.ShapeDtypeStruct((n, d), F32),
        scratch_shapes=[pltpu.VMEM((tm, d), BF16)],
        compiler_params=_cp(2), name="out_proj")(x, ya, yb, yc, w)


def _ffn_body(x_ref, nw_ref, wg_ref, wv_ref, cw_ref, cb_ref, wd_ref, hist_ref,
              o_ref, st_ref, hn_ref, acc_ref, carry_ref, *, long_seq, tps):
    i = pl.program_id(0)
    j = pl.program_id(1)

    @pl.when(j == 0)
    def _():
        x = x_ref[...]
        ms = jnp.mean(x * x, axis=-1, keepdims=True)
        hn_ref[...] = (x * lax.rsqrt(ms + NORM_EPS) * nw_ref[...]).astype(BF16)
        acc_ref[...] = x

    hn = hn_ref[...]
    g = _dot(hn, wg_ref[...])
    v = _dot(hn, wv_ref[...])
    if long_seq:
        @pl.when(i % tps == 0)
        def _():
            carry_ref[j] = hist_ref[...]
        hist = carry_ref[j]
    else:
        hist = hist_ref[...]
    p1, p2 = _prev_rows(g, hist, (1, 2), long_seq)
    cw = cw_ref[...]
    gc = cb_ref[...] + cw[0:1] * p2 + cw[1:2] * p1 + cw[2:3] * g
    h = (gc * jax.nn.sigmoid(gc)) * v
    acc_ref[...] += _dot(h.astype(BF16), wd_ref[...])
    if long_seq:
        carry_ref[j] = g[g.shape[0] - SUBLANES:]
        st_ref[...] = g[g.shape[0] - SUBLANES:]
    else:
        st_ref[...] = g

    @pl.when(j == pl.num_programs(1) - 1)
    def _():
        o_ref[...] = acc_ref[...]


def _ffn(x, nw, w_up, cw, cb, w_down, hist, seq_len, tm, tf):
    n, d = x.shape
    nf = D_FF // tf
    long_seq = seq_len > SUBLANES
    tps = max(seq_len // tm, 1)
    if long_seq:
        hist_spec = pl.BlockSpec((SUBLANES, tf), lambda i, j: (i // tps, j))
        st_spec = pl.BlockSpec((SUBLANES, tf), lambda i, j: (i, j))
        st_rows = n // tm * SUBLANES
    else:
        hist_spec = pl.BlockSpec((tm, tf), lambda i, j: (i, j))
        st_spec = pl.BlockSpec((tm, tf), lambda i, j: (i, j))
        st_rows = n
    body = functools.partial(_ffn_body, long_seq=long_seq, tps=tps)
    return pl.pallas_call(
        body, grid=(n // tm, nf),
        in_specs=[pl.BlockSpec((tm, d), lambda i, j: (i, 0)),
                  pl.BlockSpec((1, d), lambda i, j: (0, 0)),
                  pl.BlockSpec((d, tf), lambda i, j: (0, j)),
                  pl.BlockSpec((d, tf), lambda i, j: (0, j + nf)),
                  pl.BlockSpec((3, tf), lambda i, j: (0, j)),
                  pl.BlockSpec((1, tf), lambda i, j: (0, j)),
                  pl.BlockSpec((tf, d), lambda i, j: (j, 0)),
                  hist_spec],
        out_specs=[pl.BlockSpec((tm, d), lambda i, j: (i, 0)), st_spec],
        out_shape=[jax.ShapeDtypeStruct((n, d), F32),
                   jax.ShapeDtypeStruct((st_rows, D_FF), F32)],
        scratch_shapes=[pltpu.VMEM((tm, d), BF16), pltpu.VMEM((tm, d), F32),
                        pltpu.VMEM((nf, SUBLANES, tf), F32)],
        compiler_params=_cp(2), name="ffn")(x, nw, w_up, w_up, cw, cb, w_down, hist)


def _final_norm_body(x_ref, nw_ref, o_ref):
    x = x_ref[...]
    ms = jnp.mean(x * x, axis=-1, keepdims=True)
    o_ref[...] = x * lax.rsqrt(ms + NORM_EPS) * nw_ref[...]


def _final_norm(x, nw, tm):
    n, d = x.shape
    return pl.pallas_call(
        _final_norm_body, grid=(n // tm,),
        in_specs=[pl.BlockSpec((tm, d), lambda i: (i, 0)), pl.BlockSpec((1, d), lambda i: (0, 0))],
        out_specs=pl.BlockSpec((tm, d), lambda i: (i, 0)),
        out_shape=jax.ShapeDtypeStruct((n, d), F32),
        compiler_params=_cp(1), name="final_norm")(x, nw)


def _rwkv_pre_body(p_ref, hist_ref, mu_ref, w0_ref, w2_ref, a0_ref, a2_ref, g2_ref, kk_ref,
                   ka_ref, rk_ref, ones_ref,
                   r_ref, w_ref, k_ref, v_ref, na_ref, kb_ref, g_ref, bonus_ref, st_ref,
                   carry_ref, *, long_seq, tps):
    p = p_ref[...]
    if long_seq:
        @pl.when(pl.program_id(0) % tps == 0)
        def _():
            carry_ref[...] = hist_ref[...]
        hist = carry_ref[...]
    else:
        hist = hist_ref[...]
    (prev,) = _prev_rows(p, hist, (1,), long_seq)
    if long_seq:
        carry_ref[...] = p[p.shape[0] - SUBLANES:]
        st_ref[...] = p[p.shape[0] - SUBLANES:]
    else:
        st_ref[...] = p
    xs = p + (prev - p) * mu_ref[...]
    r = xs[:, 0:RWKV_W]
    k = xs[:, RWKV_W:2 * RWKV_W]
    v = xs[:, 2 * RWKV_W:3 * RWKV_W]
    xwa = xs[:, 3 * RWKV_W:3 * RWKV_W + 128]
    xg = xs[:, 3 * RWKV_W + 128:RWKV_COLS]
    zw = w0_ref[...] + _dot(jnp.tanh(xwa).astype(BF16), w2_ref[...])
    w_log = -_softplus(-zw) - 0.5
    decay = jnp.exp(-jnp.exp(w_log))
    a = jax.nn.sigmoid(a0_ref[...] + _dot(xwa.astype(BF16), a2_ref[...]))
    g = _dot(jax.nn.sigmoid(xg).astype(BF16), g2_ref[...])
    ones = ones_ref[...]
    kk = k * kk_ref[...]
    kk = kk / jnp.maximum(jnp.sqrt(_split_dot(kk * kk, ones)), 1e-12)
    kf = k * (1.0 + (a - 1.0) * ka_ref[...])
    r_ref[...] = r
    w_ref[...] = decay
    k_ref[...] = kf
    v_ref[...] = v
    na_ref[...] = -kk
    kb_ref[...] = kk * a
    g_ref[...] = g
    bonus_ref[...] = _split_dot(r * kf * rk_ref[...], ones) * v


def _rwkv_pre(proj, hist, lp, ones, seq_len, tm):
    n = proj.shape[0]
    long_seq = seq_len > SUBLANES
    tps = max(seq_len // tm, 1)
    row = lambda c: pl.BlockSpec((1, c), lambda i: (0, 0))
    full = lambda a, b: pl.BlockSpec((a, b), lambda i: (0, 0))
    if long_seq:
        hist_spec = pl.BlockSpec((SUBLANES, RWKV_COLS), lambda i: (i // tps, 0))
    else:
        hist_spec = pl.BlockSpec((tm, RWKV_COLS), lambda i: (i, 0))
    tile = pl.BlockSpec((tm, RWKV_W), lambda i: (i, 0))
    body = functools.partial(_rwkv_pre_body, long_seq=long_seq, tps=tps)
    outs = pl.pallas_call(
        body, grid=(n // tm,),
        in_specs=[pl.BlockSpec((tm, RWKV_COLS), lambda i: (i, 0)), hist_spec,
                  row(RWKV_COLS), row(RWKV_W), full(128, RWKV_W), row(RWKV_W),
                  full(128, RWKV_W), full(128, RWKV_W), row(RWKV_W), row(RWKV_W), row(RWKV_W),
                  full(RWKV_W, RWKV_W)],
        out_specs=[tile] * 8 + [hist_spec],
        out_shape=[jax.ShapeDtypeStruct((n, RWKV_W), F32)] * 8
                  + [jax.ShapeDtypeStruct(hist.shape, F32)],
        scratch_shapes=[pltpu.VMEM((SUBLANES, RWKV_COLS), F32)],
        compiler_params=_cp(1), name="rwkv_pre")(
            proj, hist, lp["mu"], lp["w0"], lp["w2p"], lp["a0"], lp["a2p"], lp["g2"],
            lp["k_k"], lp["k_a"], lp["r_k"], ones)
    return outs


def _rwkv_post_body(y_ref, bonus_ref, g_ref, lw_ref, lb_ref, ones_ref, o_ref):
    y = y_ref[...]
    ones = ones_ref[...]
    mean = _split_dot(y, ones) * (1.0 / RWKV_N)
    d = y - mean
    var = _split_dot(d * d, ones) * (1.0 / RWKV_N)
    yn = d * lax.rsqrt(var + RWKV_LN_EPS)
    o_ref[...] = (yn * lw_ref[...] + lb_ref[...] + bonus_ref[...]) * g_ref[...]


def _rwkv_post(y, bonus, g, lw, lb, ones, tm):
    n = y.shape[0]
    tile = pl.BlockSpec((tm, RWKV_W), lambda i: (i, 0))
    row = pl.BlockSpec((1, RWKV_W), lambda i: (0, 0))
    return pl.pallas_call(
        _rwkv_post_body, grid=(n // tm,),
        in_specs=[tile, tile, tile, row, row, pl.BlockSpec((RWKV_W, RWKV_W), lambda i: (0, 0))],
        out_specs=tile, out_shape=jax.ShapeDtypeStruct((n, RWKV_W), F32),
        compiler_params=_cp(1), name="rwkv_post")(y, bonus, g, lw, lb, ones)


def _rwkv_lanes_body(r_ref, w_ref, k_ref, v_ref, a_ref, b_ref, s0_ref, y_ref, sf_ref,
                     s_ref, xt_ref, y_buf):
    tb = pl.program_id(1)
    ninst = r_ref.shape[1]

    @pl.when(tb == 0)
    def _():
        for hh in range(2):
            s_ref[hh] = s0_ref[hh].T.reshape(RWKV_N, RWKV_N, ninst)

    def step(tl, carry):
        for idx, ref in enumerate((r_ref, w_ref, k_ref, v_ref, a_ref, b_ref)):
            xt_ref[idx] = ref[tl].T
        for hh in range(2):
            base = hh * RWKV_N
            heads = lambda i: xt_ref[i, base:base + RWKV_N, :]

            def row(rho, c):
                s = s_ref[hh, rho]
                sa = jnp.sum(s * heads(4), axis=0, keepdims=True)
                vrow = xt_ref[3, pl.ds(base + rho, 1), :]
                s = s * heads(1) + sa * heads(5) + vrow * heads(2)
                s_ref[hh, rho] = s
                y_buf[pl.ds(base + rho, 1), :] = jnp.sum(s * heads(0), axis=0, keepdims=True)
                return c

            lax.fori_loop(0, RWKV_N, row, 0, unroll=4)
        y_ref[tl] = y_buf[...].T
        return carry

    lax.fori_loop(0, SUBLANES, step, 0)

    @pl.when(tb == pl.num_programs(1) - 1)
    def _():
        for hh in range(2):
            sf_ref[hh] = s_ref[hh].reshape(RWKV_N * RWKV_N, ninst).T


def _rwkv_lanes(xs, s0):
    steps, ninst, _ = xs[0].shape
    xspec = pl.BlockSpec((SUBLANES, ninst, 2 * RWKV_N), lambda h, t: (t, 0, h))
    sspec = pl.BlockSpec((2, ninst, RWKV_N * RWKV_N), lambda h, t: (h, 0, 0))
    return pl.pallas_call(
        _rwkv_lanes_body, grid=(RWKV_HEADS // 2, steps // SUBLANES),
        in_specs=[xspec] * 6 + [sspec], out_specs=[xspec, sspec],
        out_shape=[jax.ShapeDtypeStruct((steps, ninst, RWKV_W), F32),
                   jax.ShapeDtypeStruct(s0.shape, F32)],
        scratch_shapes=[pltpu.VMEM((2, RWKV_N, RWKV_N, ninst), F32),
                        pltpu.VMEM((6, 2 * RWKV_N, ninst), F32),
                        pltpu.VMEM((2 * RWKV_N, ninst), F32)],
        compiler_params=_cp(2), name="rwkv_lanes")(*xs, s0)


def _rwkv_decay_scales(w, lmat):
    lw = jnp.log(w)
    cw = _split_dot_left(lmat, lw)
    return lw, cw


def _rwkv_n_body(w_ref, a_ref, b_ref, l_ref, n_ref):
    c = RWKV_CHUNK
    lmat = l_ref[...]
    strict = lax.broadcasted_iota(jnp.int32, (c, c), 1) < lax.broadcasted_iota(jnp.int32, (c, c), 0)
    lane = lax.broadcasted_iota(jnp.int32, (c, 2 * RWKV_N), 1)
    for p in range(RWKV_HEADS // 2):
        sl = slice(p * 2 * RWKV_N, (p + 1) * 2 * RWKV_N)
        lw, cw = _rwkv_decay_scales(w_ref[:, sl], lmat)
        ath, atl = _split(a_ref[:, sl] * jnp.exp(cw - lw))
        bth, btl = _split(b_ref[:, sl] * jnp.exp(-cw))
        for hh in range(2):
            m = (lane < RWKV_N) if hh == 0 else (lane >= RWKV_N)
            ah = jnp.where(m, ath, jnp.zeros_like(ath))
            al = jnp.where(m, atl, jnp.zeros_like(atl))
            n = _dot_nt(ah, bth) + (_dot_nt(ah, btl) + _dot_nt(al, bth))
            n_ref[2 * p + hh] = jnp.where(strict, n, 0.0)


def _rwkv_n(w, a, b, lmat):
    n = w.shape[0]
    c = RWKV_CHUNK
    tile = pl.BlockSpec((c, RWKV_W), lambda i: (i, 0))
    return pl.pallas_call(
        _rwkv_n_body, grid=(n // c,),
        in_specs=[tile, tile, tile, pl.BlockSpec((c, c), lambda i: (0, 0))],
        out_specs=pl.BlockSpec((RWKV_HEADS, None, c, c), lambda i: (0, i, 0, 0)),
        out_shape=jax.ShapeDtypeStruct((RWKV_HEADS, n // c, c, c), F32),
        compiler_params=_cp(1), name="rwkv_n")(w, a, b, lmat)


def _tri_solve_body(n_ref, t_ref, nt_ref, tt_ref):
    c = RWKV_CHUNK
    ninst = n_ref.shape[0]
    nt_ref[...] = n_ref[...].T.reshape(c, c, ninst)
    rows = lax.broadcasted_iota(jnp.int32, (c, ninst), 0)

    nb = 4

    def row_block(tb, carry):
        t0 = tb * nb

        def acc(s, rs):
            ts = tt_ref[s]
            return tuple(r + nt_ref[t0 + i, pl.ds(s, 1), :] * ts for i, r in enumerate(rs))

        init = tuple(jnp.where(rows == t0 + i, 1.0, 0.0).astype(F32) for i in range(nb))
        rs = list(lax.fori_loop(0, t0, acc, init))
        for i in range(nb):
            for j in range(i):
                rs[i] = rs[i] + nt_ref[t0 + i, pl.ds(t0 + j, 1), :] * rs[j]
            tt_ref[t0 + i] = rs[i]
        return carry

    lax.fori_loop(0, c // nb, row_block, 0)
    t_ref[...] = tt_ref[...].reshape(c * c, ninst).T


def _tri_solve(n3):
    nh, ninst, cc = n3.shape
    c = RWKV_CHUNK
    spec = pl.BlockSpec((None, ninst, cc), lambda h: (h, 0, 0))
    return pl.pallas_call(
        _tri_solve_body, grid=(nh,), in_specs=[spec], out_specs=spec,
        out_shape=jax.ShapeDtypeStruct(n3.shape, F32),
        scratch_shapes=[pltpu.VMEM((c, c, ninst), F32), pltpu.VMEM((c, c, ninst), F32)],
        compiler_params=_cp(1), name="tri_solve")(n3)


def _rwkv_chunk_body(r_ref, w_ref, k_ref, v_ref, a_ref, b_ref, t_ref, l_ref, y_ref, sf_ref, s_ref):
    c = RWKV_CHUNK
    pw = 2 * RWKV_N
    ci = pl.program_id(1)

    @pl.when(ci == 0)
    def _():
        s_ref[...] = jnp.zeros_like(s_ref)

    lmat = l_ref[...]
    ti = lax.broadcasted_iota(jnp.int32, (c, c), 0)
    si = lax.broadcasted_iota(jnp.int32, (c, c), 1)
    strict, incl = si < ti, si <= ti
    lane = lax.broadcasted_iota(jnp.int32, (c, pw), 1)
    lane2 = lax.broadcasted_iota(jnp.int32, (2 * c, pw), 1)
    r2 = lax.broadcasted_iota(jnp.int32, (pw, pw), 0)
    c2 = lax.broadcasted_iota(jnp.int32, (pw, pw), 1)
    same_head = (r2 < RWKV_N) == (c2 < RWKV_N)
    eye = r2 == c2
    bz = lambda x: jnp.zeros_like(x)
    pairs = range(RWKV_HEADS // 2)
    sls = [slice(p * pw, (p + 1) * pw) for p in pairs]
    halves = [(p, hh) for p in pairs for hh in range(2)]
    lws = [jnp.log(w_ref[:, sl]) for sl in sls]
    cws = [_split_dot_left(lmat, lw) for lw in lws]
    vs = [v_ref[:, sl] for sl in sls]
    wts = [jnp.exp(cw) for cw in cws]
    rts = [r_ref[:, sl] * wt for sl, wt in zip(sls, wts)]
    ats = [a_ref[:, sl] * jnp.exp(cw - lw) for sl, cw, lw in zip(sls, cws, lws)]
    lhs = [_split(jnp.concatenate([at, rt], axis=0)) for at, rt in zip(ats, rts)]
    bts = [_split(b_ref[:, sl] * jnp.exp(-cw)) for sl, cw in zip(sls, cws)]
    kts = [_split(k_ref[:, sl] * jnp.exp(-cw)) for sl, cw in zip(sls, cws)]
    a_ak, a_rk, a_rb = {}, {}, {}
    for p, hh in halves:
        m2 = (lane2 < RWKV_N) if hh == 0 else (lane2 >= RWKV_N)
        lh = jnp.where(m2, lhs[p][0], bz(lhs[p][0]))
        ll = jnp.where(m2, lhs[p][1], bz(lhs[p][1]))
        pk = _dot_nt(lh, kts[p][0]) + (_dot_nt(lh, kts[p][1]) + _dot_nt(ll, kts[p][0]))
        prb = (_dot_nt(lh[c:], bts[p][0])
               + (_dot_nt(lh[c:], bts[p][1]) + _dot_nt(ll[c:], bts[p][0])))
        a_ak[p, hh] = jnp.where(strict, pk[:c], 0.0)
        a_rk[p, hh] = jnp.where(incl, pk[c:], 0.0).astype(BF16)
        a_rb[p, hh] = jnp.where(incl, prb, 0.0).astype(BF16)
    vps = {ph: _dot3(a_ak[ph], vs[ph[0]]) for ph in halves}
    gs = {(p, hh): _dot3(t_ref[2 * p + hh], jnp.concatenate([ats[p], vps[p, hh]], axis=1))
          for p, hh in halves}
    pick = lambda x0, x1: jnp.where(lane < RWKV_N, x0, x1)
    ahats = [pick(gs[p, 0][:, :pw], gs[p, 1][:, :pw]) for p in pairs]
    vhats = [pick(gs[p, 0][:, pw:], gs[p, 1][:, pw:]) for p in pairs]
    ahb = [x.astype(BF16) for x in ahats]
    vhb = [x.astype(BF16) for x in vhats]
    vbs = [x.astype(BF16) for x in vs]
    rhats = [rts[p] + pick(_dot(a_rb[p, 0], ahb[p]), _dot(a_rb[p, 1], ahb[p])) for p in pairs]
    yhats = [pick(_dot(a_rb[p, 0], vhb[p]) + _dot(a_rk[p, 0], vbs[p]),
                  _dot(a_rb[p, 1], vhb[p]) + _dot(a_rk[p, 1], vbs[p])) for p in pairs]
    spts = [s_ref[p] for p in pairs]
    for p in pairs:
        y_ref[:, sls[p]] = _dot3(rhats[p], spts[p]) + yhats[p]
    wends = [jnp.exp(cw[c - 1:c] - cw) for cw in cws]
    bkts = [jnp.concatenate([b_ref[:, sl] * we, k_ref[:, sl] * we], axis=0).T
            for sl, we in zip(sls, wends)]
    mpts = [jnp.where(same_head, _dot3(bkts[p], jnp.concatenate([ahats[p], bz(ahats[p])], axis=0)), 0.0)
            + jnp.where(eye, wts[p][c - 1:c], 0.0) for p in pairs]
    zpts = [jnp.where(same_head, _dot3(bkts[p], jnp.concatenate([vhats[p], vs[p]], axis=0)), 0.0)
            for p in pairs]
    for p in pairs:
        s_ref[p] = _dot3(mpts[p], spts[p]) + zpts[p]

    @pl.when(ci == pl.num_programs(1) - 1)
    def _():
        sf_ref[...] = s_ref[...]


def _rwkv_chunk(xs, t4, lmat, nseq):
    n = xs[0].shape[0]
    c = RWKV_CHUNK
    nch = n // c // nseq
    pw = 2 * RWKV_N
    tile = pl.BlockSpec((c, RWKV_W), lambda b, i: (b * nch + i, 0))
    return pl.pallas_call(
        _rwkv_chunk_body, grid=(nseq, nch),
        in_specs=[tile] * 6 + [pl.BlockSpec((RWKV_HEADS, None, c, c), lambda b, i: (0, b * nch + i, 0, 0)),
                               pl.BlockSpec((c, c), lambda b, i: (0, 0))],
        out_specs=[tile, pl.BlockSpec((None, RWKV_HEADS // 2, pw, pw), lambda b, i: (b, 0, 0, 0))],
        out_shape=[jax.ShapeDtypeStruct((n, RWKV_W), F32),
                   jax.ShapeDtypeStruct((nseq, RWKV_HEADS // 2, pw, pw), F32)],
        scratch_shapes=[pltpu.VMEM((RWKV_HEADS // 2, pw, pw), F32)],
        compiler_params=_cp(2), name="rwkv_chunk")(*xs, t4, lmat)


def _hgrn_lower_bound(logits, layer):
    m = jnp.max(logits, axis=0, keepdims=True)
    e = jnp.exp(logits - m)
    gam = e / jnp.sum(e, axis=0, keepdims=True)
    cs = gam[0:1]
    for i in range(1, layer + 1):
        cs = cs + gam[i:i + 1]
    return cs - gam[0:1]


def _hgrn_level_matrix():
    t = np.arange(HGRN_CHUNK)[:, None]
    s = np.arange(HGRN_CHUNK)[None, :]
    mats = []
    for upper in (False, True):
        for e in range(1, HGRN_LEVELS + 1):
            same = (t >> e) == (s >> e)
            mats.append(same & ((s > t) if upper else (s <= t)))
    return jnp.asarray(np.concatenate(mats, axis=0).astype(np.float32), dtype=BF16)


def _hgrn_prompt_body(q_ref, f_ref, i_ref, g_ref, lg_ref, nw_ref, cm_ref, o_ref, sf_ref, s_ref,
                      *, layer):
    c = pl.program_id(2)

    @pl.when(c == 0)
    def _():
        s_ref[...] = jnp.zeros_like(s_ref)

    n = HGRN_CHUNK
    t = lax.broadcasted_iota(jnp.int32, (n, n), 0)
    s_ = lax.broadcasted_iota(jnp.int32, (n, n), 1)
    lb_all = _hgrn_lower_bound(lg_ref[...], layer)
    cm = cm_ref[...]
    ones = jnp.ones((n, n), BF16)
    for hh in range(2):
        sl = slice(hh * HGRN_D, (hh + 1) * HGRN_D)
        lb = lb_all[:, sl]
        qr = q_ref[:, sl]
        q = qr * jax.nn.sigmoid(qr)
        fg = lb + (1.0 - lb) * jax.nn.sigmoid(f_ref[:, sl])
        lf = jnp.log(fg)
        k = 1.0 - fg
        v = i_ref[:, sl]
        vb = v.astype(BF16)
        cs = _split_dot_left(cm, lf)
        cl = lambda e: cs[(e - 1) * n:e * n]
        cu = lambda e: cs[(HGRN_LEVELS + e - 1) * n:(HGRN_LEVELS + e) * n]
        a = jnp.where(t == s_, _dot_nt(q.astype(BF16), k.astype(BF16)), 0.0)
        for e in range(HGRN_LEVELS):
            qe = q * (fg if e == 0 else jnp.exp(cl(e)))
            ke = k if e == 0 else k * jnp.exp(cu(e))
            mask = (((t >> e) & 1) == 1) & ((s_ >> e) == ((t >> e) - 1))
            a = a + jnp.where(mask, _dot_nt(qe.astype(BF16), ke.astype(BF16)), 0.0)
        qf = q * jnp.exp(cl(HGRN_LEVELS))
        kf = k * jnp.exp(cu(HGRN_LEVELS))
        st = s_ref[hh]
        o = _dot(a.astype(BF16), vb) + _dot(qf.astype(BF16), st.astype(BF16))
        dk = jnp.exp(_split_dot(lf.T, ones))
        s_ref[hh] = dk * st + _dot(kf.T.astype(BF16), vb)
        ms = jnp.mean(o * o, axis=-1, keepdims=True)
        gr = g_ref[:, sl]
        o_ref[:, sl] = (o * lax.rsqrt(ms + NORM_EPS) * nw_ref[:, sl]) * (gr * jax.nn.sigmoid(gr))

    @pl.when(c == pl.num_programs(2) - 1)
    def _():
        sf_ref[...] = s_ref[...]


def _split_dot_left(m, x):
    hi, lo = _split(x)
    return _dot(m, hi) + _dot(m, lo)


def _hgrn_prompt(proj, logits, nw, cmat, nseq, seq_len, layer):
    n = proj.shape[0]
    nch = seq_len // HGRN_CHUNK
    wb = 2 * HGRN_D
    col = lambda off: pl.BlockSpec((HGRN_CHUNK, wb),
                                   lambda p, b, c, off=off: (b * nch + c, off // wb + p))
    return pl.pallas_call(
        functools.partial(_hgrn_prompt_body, layer=layer),
        grid=(HGRN_HEADS // 2, nseq, nch),
        in_specs=[col(HGRN_COL0), col(HGRN_COL0 + HGRN_W), col(HGRN_COL0 + 2 * HGRN_W),
                  col(HGRN_COL0 + 3 * HGRN_W),
                  pl.BlockSpec((logits.shape[0], wb), lambda p, b, c: (0, p)),
                  pl.BlockSpec((1, wb), lambda p, b, c: (0, p)),
                  pl.BlockSpec(cmat.shape, lambda p, b, c: (0, 0))],
        out_specs=[pl.BlockSpec((HGRN_CHUNK, wb), lambda p, b, c: (b * nch + c, p)),
                   pl.BlockSpec((None, 2, HGRN_D, HGRN_D), lambda p, b, c: (b, p, 0, 0))],
        out_shape=[jax.ShapeDtypeStruct((n, HGRN_W), F32),
                   jax.ShapeDtypeStruct((nseq, HGRN_HEADS, HGRN_D, HGRN_D), F32)],
        scratch_shapes=[pltpu.VMEM((2, HGRN_D, HGRN_D), F32)],
        compiler_params=_cp(3), name="hgrn_prompt")(proj, proj, proj, proj, logits, nw, cmat)


HGRN_KT = 32


def _hgrn_lanes_body(q_ref, f_ref, i_ref, g_ref, lg_ref, nw_ref, s0_ref, o_ref, sf_ref,
                     s_ref, qt_ref, ft_ref, kt_ref, vt_ref, oacc_ref, *, layer):
    kt = pl.program_id(1)
    ninst = q_ref.shape[1]
    lb = _hgrn_lower_bound(lg_ref[...], layer)
    s_ref[...] = s0_ref[0].T.reshape(HGRN_KT, HGRN_D, ninst)

    @pl.when(kt == 0)
    def _():
        oacc_ref[...] = jnp.zeros_like(oacc_ref)

    for tl in range(SUBLANES):
        qr = q_ref[tl]
        fg = lb + (1.0 - lb) * jax.nn.sigmoid(f_ref[tl])
        qt_ref[tl] = (qr * jax.nn.sigmoid(qr)).T
        ft_ref[tl] = fg.T
        kt_ref[tl] = (1.0 - fg).T
        vt_ref[tl] = i_ref[tl].T

    def step(tl, carry):
        vt = vt_ref[tl]

        def krow(kk, o):
            row = kt * HGRN_KT + kk
            s = ft_ref[tl, pl.ds(row, 1), :] * s_ref[kk] + kt_ref[tl, pl.ds(row, 1), :] * vt
            s_ref[kk] = s
            return o + qt_ref[tl, pl.ds(row, 1), :] * s

        oacc_ref[tl] = lax.fori_loop(0, HGRN_KT, krow, oacc_ref[tl], unroll=2)
        return carry

    lax.fori_loop(0, SUBLANES, step, 0)
    sf_ref[0] = s_ref[...].reshape(HGRN_KT * HGRN_D, ninst).T

    @pl.when(kt == pl.num_programs(1) - 1)
    def _():
        for tl in range(SUBLANES):
            o = oacc_ref[tl].T
            ms = jnp.mean(o * o, axis=-1, keepdims=True)
            gr = g_ref[tl]
            o_ref[tl] = (o * lax.rsqrt(ms + NORM_EPS) * nw_ref[...]) * (gr * jax.nn.sigmoid(gr))


def _hgrn_lanes(proj_t, logits, nw, s0, layer):
    steps, ninst, _ = proj_t.shape
    col = lambda off: pl.BlockSpec((steps, ninst, HGRN_D),
                                   lambda h, k, off=off: (0, 0, off // HGRN_D + h))
    sspec = pl.BlockSpec((1, ninst, HGRN_KT * HGRN_D), lambda h, k: (h, 0, k))
    tbuf = pltpu.VMEM((steps, HGRN_D, ninst), F32)
    return pl.pallas_call(
        functools.partial(_hgrn_lanes_body, layer=layer),
        grid=(HGRN_HEADS, HGRN_D // HGRN_KT),
        in_specs=[col(HGRN_COL0), col(HGRN_COL0 + HGRN_W), col(HGRN_COL0 + 2 * HGRN_W),
                  col(HGRN_COL0 + 3 * HGRN_W),
                  pl.BlockSpec((logits.shape[0], HGRN_D), lambda h, k: (0, h)),
                  pl.BlockSpec((1, HGRN_D), lambda h, k: (0, h)), sspec],
        out_specs=[pl.BlockSpec((steps, ninst, HGRN_D), lambda h, k: (0, 0, h)), sspec],
        out_shape=[jax.ShapeDtypeStruct((steps, ninst, HGRN_W), F32),
                   jax.ShapeDtypeStruct(s0.shape, F32)],
        scratch_shapes=[pltpu.VMEM((HGRN_KT, HGRN_D, ninst), F32), tbuf, tbuf, tbuf, tbuf, tbuf],
        compiler_params=_cp(2), name="hgrn_lanes")(
            proj_t, proj_t, proj_t, proj_t, logits, nw, s0)


def _rglru_body(xb_ref, gate_ref, chist_ref, hinit_ref, cw_ref, cb_ref, wa_ref, ba_ref, wx_ref,
                bx_ref, lam_ref, y_ref, cst_ref, hst_ref, ccarry_ref, hcarry_ref,
                *, long_seq, tps):
    x = xb_ref[...]
    tm, c = x.shape
    if long_seq:
        @pl.when(pl.program_id(0) % tps == 0)
        def _():
            ccarry_ref[...] = chist_ref[...]
            hcarry_ref[...] = hinit_ref[...]
        hist = ccarry_ref[...]
    else:
        hist = chist_ref[...]
    p1, p2, p3 = _prev_rows(x, hist, (1, 2, 3), long_seq)
    cw = cw_ref[...]
    xc = cb_ref[...] + cw[0:1] * p3 + cw[1:2] * p2 + cw[2:3] * p1 + cw[3:4] * x
    xcb = xc.astype(BF16)
    r = jax.nn.sigmoid(_dot(xcb, wa_ref[...]) + ba_ref[...])
    ig = jax.nn.sigmoid(_dot(xcb, wx_ref[...]) + bx_ref[...])
    log_a = (-LRU_C) * r * _softplus(-lam_ref[...])
    a = jnp.exp(log_a)
    h = jnp.sqrt(-_expm1(2.0 * log_a)) * (ig * xc)
    if long_seq:
        pos = lax.broadcasted_iota(jnp.int32, (tm, c), 0)
        k = 1
        while k < tm:
            keep = pos >= k
            h = h + a * jnp.where(keep, pltpu.roll(h, k, 0), 0.0)
            a = a * jnp.where(keep, pltpu.roll(a, k, 0), 1.0)
            k *= 2
        h = h + a * hcarry_ref[SUBLANES - 1:SUBLANES, :]
        hcarry_ref[...] = h[tm - SUBLANES:]
        ccarry_ref[...] = x[tm - SUBLANES:]
        cst_ref[...] = x[tm - SUBLANES:]
        hst_ref[...] = h[tm - SUBLANES:]
    else:
        shp = (tm // SUBLANES, SUBLANES, c)
        h3, a3 = h.reshape(shp), a.reshape(shp)
        pos = lax.broadcasted_iota(jnp.int32, shp, 1)
        k = 1
        while k < SUBLANES:
            keep = pos >= k
            h3 = h3 + a3 * jnp.where(keep, pltpu.roll(h3, k, 1), 0.0)
            a3 = a3 * jnp.where(keep, pltpu.roll(a3, k, 1), 1.0)
            k *= 2
        h = (h3 + a3 * hinit_ref[...].reshape(shp)).reshape(tm, c)
        cst_ref[...] = x
        hst_ref[...] = h
    y_ref[...] = h * jax.nn.gelu(gate_ref[...])


def _rglru(proj, chist, hinit, lp, seq_len, tm):
    n = proj.shape[0]
    long_seq = seq_len > SUBLANES
    tps = max(seq_len // tm, 1)
    c0 = LRU_COL0 // LRU_W
    if long_seq:
        hs = pl.BlockSpec((SUBLANES, LRU_W), lambda i: (i // tps, 0))
    else:
        hs = pl.BlockSpec((tm, LRU_W), lambda i: (i, 0))
    row = pl.BlockSpec((1, LRU_W), lambda i: (0, 0))
    sq = pl.BlockSpec((LRU_W, LRU_W), lambda i: (0, 0))
    return pl.pallas_call(
        functools.partial(_rglru_body, long_seq=long_seq, tps=tps), grid=(n // tm,),
        in_specs=[pl.BlockSpec((tm, LRU_W), lambda i: (i, c0)),
                  pl.BlockSpec((tm, LRU_W), lambda i: (i, c0 + 1)), hs, hs,
                  pl.BlockSpec((4, LRU_W), lambda i: (0, 0)), row, sq, row, sq, row, row],
        out_specs=[pl.BlockSpec((tm, LRU_W), lambda i: (i, 0)), hs, hs],
        out_shape=[jax.ShapeDtypeStruct((n, LRU_W), F32),
                   jax.ShapeDtypeStruct(chist.shape, F32),
                   jax.ShapeDtypeStruct(chist.shape, F32)],
        scratch_shapes=[pltpu.VMEM((SUBLANES, LRU_W), F32), pltpu.VMEM((SUBLANES, LRU_W), F32)],
        compiler_params=_cp(1), name="rglru")(
            proj, proj, chist, hinit, lp["lru_cw"], lp["lru_cb"], lp["lru_wa"], lp["lru_ba"],
            lp["lru_wx"], lp["lru_bx"], lp["lru_lam"])


def _hist(state):
    nseq, k, c = state.shape
    return jnp.pad(state, ((0, 0), (SUBLANES - k, 0), (0, 0))).reshape(nseq * SUBLANES, c)


def _block_diag(w):
    h, a, b = w.shape
    eye = jnp.eye(h, dtype=w.dtype)
    return (eye[:, None, :, None] * w[:, :, None, :]).reshape(h * a, h * b)


def _layer(x3, st, lp, consts, layer, last):
    nseq, seq_len, d = x3.shape
    n = nseq * seq_len
    long_seq = seq_len > SUBLANES
    s_rw, shift_rw, s_hg, h_lru, buf_lru, buf_ffn = st
    x = x3.reshape(n, d)
    ones = consts["ones"]

    proj = _in_proj(x, lp["norm_mix"], lp["w_in"], tm=min(n, 1024), tn=512)

    tm_pre = min(n, seq_len if long_seq else n, 256)
    outs = _rwkv_pre(proj, _hist(shift_rw[:, None, :]), lp, ones, seq_len, tm_pre)
    r, w, k, v, na, kb, g, bonus, shift_out = outs
    new_shift = shift_out.reshape(nseq, SUBLANES, RWKV_COLS)[:, SUBLANES - 1]
    if long_seq:
        c = RWKV_CHUNK
        n4 = _rwkv_n(w, na, kb, consts["lmat"])
        t3 = _tri_solve(n4.reshape(RWKV_HEADS, n // c, c * c))
        y, spt = _rwkv_chunk((r, w, k, v, na, kb), t3.reshape(n4.shape), consts["lmat"], nseq)
        sp = spt.reshape(nseq, RWKV_HEADS // 2, 2, RWKV_N, 2, RWKV_N)
        new_s_rw = jnp.stack([sp[:, :, 0, :, 0, :], sp[:, :, 1, :, 1, :]], axis=2)
        new_s_rw = new_s_rw.reshape(nseq, RWKV_HEADS, RWKV_N, RWKV_N).transpose(0, 1, 3, 2)
    else:
        tmaj = lambda t: t.reshape(nseq, seq_len, RWKV_W).transpose(1, 0, 2)
        s0 = s_rw.reshape(nseq, RWKV_HEADS, RWKV_N * RWKV_N).transpose(1, 0, 2)
        yb, sf = _rwkv_lanes([tmaj(t) for t in (r, w, k, v, na, kb)], s0)
        y = yb.transpose(1, 0, 2).reshape(n, RWKV_W)
        new_s_rw = sf.transpose(1, 0, 2).reshape(nseq, RWKV_HEADS, RWKV_N, RWKV_N)
    y_rw = _rwkv_post(y, bonus, g, lp["ln_w"], lp["ln_b"], ones, tm=min(n, 512))

    if long_seq:
        y_hg, new_s_hg = _hgrn_prompt(proj, consts["lb_logits"], lp["hgrn_nw"], consts["cmat"],
                                      nseq, seq_len, layer)
    else:
        proj_t = proj.reshape(nseq, seq_len, IN_COLS).transpose(1, 0, 2)
        s0 = s_hg.reshape(nseq, HGRN_HEADS, HGRN_D * HGRN_D).transpose(1, 0, 2)
        o_t, sf = _hgrn_lanes(proj_t, consts["lb_logits"], lp["hgrn_nw"], s0, layer)
        y_hg = o_t.transpose(1, 0, 2).reshape(n, HGRN_W)
        new_s_hg = sf.transpose(1, 0, 2).reshape(nseq, HGRN_HEADS, HGRN_D, HGRN_D)

    if long_seq:
        hinit = _hist(h_lru[:, None, :])
        tm_lru = min(seq_len, 256)
    else:
        hinit = jnp.broadcast_to(h_lru[:, None, :], (nseq, seq_len, LRU_W)).reshape(n, LRU_W)
        tm_lru = min(n, 512)
    y_lru, cst, hst = _rglru(proj, _hist(buf_lru), hinit, lp, seq_len, tm_lru)
    new_buf_lru = cst.reshape(nseq, SUBLANES, LRU_W)[:, SUBLANES - 3:]
    new_h_lru = hst.reshape(nseq, SUBLANES, LRU_W)[:, SUBLANES - 1]

    x = _out_proj(x, y_rw, y_hg, y_lru, lp["w_out"], tm=min(n, 1024), tn=512)
    tm_ffn = min(n, seq_len if long_seq else n, 512)
    x, fst = _ffn(x, lp["norm_ffn"], lp["w_up"], lp["ffn_cw"], lp["ffn_cb"], lp["w_down"],
                  _hist(buf_ffn), seq_len, tm=tm_ffn, tf=512)
    new_buf_ffn = fst.reshape(nseq, -1, SUBLANES, D_FF)[:, -1, SUBLANES - 2:]
    return x.reshape(nseq, seq_len, d), (new_s_rw, new_shift, new_s_hg, new_h_lru, new_buf_lru,
                                         new_buf_ffn)


def kernel(x_prompt, x_sample, state_rwkv, state_rwkv_shift, state_hgrn, state_rglru, cache_rglru_conv, cache_ffn_conv, norm_mix, w_in, rwkv_mu, rwkv_w0, rwkv_w2, rwkv_a0, rwkv_a2, rwkv_g2, rwkv_k_k, rwkv_k_a, rwkv_r_k, rwkv_ln_w, rwkv_ln_b, hgrn_lb_logits, hgrn_norm_w, rglru_conv_w, rglru_conv_b, rglru_wa, rglru_ba, rglru_wx, rglru_bx, rglru_lambda, w_out, norm_ffn, ffn_w_up, ffn_conv_w, ffn_conv_b, ffn_w_down, norm_final):
    depth = w_in.shape[0]
    nb, nt, _ = x_prompt.shape
    db = x_sample.shape[0]
    consts = {
        "ones": jnp.asarray(np.kron(np.eye(RWKV_HEADS), np.ones((RWKV_N, RWKV_N))), dtype=BF16),
        "cmat": _hgrn_level_matrix(),
        "lmat": jnp.asarray(np.tril(np.ones((RWKV_CHUNK, RWKV_CHUNK), np.float32)), dtype=BF16),
        "lb_logits": hgrn_lb_logits,
    }
    rowv = lambda a: a.reshape(1, -1)
    x_p, x_s = x_prompt, x_sample
    new_p, new_s = [], []
    for l in range(depth):
        lp = {
            "norm_mix": rowv(norm_mix[l]), "w_in": w_in[l].astype(BF16), "mu": rowv(rwkv_mu[l]),
            "w0": rowv(rwkv_w0[l]),
            "w2p": jnp.pad(rwkv_w2[l], ((0, 64), (0, 0))).astype(BF16),
            "a0": rowv(rwkv_a0[l]),
            "a2p": jnp.pad(rwkv_a2[l], ((64, 0), (0, 0))).astype(BF16),
            "g2": rwkv_g2[l].astype(BF16), "k_k": rowv(rwkv_k_k[l]), "k_a": rowv(rwkv_k_a[l]),
            "r_k": rowv(rwkv_r_k[l]), "ln_w": rowv(rwkv_ln_w[l]), "ln_b": rowv(rwkv_ln_b[l]),
            "hgrn_nw": rowv(hgrn_norm_w[l]),
            "lru_cw": rglru_conv_w[l], "lru_cb": rowv(rglru_conv_b[l]),
            "lru_wa": _block_diag(rglru_wa[l]).astype(BF16), "lru_ba": rowv(rglru_ba[l]),
            "lru_wx": _block_diag(rglru_wx[l]).astype(BF16), "lru_bx": rowv(rglru_bx[l]),
            "lru_lam": rowv(rglru_lambda[l]),
            "w_out": w_out[l].astype(BF16), "norm_ffn": rowv(norm_ffn[l]),
            "w_up": ffn_w_up[l].astype(BF16), "ffn_cw": ffn_conv_w[l],
            "ffn_cb": rowv(ffn_conv_b[l]), "w_down": ffn_w_down[l].astype(BF16),
        }
        zero = lambda *s: jnp.zeros(s, F32)
        st_p = (None, zero(nb, RWKV_COLS), None, zero(nb, LRU_W), zero(nb, 3, LRU_W),
                zero(nb, 2, D_FF))
        x_p, sp = _layer(x_p, st_p, lp, consts, l, l == depth - 1)
        st_s = (state_rwkv[l], state_rwkv_shift[l], state_hgrn[l], state_rglru[l],
                cache_rglru_conv[l], cache_ffn_conv[l])
        x_s, ss = _layer(x_s, st_s, lp, consts, l, l == depth - 1)
        new_p.append(sp)
        new_s.append(ss)
    nf = rowv(norm_final)
    y_prompt = _final_norm(x_p.reshape(nb * nt, D_MODEL), nf, min(nb * nt, 1024)).reshape(x_prompt.shape)
    y_sample = _final_norm(x_s.reshape(-1, D_MODEL), nf, min(x_s.shape[0] * x_s.shape[1], 1024)).reshape(x_sample.shape)
    stack = lambda sts: [jnp.stack(s, axis=0) for s in zip(*sts)]
    return (y_prompt, y_sample, *stack(new_p), *stack(new_s))
```

```python
import functools

import numpy as np
import jax
import jax.numpy as jnp
from jax import lax
from jax.experimental import pallas as pl
from jax.experimental.pallas import tpu as pltpu

F32 = jnp.float32
BF16 = jnp.bfloat16

D_MODEL = 2048
RWKV_HEADS = 12
RWKV_N = 64
RWKV_W = RWKV_HEADS * RWKV_N
RWKV_COLS = 2560
RWKV_LN_EPS = 64e-5
HGRN_HEADS = 6
HGRN_D = 128
HGRN_W = HGRN_HEADS * HGRN_D
HGRN_COL0 = RWKV_COLS
LRU_W = 512
LRU_COL0 = RWKV_COLS + 4 * HGRN_W
LRU_C = 8.0
IN_COLS = 6656
D_FF = 5632
NORM_EPS = 1e-6

SUBLANES = 8
LANES = 128
RWKV_CHUNK = 64
HGRN_CHUNK = 128
HGRN_LEVELS = 7
VMEM_LIMIT = 48 * 1024 * 1024


def _cp(n):
    return pltpu.CompilerParams(dimension_semantics=("arbitrary",) * n,
                                vmem_limit_bytes=VMEM_LIMIT)


def _dot(a, b):
    return jnp.dot(a, b, preferred_element_type=F32)


def _dot_nt(a, b):
    return lax.dot_general(a, b, (((1,), (1,)), ((), ())), preferred_element_type=F32)


def _split(x):
    hi = x.astype(BF16)
    lo = (x - hi.astype(F32)).astype(BF16)
    return hi, lo


def _split_dot(x, m):
    hi, lo = _split(x)
    return _dot(hi, m) + _dot(lo, m)


def _dot3(a, b, nt=False):
    f = _dot_nt if nt else _dot
    ah, al = _split(a)
    bh, bl = _split(b)
    return f(ah, bh) + (f(ah, bl) + f(al, bh))


def _softplus(z):
    return jnp.maximum(z, 0.0) + jnp.log1p(jnp.exp(-jnp.abs(z)))


def _expm1(z):
    return jnp.tanh(0.5 * z) * (jnp.exp(z) + 1.0)


def _prev_rows(x, hist, ks, long_seq):
    tm, c = x.shape
    if long_seq:
        ext = jnp.concatenate([hist, x], axis=0)
        return [pltpu.roll(ext, k, 0)[SUBLANES:] for k in ks]
    x3 = x.reshape(tm // SUBLANES, SUBLANES, c)
    h3 = hist.reshape(tm // SUBLANES, SUBLANES, c)
    pos = lax.broadcasted_iota(jnp.int32, x3.shape, 1)
    return [jnp.where(pos >= k, pltpu.roll(x3, k, 1), pltpu.roll(h3, k, 1)).reshape(tm, c)
            for k in ks]


def _in_proj_body(x_ref, nw_ref, w_ref, o_ref, xn_ref):
    @pl.when(pl.program_id(1) == 0)
    def _():
        x = x_ref[...]
        ms = jnp.mean(x * x, axis=-1, keepdims=True)
        xn_ref[...] = (x * lax.rsqrt(ms + NORM_EPS) * nw_ref[...]).astype(BF16)
    o_ref[...] = _dot(xn_ref[...], w_ref[...])


def _in_proj(x, nw, w, layer, tm, tn):
    n, d = x.shape
    c = w.shape[2]
    return pl.pallas_call(
        _in_proj_body, grid=(n // tm, c // tn),
        in_specs=[pl.BlockSpec((tm, d), lambda i, j: (i, 0)),
                  pl.BlockSpec((1, d), lambda i, j: (0, 0)),
                  pl.BlockSpec((None, d, tn), lambda i, j: (layer, 0, j))],
        out_specs=pl.BlockSpec((tm, tn), lambda i, j: (i, j)),
        out_shape=jax.ShapeDtypeStruct((n, c), F32),
        scratch_shapes=[pltpu.VMEM((tm, d), BF16)],
        compiler_params=_cp(2), name="in_proj")(x, nw, w)


def _out_proj_body(x_ref, ya_ref, yb_ref, yc_ref, w_ref, o_ref, y_ref):
    @pl.when(pl.program_id(1) == 0)
    def _():
        y_ref[:, 0:RWKV_W] = ya_ref[...].astype(BF16)
        y_ref[:, RWKV_W:RWKV_W + HGRN_W] = yb_ref[...].astype(BF16)
        y_ref[:, RWKV_W + HGRN_W:] = yc_ref[...].astype(BF16)
    o_ref[...] = x_ref[...] + _dot(y_ref[...], w_ref[...])


def _out_proj(x, ya, yb, yc, w, layer, tm, tn):
    n, d = x.shape
    return pl.pallas_call(
        _out_proj_body, grid=(n // tm, d // tn),
        in_specs=[pl.BlockSpec((tm, tn), lambda i, j: (i, j)),
                  pl.BlockSpec((tm, RWKV_W), lambda i, j: (i, 0)),
                  pl.BlockSpec((tm, HGRN_W), lambda i, j: (i, 0)),
                  pl.BlockSpec((tm, LRU_W), lambda i, j: (i, 0)),
                  pl.BlockSpec((None, d, tn), lambda i, j: (layer, 0, j))],
        out_specs=pl.BlockSpec((tm, tn), lambda i, j: (i, j)),
        out_shape=jax.ShapeDtypeStruct((n, d), F32),
        scratch_shapes=[pltpu.VMEM((tm, d), BF16)],
        compiler_params=_cp(2), name="out_proj")(x, ya, yb, yc, w)


def _ffn_body(x_ref, nw_ref, wg_ref, wv_ref, cw_ref, cb_ref, wd_ref, hist_ref,
              o_ref, st_ref, hn_ref, acc_ref, carry_ref, *, long_seq, tps):
    i = pl.program_id(0)
    j = pl.program_id(1)

    @pl.when(j == 0)
    def _():
        x = x_ref[...]
        ms = jnp.mean(x * x, axis=-1, keepdims=True)
        hn_ref[...] = (x * lax.rsqrt(ms + NORM_EPS) * nw_ref[...]).astype(BF16)
        acc_ref[...] = x

    hn = hn_ref[...]
    g = _dot(hn, wg_ref[...])
    v = _dot(hn, wv_ref[...])
    if long_seq:
        @pl.when(i % tps == 0)
        def _():
            carry_ref[j] = hist_ref[...]
        hist = carry_ref[j]
    else:
        hist = hist_ref[...]
    p1, p2 = _prev_rows(g, hist, (1, 2), long_seq)
    cw = cw_ref[...]
    gc = cb_ref[...] + cw[0:1] * p2 + cw[1:2] * p1 + cw[2:3] * g
    h = (gc * jax.nn.sigmoid(gc)) * v
    acc_ref[...] += _dot(h.astype(BF16), wd_ref[...])
    if long_seq:
        carry_ref[j] = g[g.shape[0] - SUBLANES:]
        st_ref[...] = g[g.shape[0] - SUBLANES:]
    else:
        st_ref[...] = g

    @pl.when(j == pl.num_programs(1) - 1)
    def _():
        o_ref[...] = acc_ref[...]


def _ffn(x, nw, w_up, cw, cb, w_down, hist, layer, seq_len, tm, tf):
    n, d = x.shape
    nf = D_FF // tf
    long_seq = seq_len > SUBLANES
    tps = max(seq_len // tm, 1)
    if long_seq:
        hist_spec = pl.BlockSpec((SUBLANES, tf), lambda i, j: (i // tps, j))
        st_spec = pl.BlockSpec((SUBLANES, tf), lambda i, j: (i, j))
        st_rows = n // tm * SUBLANES
    else:
        hist_spec = pl.BlockSpec((tm, tf), lambda i, j: (i, j))
        st_spec = pl.BlockSpec((tm, tf), lambda i, j: (i, j))
        st_rows = n
    body = functools.partial(_ffn_body, long_seq=long_seq, tps=tps)
    return pl.pallas_call(
        body, grid=(n // tm, nf),
        in_specs=[pl.BlockSpec((tm, d), lambda i, j: (i, 0)),
                  pl.BlockSpec((1, d), lambda i, j: (0, 0)),
                  pl.BlockSpec((None, d, tf), lambda i, j: (layer, 0, j)),
                  pl.BlockSpec((None, d, tf), lambda i, j: (layer, 0, j + nf)),
                  pl.BlockSpec((3, tf), lambda i, j: (0, j)),
                  pl.BlockSpec((1, tf), lambda i, j: (0, j)),
                  pl.BlockSpec((None, tf, d), lambda i, j: (layer, j, 0)),
                  hist_spec],
        out_specs=[pl.BlockSpec((tm, d), lambda i, j: (i, 0)), st_spec],
        out_shape=[jax.ShapeDtypeStruct((n, d), F32),
                   jax.ShapeDtypeStruct((st_rows, D_FF), F32)],
        scratch_shapes=[pltpu.VMEM((tm, d), BF16), pltpu.VMEM((tm, d), F32),
                        pltpu.VMEM((nf, SUBLANES, tf), F32)],
        compiler_params=_cp(2), name="ffn")(x, nw, w_up, w_up, cw, cb, w_down, hist)


def _final_norm_body(x_ref, nw_ref, o_ref):
    x = x_ref[...]
    ms = jnp.mean(x * x, axis=-1, keepdims=True)
    o_ref[...] = x * lax.rsqrt(ms + NORM_EPS) * nw_ref[...]


def _final_norm(x, nw, tm):
    n, d = x.shape
    return pl.pallas_call(
        _final_norm_body, grid=(n // tm,),
        in_specs=[pl.BlockSpec((tm, d), lambda i: (i, 0)), pl.BlockSpec((1, d), lambda i: (0, 0))],
        out_specs=pl.BlockSpec((tm, d), lambda i: (i, 0)),
        out_shape=jax.ShapeDtypeStruct((n, d), F32),
        compiler_params=_cp(1), name="final_norm")(x, nw)


def _rwkv_pre_body(p_ref, hist_ref, mu_ref, w0_ref, w2_ref, a0_ref, a2_ref, g2_ref, kk_ref,
                   ka_ref, rk_ref, ones_ref,
                   r_ref, w_ref, k_ref, v_ref, na_ref, kb_ref, g_ref, bonus_ref, st_ref,
                   carry_ref, *, long_seq, tps):
    p = p_ref[...]
    if long_seq:
        @pl.when(pl.program_id(0) % tps == 0)
        def _():
            carry_ref[...] = hist_ref[...]
        hist = carry_ref[...]
    else:
        hist = hist_ref[...]
    (prev,) = _prev_rows(p, hist, (1,), long_seq)
    if long_seq:
        carry_ref[...] = p[p.shape[0] - SUBLANES:]
        st_ref[...] = p[p.shape[0] - SUBLANES:]
    else:
        st_ref[...] = p
    xs = p + (prev - p) * mu_ref[...]
    r = xs[:, 0:RWKV_W]
    k = xs[:, RWKV_W:2 * RWKV_W]
    v = xs[:, 2 * RWKV_W:3 * RWKV_W]
    xwa = xs[:, 3 * RWKV_W:3 * RWKV_W + 128]
    xg = xs[:, 3 * RWKV_W + 128:RWKV_COLS]
    zw = w0_ref[...] + _dot(jnp.tanh(xwa).astype(BF16), w2_ref[...])
    w_log = -_softplus(-zw) - 0.5
    decay = jnp.exp(-jnp.exp(w_log))
    a = jax.nn.sigmoid(a0_ref[...] + _dot(xwa.astype(BF16), a2_ref[...]))
    g = _dot(jax.nn.sigmoid(xg).astype(BF16), g2_ref[...])
    ones = ones_ref[...]
    kk = k * kk_ref[...]
    kk = kk / jnp.maximum(jnp.sqrt(_split_dot(kk * kk, ones)), 1e-12)
    kf = k * (1.0 + (a - 1.0) * ka_ref[...])
    r_ref[...] = r
    w_ref[...] = decay
    k_ref[...] = kf
    v_ref[...] = v
    na_ref[...] = -kk
    kb_ref[...] = kk * a
    g_ref[...] = g
    bonus_ref[...] = _split_dot(r * kf * rk_ref[...], ones) * v


def _rwkv_pre(proj, hist, lp, ones, seq_len, tm):
    n = proj.shape[0]
    long_seq = seq_len > SUBLANES
    tps = max(seq_len // tm, 1)
    row = lambda c: pl.BlockSpec((1, c), lambda i: (0, 0))
    full = lambda a, b: pl.BlockSpec((a, b), lambda i: (0, 0))
    if long_seq:
        hist_spec = pl.BlockSpec((SUBLANES, RWKV_COLS), lambda i: (i // tps, 0))
    else:
        hist_spec = pl.BlockSpec((tm, RWKV_COLS), lambda i: (i, 0))
    tile = pl.BlockSpec((tm, RWKV_W), lambda i: (i, 0))
    body = functools.partial(_rwkv_pre_body, long_seq=long_seq, tps=tps)
    outs = pl.pallas_call(
        body, grid=(n // tm,),
        in_specs=[pl.BlockSpec((tm, RWKV_COLS), lambda i: (i, 0)), hist_spec,
                  row(RWKV_COLS), row(RWKV_W), full(128, RWKV_W), row(RWKV_W),
                  full(128, RWKV_W), full(128, RWKV_W), row(RWKV_W), row(RWKV_W), row(RWKV_W),
                  full(RWKV_W, RWKV_W)],
        out_specs=[tile] * 8 + [hist_spec],
        out_shape=[jax.ShapeDtypeStruct((n, RWKV_W), F32)] * 8
                  + [jax.ShapeDtypeStruct(hist.shape, F32)],
        scratch_shapes=[pltpu.VMEM((SUBLANES, RWKV_COLS), F32)],
        compiler_params=_cp(1), name="rwkv_pre")(
            proj, hist, lp["mu"], lp["w0"], lp["w2p"], lp["a0"], lp["a2p"], lp["g2"],
            lp["k_k"], lp["k_a"], lp["r_k"], ones)
    return outs


def _rwkv_post_body(y_ref, bonus_ref, g_ref, lw_ref, lb_ref, ones_ref, o_ref):
    y = y_ref[...]
    ones = ones_ref[...]
    mean = _split_dot(y, ones) * (1.0 / RWKV_N)
    d = y - mean
    var = _split_dot(d * d, ones) * (1.0 / RWKV_N)
    yn = d * lax.rsqrt(var + RWKV_LN_EPS)
    o_ref[...] = (yn * lw_ref[...] + lb_ref[...] + bonus_ref[...]) * g_ref[...]


def _rwkv_post(y, bonus, g, lw, lb, ones, tm):
    n = y.shape[0]
    tile = pl.BlockSpec((tm, RWKV_W), lambda i: (i, 0))
    row = pl.BlockSpec((1, RWKV_W), lambda i: (0, 0))
    return pl.pallas_call(
        _rwkv_post_body, grid=(n // tm,),
        in_specs=[tile, tile, tile, row, row, pl.BlockSpec((RWKV_W, RWKV_W), lambda i: (0, 0))],
        out_specs=tile, out_shape=jax.ShapeDtypeStruct((n, RWKV_W), F32),
        compiler_params=_cp(1), name="rwkv_post")(y, bonus, g, lw, lb, ones)


def _rwkv_lanes_body(r_ref, w_ref, k_ref, v_ref, a_ref, b_ref, s0_ref, y_ref, sf_ref,
                     s_ref, xt_ref, y_buf):
    tb = pl.program_id(1)
    ninst = r_ref.shape[1]

    @pl.when(tb == 0)
    def _():
        for hh in range(2):
            s_ref[hh] = s0_ref[hh].T.reshape(RWKV_N, RWKV_N, ninst)

    def step(tl, carry):
        for idx, ref in enumerate((r_ref, w_ref, k_ref, v_ref, a_ref, b_ref)):
            xt_ref[idx] = ref[tl].T
        for hh in range(2):
            base = hh * RWKV_N
            heads = lambda i: xt_ref[i, base:base + RWKV_N, :]

            def row(rho, c):
                s = s_ref[hh, rho]
                sa = jnp.sum(s * heads(4), axis=0, keepdims=True)
                vrow = xt_ref[3, pl.ds(base + rho, 1), :]
                s = s * heads(1) + sa * heads(5) + vrow * heads(2)
                s_ref[hh, rho] = s
                y_buf[pl.ds(base + rho, 1), :] = jnp.sum(s * heads(0), axis=0, keepdims=True)
                return c

            lax.fori_loop(0, RWKV_N, row, 0, unroll=4)
        y_ref[tl] = y_buf[...].T
        return carry

    lax.fori_loop(0, SUBLANES, step, 0)

    @pl.when(tb == pl.num_programs(1) - 1)
    def _():
        for hh in range(2):
            sf_ref[hh] = s_ref[hh].reshape(RWKV_N * RWKV_N, ninst).T


def _rwkv_lanes(xs, s0):
    steps, ninst, _ = xs[0].shape
    xspec = pl.BlockSpec((SUBLANES, ninst, 2 * RWKV_N), lambda h, t: (t, 0, h))
    sspec = pl.BlockSpec((2, ninst, RWKV_N * RWKV_N), lambda h, t: (h, 0, 0))
    return pl.pallas_call(
        _rwkv_lanes_body, grid=(RWKV_HEADS // 2, steps // SUBLANES),
        in_specs=[xspec] * 6 + [sspec], out_specs=[xspec, sspec],
        out_shape=[jax.ShapeDtypeStruct((steps, ninst, RWKV_W), F32),
                   jax.ShapeDtypeStruct(s0.shape, F32)],
        scratch_shapes=[pltpu.VMEM((2, RWKV_N, RWKV_N, ninst), F32),
                        pltpu.VMEM((6, 2 * RWKV_N, ninst), F32),
                        pltpu.VMEM((2 * RWKV_N, ninst), F32)],
        compiler_params=_cp(2), name="rwkv_lanes")(*xs, s0)


def _rwkv_n_body(w_ref, a_ref, b_ref, l_ref, n_ref):
    c = RWKV_CHUNK
    lmat = l_ref[...]
    strict = lax.broadcasted_iota(jnp.int32, (c, c), 1) < lax.broadcasted_iota(jnp.int32, (c, c), 0)
    lane = lax.broadcasted_iota(jnp.int32, (c, 2 * RWKV_N), 1)
    pairs = range(RWKV_HEADS // 2)
    sls = [slice(p * 2 * RWKV_N, (p + 1) * 2 * RWKV_N) for p in pairs]
    lws = [jnp.log(w_ref[:, sl]) for sl in sls]
    cws = [_split_dot_left(lmat, lw) for lw in lws]
    ats = [_split(a_ref[:, sl] * jnp.exp(cw - lw)) for sl, cw, lw in zip(sls, cws, lws)]
    bts = [_split(b_ref[:, sl] * jnp.exp(-cw)) for sl, cw in zip(sls, cws)]
    for p in pairs:
        for hh in range(2):
            m = (lane < RWKV_N) if hh == 0 else (lane >= RWKV_N)
            ah = jnp.where(m, ats[p][0], jnp.zeros_like(ats[p][0]))
            al = jnp.where(m, ats[p][1], jnp.zeros_like(ats[p][1]))
            n = _dot_nt(ah, bts[p][0]) + (_dot_nt(ah, bts[p][1]) + _dot_nt(al, bts[p][0]))
            n_ref[2 * p + hh] = jnp.where(strict, n, 0.0)


def _rwkv_n(w, a, b, lmat):
    n = w.shape[0]
    c = RWKV_CHUNK
    tile = pl.BlockSpec((c, RWKV_W), lambda i: (i, 0))
    return pl.pallas_call(
        _rwkv_n_body, grid=(n // c,),
        in_specs=[tile, tile, tile, pl.BlockSpec((c, c), lambda i: (0, 0))],
        out_specs=pl.BlockSpec((RWKV_HEADS, None, c, c), lambda i: (0, i, 0, 0)),
        out_shape=jax.ShapeDtypeStruct((RWKV_HEADS, n // c, c, c), F32),
        compiler_params=_cp(1), name="rwkv_n")(w, a, b, lmat)


def _tri_solve_body(n_ref, t_ref, nt_ref, tt_ref):
    c = RWKV_CHUNK
    ninst = n_ref.shape[0]
    nt_ref[...] = n_ref[...].T.reshape(c, c, ninst)
    tt_ref[...] = jnp.zeros_like(tt_ref)
    g = SUBLANES
    nb = 4
    sub = lax.broadcasted_iota(jnp.int32, (g, ninst), 0)
    for t0 in range(0, c, nb):
        ngrp = (t0 + nb - 1) // g + 1
        rs = [[jnp.where(sub + g * j == t0 + i, 1.0, 0.0).astype(F32) for j in range(ngrp)]
              for i in range(nb)]
        for sg in range(-(-t0 // g)):

            def acc(s, carry, sg=sg, t0=t0):
                ts = [tt_ref[s, j * g:(j + 1) * g, :] for j in range(sg + 1)]
                nrow = [nt_ref[t0 + i, pl.ds(s, 1), :] for i in range(nb)]
                return tuple(tuple(carry[i][j] + nrow[i] * ts[j] for j in range(sg + 1))
                             for i in range(nb))

            res = lax.fori_loop(sg * g, min(sg * g + g, t0), acc,
                                tuple(tuple(rs[i][:sg + 1]) for i in range(nb)))
            for i in range(nb):
                rs[i][:sg + 1] = list(res[i])
        for i in range(nb):
            for j in range(i):
                nij = nt_ref[t0 + i, t0 + j:t0 + j + 1, :]
                rs[i] = [x + nij * y for x, y in zip(rs[i], rs[j])]
            tt_ref[t0 + i, 0:ngrp * g, :] = jnp.concatenate(rs[i], axis=0)
    t_ref[...] = tt_ref[...].reshape(c * c, ninst).T


def _tri_solve(n3):
    nh, ninst, cc = n3.shape
    c = RWKV_CHUNK
    spec = pl.BlockSpec((None, ninst, cc), lambda h: (h, 0, 0))
    return pl.pallas_call(
        _tri_solve_body, grid=(nh,), in_specs=[spec], out_specs=spec,
        out_shape=jax.ShapeDtypeStruct(n3.shape, F32),
        scratch_shapes=[pltpu.VMEM((c, c, ninst), F32), pltpu.VMEM((c, c, ninst), F32)],
        compiler_params=_cp(1), name="tri_solve")(n3)


def _rwkv_chunk_body(r_ref, w_ref, k_ref, v_ref, a_ref, b_ref, t_ref, l_ref, y_ref, sf_ref, s_ref):
    c = RWKV_CHUNK
    pw = 2 * RWKV_N
    ci = pl.program_id(1)

    @pl.when(ci == 0)
    def _():
        s_ref[...] = jnp.zeros_like(s_ref)

    lmat = l_ref[...]
    ti = lax.broadcasted_iota(jnp.int32, (c, c), 0)
    si = lax.broadcasted_iota(jnp.int32, (c, c), 1)
    strict, incl = si < ti, si <= ti
    lane = lax.broadcasted_iota(jnp.int32, (c, pw), 1)
    lane2 = lax.broadcasted_iota(jnp.int32, (2 * c, pw), 1)
    r2 = lax.broadcasted_iota(jnp.int32, (pw, pw), 0)
    c2 = lax.broadcasted_iota(jnp.int32, (pw, pw), 1)
    same_head = (r2 < RWKV_N) == (c2 < RWKV_N)
    eye = r2 == c2
    bz = lambda x: jnp.zeros_like(x)
    pairs = range(RWKV_HEADS // 2)
    sls = [slice(p * pw, (p + 1) * pw) for p in pairs]
    halves = [(p, hh) for p in pairs for hh in range(2)]
    lws = [jnp.log(w_ref[:, sl]) for sl in sls]
    cws = [_split_dot_left(lmat, lw) for lw in lws]
    vs = [v_ref[:, sl] for sl in sls]
    wts = [jnp.exp(cw) for cw in cws]
    rts = [r_ref[:, sl] * wt for sl, wt in zip(sls, wts)]
    ats = [a_ref[:, sl] * jnp.exp(cw - lw) for sl, cw, lw in zip(sls, cws, lws)]
    lhs = [_split(jnp.concatenate([at, rt], axis=0)) for at, rt in zip(ats, rts)]
    bts = [_split(b_ref[:, sl] * jnp.exp(-cw)) for sl, cw in zip(sls, cws)]
    kts = [_split(k_ref[:, sl] * jnp.exp(-cw)) for sl, cw in zip(sls, cws)]
    a_ak, a_rk, a_rb = {}, {}, {}
    for p, hh in halves:
        m2 = (lane2 < RWKV_N) if hh == 0 else (lane2 >= RWKV_N)
        lh = jnp.where(m2, lhs[p][0], bz(lhs[p][0]))
        ll = jnp.where(m2, lhs[p][1], bz(lhs[p][1]))
        pk = _dot_nt(lh, kts[p][0]) + (_dot_nt(lh, kts[p][1]) + _dot_nt(ll, kts[p][0]))
        prb = (_dot_nt(lh[c:], bts[p][0])
               + (_dot_nt(lh[c:], bts[p][1]) + _dot_nt(ll[c:], bts[p][0])))
        a_ak[p, hh] = jnp.where(strict, pk[:c], 0.0)
        a_rk[p, hh] = jnp.where(incl, pk[c:], 0.0).astype(BF16)
        a_rb[p, hh] = jnp.where(incl, prb, 0.0).astype(BF16)
    vps = {ph: _dot3(a_ak[ph], vs[ph[0]]) for ph in halves}
    gs = {(p, hh): _dot3(t_ref[2 * p + hh], jnp.concatenate([ats[p], vps[p, hh]], axis=1))
          for p, hh in halves}
    pick = lambda x0, x1: jnp.where(lane < RWKV_N, x0, x1)
    ahats = [pick(gs[p, 0][:, :pw], gs[p, 1][:, :pw]) for p in pairs]
    vhats = [pick(gs[p, 0][:, pw:], gs[p, 1][:, pw:]) for p in pairs]
    ahb = [x.astype(BF16) for x in ahats]
    vhb = [x.astype(BF16) for x in vhats]
    vbs = [x.astype(BF16) for x in vs]
    rhats = [rts[p] + pick(_dot(a_rb[p, 0], ahb[p]), _dot(a_rb[p, 1], ahb[p])) for p in pairs]
    yhats = [pick(_dot(a_rb[p, 0], vhb[p]) + _dot(a_rk[p, 0], vbs[p]),
                  _dot(a_rb[p, 1], vhb[p]) + _dot(a_rk[p, 1], vbs[p])) for p in pairs]
    spts = [s_ref[p] for p in pairs]
    for p in pairs:
        y_ref[:, sls[p]] = _dot3(rhats[p], spts[p]) + yhats[p]
    wends = [jnp.exp(cw[c - 1:c] - cw) for cw in cws]
    bkts = [jnp.concatenate([b_ref[:, sl] * we, k_ref[:, sl] * we], axis=0).T
            for sl, we in zip(sls, wends)]
    mpts = [jnp.where(same_head, _dot3(bkts[p], jnp.concatenate([ahats[p], bz(ahats[p])], axis=0)), 0.0)
            + jnp.where(eye, wts[p][c - 1:c], 0.0) for p in pairs]
    zpts = [jnp.where(same_head, _dot3(bkts[p], jnp.concatenate([vhats[p], vs[p]], axis=0)), 0.0)
            for p in pairs]
    for p in pairs:
        s_ref[p] = _dot3(mpts[p], spts[p]) + zpts[p]

    @pl.when(ci == pl.num_programs(1) - 1)
    def _():
        sf_ref[...] = s_ref[...]


def _rwkv_chunk(xs, t4, lmat, nseq):
    n = xs[0].shape[0]
    c = RWKV_CHUNK
    nch = n // c // nseq
    pw = 2 * RWKV_N
    tile = pl.BlockSpec((c, RWKV_W), lambda b, i: (b * nch + i, 0))
    return pl.pallas_call(
        _rwkv_chunk_body, grid=(nseq, nch),
        in_specs=[tile] * 6 + [pl.BlockSpec((RWKV_HEADS, None, c, c), lambda b, i: (0, b * nch + i, 0, 0)),
                               pl.BlockSpec((c, c), lambda b, i: (0, 0))],
        out_specs=[tile, pl.BlockSpec((None, RWKV_HEADS // 2, pw, pw), lambda b, i: (b, 0, 0, 0))],
        out_shape=[jax.ShapeDtypeStruct((n, RWKV_W), F32),
                   jax.ShapeDtypeStruct((nseq, RWKV_HEADS // 2, pw, pw), F32)],
        scratch_shapes=[pltpu.VMEM((RWKV_HEADS // 2, pw, pw), F32)],
        compiler_params=_cp(2), name="rwkv_chunk")(*xs, t4, lmat)


def _hgrn_lower_bound(logits, layer):
    m = jnp.max(logits, axis=0, keepdims=True)
    e = jnp.exp(logits - m)
    gam = e / jnp.sum(e, axis=0, keepdims=True)
    cs = gam[0:1]
    for i in range(1, layer + 1):
        cs = cs + gam[i:i + 1]
    return cs - gam[0:1]


def _hgrn_level_matrix():
    t = np.arange(HGRN_CHUNK)[:, None]
    s = np.arange(HGRN_CHUNK)[None, :]
    mats = []
    for upper in (False, True):
        for e in range(1, HGRN_LEVELS + 1):
            same = (t >> e) == (s >> e)
            mats.append(same & ((s > t) if upper else (s <= t)))
    return jnp.asarray(np.concatenate(mats, axis=0).astype(np.float32), dtype=BF16)


def _hgrn_prompt_body(*refs, layer):
    npair = HGRN_HEADS // 2
    q_refs, f_refs, i_refs, g_refs = (refs[i * npair:(i + 1) * npair] for i in range(4))
    lg_ref, nw_ref, cm_ref, o_ref, sf_ref, s_ref = refs[4 * npair:]
    c = pl.program_id(1)

    @pl.when(c == 0)
    def _():
        s_ref[...] = jnp.zeros_like(s_ref)

    n = HGRN_CHUNK
    heads = range(HGRN_HEADS)
    col = lambda rs, h: rs[h // 2][:, (h % 2) * HGRN_D:(h % 2 + 1) * HGRN_D]
    hsl = lambda h: slice(h * HGRN_D, (h + 1) * HGRN_D)
    t = lax.broadcasted_iota(jnp.int32, (n, n), 0)
    s_ = lax.broadcasted_iota(jnp.int32, (n, n), 1)
    lb_all = _hgrn_lower_bound(lg_ref[...], layer)
    cm = cm_ref[...]
    ones = jnp.ones((n, n), BF16)
    fgs = [lb_all[:, hsl(h)] + (1.0 - lb_all[:, hsl(h)]) * jax.nn.sigmoid(col(f_refs, h))
           for h in heads]
    lfs = [jnp.log(fg) for fg in fgs]
    css = [_split_dot_left(cm, lf) for lf in lfs]
    cl = lambda h, e: css[h][(e - 1) * n:e * n]
    cu = lambda h, e: css[h][(HGRN_LEVELS + e - 1) * n:(HGRN_LEVELS + e) * n]
    qs = [col(q_refs, h) * jax.nn.sigmoid(col(q_refs, h)) for h in heads]
    ks = [1.0 - fg for fg in fgs]
    vbs = [col(i_refs, h).astype(BF16) for h in heads]
    acc = [jnp.where(t == s_, _dot_nt(qs[h].astype(BF16), ks[h].astype(BF16)), 0.0) for h in heads]
    for e in range(HGRN_LEVELS):
        mask = (((t >> e) & 1) == 1) & ((s_ >> e) == ((t >> e) - 1))
        for h in heads:
            qe = qs[h] * (fgs[h] if e == 0 else jnp.exp(cl(h, e)))
            ke = ks[h] if e == 0 else ks[h] * jnp.exp(cu(h, e))
            acc[h] = acc[h] + jnp.where(mask, _dot_nt(qe.astype(BF16), ke.astype(BF16)), 0.0)
    sts = [s_ref[h] for h in heads]
    os_ = [_dot(acc[h].astype(BF16), vbs[h])
           + _dot((qs[h] * jnp.exp(cl(h, HGRN_LEVELS))).astype(BF16), sts[h].astype(BF16))
           for h in heads]
    dks = [jnp.exp(_split_dot(lfs[h].T, ones)) for h in heads]
    for h in heads:
        kf = ks[h] * jnp.exp(cu(h, HGRN_LEVELS))
        s_ref[h] = dks[h] * sts[h] + _dot(kf.T.astype(BF16), vbs[h])
    for h in heads:
        o = os_[h]
        ms = jnp.mean(o * o, axis=-1, keepdims=True)
        gr = col(g_refs, h)
        o_ref[:, hsl(h)] = (o * lax.rsqrt(ms + NORM_EPS) * nw_ref[:, hsl(h)]) * (gr * jax.nn.sigmoid(gr))

    @pl.when(c == pl.num_programs(1) - 1)
    def _():
        sf_ref[...] = s_ref[...]


def _split_dot_left(m, x):
    hi, lo = _split(x)
    return _dot(m, hi) + _dot(m, lo)


def _hgrn_prompt(proj, logits, nw, cmat, nseq, seq_len, layer):
    n = proj.shape[0]
    nch = seq_len // HGRN_CHUNK
    wb = 2 * HGRN_D
    npair = HGRN_HEADS // 2
    cols = [pl.BlockSpec((HGRN_CHUNK, wb), lambda b, c, j=(HGRN_COL0 + g * HGRN_W) // wb + p:
                         (b * nch + c, j)) for g in range(4) for p in range(npair)]
    return pl.pallas_call(
        functools.partial(_hgrn_prompt_body, layer=layer),
        grid=(nseq, nch),
        in_specs=cols + [pl.BlockSpec(logits.shape, lambda b, c: (0, 0)),
                         pl.BlockSpec((1, HGRN_W), lambda b, c: (0, 0)),
                         pl.BlockSpec(cmat.shape, lambda b, c: (0, 0))],
        out_specs=[pl.BlockSpec((HGRN_CHUNK, HGRN_W), lambda b, c: (b * nch + c, 0)),
                   pl.BlockSpec((None, HGRN_HEADS, HGRN_D, HGRN_D), lambda b, c: (b, 0, 0, 0))],
        out_shape=[jax.ShapeDtypeStruct((n, HGRN_W), F32),
                   jax.ShapeDtypeStruct((nseq, HGRN_HEADS, HGRN_D, HGRN_D), F32)],
        scratch_shapes=[pltpu.VMEM((HGRN_HEADS, HGRN_D, HGRN_D), F32)],
        compiler_params=_cp(2), name="hgrn_prompt")(*([proj] * (4 * npair)), logits, nw, cmat)


HGRN_KT = 32


def _hgrn_lanes_body(q_ref, f_ref, i_ref, g_ref, lg_ref, nw_ref, s0_ref, o_ref, sf_ref,
                     s_ref, qt_ref, ft_ref, kt_ref, vt_ref, oacc_ref, *, layer):
    kt = pl.program_id(1)
    ninst = q_ref.shape[1]
    lb = _hgrn_lower_bound(lg_ref[...], layer)
    s_ref[...] = s0_ref[0].T.reshape(HGRN_KT, HGRN_D, ninst)

    @pl.when(kt == 0)
    def _():
        oacc_ref[...] = jnp.zeros_like(oacc_ref)

    for tl in range(SUBLANES):
        qr = q_ref[tl]
        fg = lb + (1.0 - lb) * jax.nn.sigmoid(f_ref[tl])
        qt_ref[tl] = (qr * jax.nn.sigmoid(qr)).T
        ft_ref[tl] = fg.T
        kt_ref[tl] = (1.0 - fg).T
        vt_ref[tl] = i_ref[tl].T

    def step(tl, carry):
        vt = vt_ref[tl]

        def krow(kk, o):
            row = kt * HGRN_KT + kk
            s = ft_ref[tl, pl.ds(row, 1), :] * s_ref[kk] + kt_ref[tl, pl.ds(row, 1), :] * vt
            s_ref[kk] = s
            return o + qt_ref[tl, pl.ds(row, 1), :] * s

        oacc_ref[tl] = lax.fori_loop(0, HGRN_KT, krow, oacc_ref[tl], unroll=2)
        return carry

    lax.fori_loop(0, SUBLANES, step, 0)
    sf_ref[0] = s_ref[...].reshape(HGRN_KT * HGRN_D, ninst).T

    @pl.when(kt == pl.num_programs(1) - 1)
    def _():
        for tl in range(SUBLANES):
            o = oacc_ref[tl].T
            ms = jnp.mean(o * o, axis=-1, keepdims=True)
            gr = g_ref[tl]
            o_ref[tl] = (o * lax.rsqrt(ms + NORM_EPS) * nw_ref[...]) * (gr * jax.nn.sigmoid(gr))


def _hgrn_lanes(proj_t, logits, nw, s0, layer):
    steps, ninst, _ = proj_t.shape
    col = lambda off: pl.BlockSpec((steps, ninst, HGRN_D),
                                   lambda h, k, off=off: (0, 0, off // HGRN_D + h))
    sspec = pl.BlockSpec((1, ninst, HGRN_KT * HGRN_D), lambda h, k: (h, 0, k))
    tbuf = pltpu.VMEM((steps, HGRN_D, ninst), F32)
    return pl.pallas_call(
        functools.partial(_hgrn_lanes_body, layer=layer),
        grid=(HGRN_HEADS, HGRN_D // HGRN_KT),
        in_specs=[col(HGRN_COL0), col(HGRN_COL0 + HGRN_W), col(HGRN_COL0 + 2 * HGRN_W),
                  col(HGRN_COL0 + 3 * HGRN_W),
                  pl.BlockSpec((logits.shape[0], HGRN_D), lambda h, k: (0, h)),
                  pl.BlockSpec((1, HGRN_D), lambda h, k: (0, h)), sspec],
        out_specs=[pl.BlockSpec((steps, ninst, HGRN_D), lambda h, k: (0, 0, h)), sspec],
        out_shape=[jax.ShapeDtypeStruct((steps, ninst, HGRN_W), F32),
                   jax.ShapeDtypeStruct(s0.shape, F32)],
        scratch_shapes=[pltpu.VMEM((HGRN_KT, HGRN_D, ninst), F32), tbuf, tbuf, tbuf, tbuf, tbuf],
        compiler_params=_cp(2), name="hgrn_lanes")(
            proj_t, proj_t, proj_t, proj_t, logits, nw, s0)


def _rglru_body(xb_ref, gate_ref, chist_ref, hinit_ref, cw_ref, cb_ref, wa_ref, ba_ref, wx_ref,
                bx_ref, lam_ref, y_ref, cst_ref, hst_ref, ccarry_ref, hcarry_ref,
                *, long_seq, tps):
    x = xb_ref[...]
    tm, c = x.shape
    if long_seq:
        @pl.when(pl.program_id(0) % tps == 0)
        def _():
            ccarry_ref[...] = chist_ref[...]
            hcarry_ref[...] = hinit_ref[...]
        hist = ccarry_ref[...]
    else:
        hist = chist_ref[...]
    p1, p2, p3 = _prev_rows(x, hist, (1, 2, 3), long_seq)
    cw = cw_ref[...]
    xc = cb_ref[...] + cw[0:1] * p3 + cw[1:2] * p2 + cw[2:3] * p1 + cw[3:4] * x
    xcb = xc.astype(BF16)
    r = jax.nn.sigmoid(_dot(xcb, wa_ref[...]) + ba_ref[...])
    ig = jax.nn.sigmoid(_dot(xcb, wx_ref[...]) + bx_ref[...])
    log_a = (-LRU_C) * r * _softplus(-lam_ref[...])
    a = jnp.exp(log_a)
    h = jnp.sqrt(-_expm1(2.0 * log_a)) * (ig * xc)
    if long_seq:
        pos = lax.broadcasted_iota(jnp.int32, (tm, c), 0)
        k = 1
        while k < tm:
            keep = pos >= k
            h = h + a * jnp.where(keep, pltpu.roll(h, k, 0), 0.0)
            a = a * jnp.where(keep, pltpu.roll(a, k, 0), 1.0)
            k *= 2
        h = h + a * hcarry_ref[SUBLANES - 1:SUBLANES, :]
        hcarry_ref[...] = h[tm - SUBLANES:]
        ccarry_ref[...] = x[tm - SUBLANES:]
        cst_ref[...] = x[tm - SUBLANES:]
        hst_ref[...] = h[tm - SUBLANES:]
    else:
        shp = (tm // SUBLANES, SUBLANES, c)
        h3, a3 = h.reshape(shp), a.reshape(shp)
        pos = lax.broadcasted_iota(jnp.int32, shp, 1)
        k = 1
        while k < SUBLANES:
            keep = pos >= k
            h3 = h3 + a3 * jnp.where(keep, pltpu.roll(h3, k, 1), 0.0)
            a3 = a3 * jnp.where(keep, pltpu.roll(a3, k, 1), 1.0)
            k *= 2
        h = (h3 + a3 * hinit_ref[...].reshape(shp)).reshape(tm, c)
        cst_ref[...] = x
        hst_ref[...] = h
    y_ref[...] = h * jax.nn.gelu(gate_ref[...])


def _rglru(proj, chist, hinit, lp, seq_len, tm):
    n = proj.shape[0]
    long_seq = seq_len > SUBLANES
    tps = max(seq_len // tm, 1)
    c0 = LRU_COL0 // LRU_W
    if long_seq:
        hs = pl.BlockSpec((SUBLANES, LRU_W), lambda i: (i // tps, 0))
    else:
        hs = pl.BlockSpec((tm, LRU_W), lambda i: (i, 0))
    row = pl.BlockSpec((1, LRU_W), lambda i: (0, 0))
    sq = pl.BlockSpec((LRU_W, LRU_W), lambda i: (0, 0))
    return pl.pallas_call(
        functools.partial(_rglru_body, long_seq=long_seq, tps=tps), grid=(n // tm,),
        in_specs=[pl.BlockSpec((tm, LRU_W), lambda i: (i, c0)),
                  pl.BlockSpec((tm, LRU_W), lambda i: (i, c0 + 1)), hs, hs,
                  pl.BlockSpec((4, LRU_W), lambda i: (0, 0)), row, sq, row, sq, row, row],
        out_specs=[pl.BlockSpec((tm, LRU_W), lambda i: (i, 0)), hs, hs],
        out_shape=[jax.ShapeDtypeStruct((n, LRU_W), F32),
                   jax.ShapeDtypeStruct(chist.shape, F32),
                   jax.ShapeDtypeStruct(chist.shape, F32)],
        scratch_shapes=[pltpu.VMEM((SUBLANES, LRU_W), F32), pltpu.VMEM((SUBLANES, LRU_W), F32)],
        compiler_params=_cp(1), name="rglru")(
            proj, proj, chist, hinit, lp["lru_cw"], lp["lru_cb"], lp["lru_wa"], lp["lru_ba"],
            lp["lru_wx"], lp["lru_bx"], lp["lru_lam"])


def _hist(state):
    nseq, k, c = state.shape
    return jnp.pad(state, ((0, 0), (SUBLANES - k, 0), (0, 0))).reshape(nseq * SUBLANES, c)


def _block_diag(w):
    h, a, b = w.shape
    eye = jnp.eye(h, dtype=w.dtype)
    return (eye[:, None, :, None] * w[:, :, None, :]).reshape(h * a, h * b)


def _layer(x3, st, lp, consts, layer, last):
    nseq, seq_len, d = x3.shape
    n = nseq * seq_len
    long_seq = seq_len > SUBLANES
    s_rw, shift_rw, s_hg, h_lru, buf_lru, buf_ffn = st
    x = x3.reshape(n, d)
    ones = consts["ones"]

    proj = _in_proj(x, lp["norm_mix"], consts["w_in"], layer, tm=min(n, 1024), tn=512)

    tm_pre = min(n, seq_len if long_seq else n, 256)
    outs = _rwkv_pre(proj, _hist(shift_rw[:, None, :]), lp, ones, seq_len, tm_pre)
    r, w, k, v, na, kb, g, bonus, shift_out = outs
    new_shift = shift_out.reshape(nseq, SUBLANES, RWKV_COLS)[:, SUBLANES - 1]
    if long_seq:
        c = RWKV_CHUNK
        n4 = _rwkv_n(w, na, kb, consts["lmat"])
        t3 = _tri_solve(n4.reshape(RWKV_HEADS, n // c, c * c))
        y, spt = _rwkv_chunk((r, w, k, v, na, kb), t3.reshape(n4.shape), consts["lmat"], nseq)
        sp = spt.reshape(nseq, RWKV_HEADS // 2, 2, RWKV_N, 2, RWKV_N)
        new_s_rw = jnp.stack([sp[:, :, 0, :, 0, :], sp[:, :, 1, :, 1, :]], axis=2)
        new_s_rw = new_s_rw.reshape(nseq, RWKV_HEADS, RWKV_N, RWKV_N).transpose(0, 1, 3, 2)
    else:
        tmaj = lambda t: t.reshape(nseq, seq_len, RWKV_W).transpose(1, 0, 2)
        s0 = s_rw.reshape(nseq, RWKV_HEADS, RWKV_N * RWKV_N).transpose(1, 0, 2)
        yb, sf = _rwkv_lanes([tmaj(t) for t in (r, w, k, v, na, kb)], s0)
        y = yb.transpose(1, 0, 2).reshape(n, RWKV_W)
        new_s_rw = sf.transpose(1, 0, 2).reshape(nseq, RWKV_HEADS, RWKV_N, RWKV_N)
    y_rw = _rwkv_post(y, bonus, g, lp["ln_w"], lp["ln_b"], ones, tm=min(n, 512))

    if long_seq:
        y_hg, new_s_hg = _hgrn_prompt(proj, consts["lb_logits"], lp["hgrn_nw"], consts["cmat"],
                                      nseq, seq_len, layer)
    else:
        proj_t = proj.reshape(nseq, seq_len, IN_COLS).transpose(1, 0, 2)
        s0 = s_hg.reshape(nseq, HGRN_HEADS, HGRN_D * HGRN_D).transpose(1, 0, 2)
        o_t, sf = _hgrn_lanes(proj_t, consts["lb_logits"], lp["hgrn_nw"], s0, layer)
        y_hg = o_t.transpose(1, 0, 2).reshape(n, HGRN_W)
        new_s_hg = sf.transpose(1, 0, 2).reshape(nseq, HGRN_HEADS, HGRN_D, HGRN_D)

    if long_seq:
        hinit = _hist(h_lru[:, None, :])
        tm_lru = min(seq_len, 256)
    else:
        hinit = jnp.broadcast_to(h_lru[:, None, :], (nseq, seq_len, LRU_W)).reshape(n, LRU_W)
        tm_lru = min(n, 512)
    y_lru, cst, hst = _rglru(proj, _hist(buf_lru), hinit, lp, seq_len, tm_lru)
    new_buf_lru = cst.reshape(nseq, SUBLANES, LRU_W)[:, SUBLANES - 3:]
    new_h_lru = hst.reshape(nseq, SUBLANES, LRU_W)[:, SUBLANES - 1]

    x = _out_proj(x, y_rw, y_hg, y_lru, consts["w_out"], layer, tm=min(n, 1024), tn=512)
    tm_ffn = min(n, seq_len if long_seq else n, 512)
    x, fst = _ffn(x, lp["norm_ffn"], consts["w_up"], lp["ffn_cw"], lp["ffn_cb"], consts["w_down"],
                  _hist(buf_ffn), layer, seq_len, tm=tm_ffn, tf=512)
    new_buf_ffn = fst.reshape(nseq, -1, SUBLANES, D_FF)[:, -1, SUBLANES - 2:]
    return x.reshape(nseq, seq_len, d), (new_s_rw, new_shift, new_s_hg, new_h_lru, new_buf_lru,
                                         new_buf_ffn)


def kernel(x_prompt, x_sample, state_rwkv, state_rwkv_shift, state_hgrn, state_rglru, cache_rglru_conv, cache_ffn_conv, norm_mix, w_in, rwkv_mu, rwkv_w0, rwkv_w2, rwkv_a0, rwkv_a2, rwkv_g2, rwkv_k_k, rwkv_k_a, rwkv_r_k, rwkv_ln_w, rwkv_ln_b, hgrn_lb_logits, hgrn_norm_w, rglru_conv_w, rglru_conv_b, rglru_wa, rglru_ba, rglru_wx, rglru_bx, rglru_lambda, w_out, norm_ffn, ffn_w_up, ffn_conv_w, ffn_conv_b, ffn_w_down, norm_final):
    depth = w_in.shape[0]
    nb, nt, _ = x_prompt.shape
    db = x_sample.shape[0]
    consts = {
        "ones": jnp.asarray(np.kron(np.eye(RWKV_HEADS), np.ones((RWKV_N, RWKV_N))), dtype=BF16),
        "cmat": _hgrn_level_matrix(),
        "lmat": jnp.asarray(np.tril(np.ones((RWKV_CHUNK, RWKV_CHUNK), np.float32)), dtype=BF16),
        "lb_logits": hgrn_lb_logits,
        "w_in": w_in.astype(BF16), "w_out": w_out.astype(BF16),
        "w_up": ffn_w_up.astype(BF16), "w_down": ffn_w_down.astype(BF16),
    }
    rowv = lambda a: a.reshape(1, -1)
    x_p, x_s = x_prompt, x_sample
    new_p, new_s = [], []
    for l in range(depth):
        lp = {
            "norm_mix": rowv(norm_mix[l]), "mu": rowv(rwkv_mu[l]),
            "w0": rowv(rwkv_w0[l]),
            "w2p": jnp.pad(rwkv_w2[l], ((0, 64), (0, 0))).astype(BF16),
            "a0": rowv(rwkv_a0[l]),
            "a2p": jnp.pad(rwkv_a2[l], ((64, 0), (0, 0))).astype(BF16),
            "g2": rwkv_g2[l].astype(BF16), "k_k": rowv(rwkv_k_k[l]), "k_a": rowv(rwkv_k_a[l]),
            "r_k": rowv(rwkv_r_k[l]), "ln_w": rowv(rwkv_ln_w[l]), "ln_b": rowv(rwkv_ln_b[l]),
            "hgrn_nw": rowv(hgrn_norm_w[l]),
            "lru_cw": rglru_conv_w[l], "lru_cb": rowv(rglru_conv_b[l]),
            "lru_wa": _block_diag(rglru_wa[l]).astype(BF16), "lru_ba": rowv(rglru_ba[l]),
            "lru_wx": _block_diag(rglru_wx[l]).astype(BF16), "lru_bx": rowv(rglru_bx[l]),
            "lru_lam": rowv(rglru_lambda[l]),
            "norm_ffn": rowv(norm_ffn[l]), "ffn_cw": ffn_conv_w[l],
            "ffn_cb": rowv(ffn_conv_b[l]),
        }
        zero = lambda *s: jnp.zeros(s, F32)
        st_p = (None, zero(nb, RWKV_COLS), None, zero(nb, LRU_W), zero(nb, 3, LRU_W),
                zero(nb, 2, D_FF))
        x_p, sp = _layer(x_p, st_p, lp, consts, l, l == depth - 1)
        st_s = (state_rwkv[l], state_rwkv_shift[l], state_hgrn[l], state_rglru[l],
                cache_rglru_conv[l], cache_ffn_conv[l])
        x_s, ss = _layer(x_s, st_s, lp, consts, l, l == depth - 1)
        new_p.append(sp)
        new_s.append(ss)
    nf = rowv(norm_final)
    y_prompt = _final_norm(x_p.reshape(nb * nt, D_MODEL), nf, min(nb * nt, 1024)).reshape(x_prompt.shape)
    y_sample = _final_norm(x_s.reshape(-1, D_MODEL), nf, min(x_s.shape[0] * x_s.shape[1], 1024)).reshape(x_sample.shape)
    stack = lambda sts: [jnp.stack(s, axis=0) for s in zip(*sts)]
    return (y_prompt, y_sample, *stack(new_p), *stack(new_s))
```

```python
import functools

import numpy as np
import jax
import jax.numpy as jnp
from jax import lax
from jax.experimental import pallas as pl
from jax.experimental.pallas import tpu as pltpu

F32 = jnp.float32
BF16 = jnp.bfloat16

D_MODEL = 2048
RWKV_HEADS = 12
RWKV_N = 64
RWKV_W = RWKV_HEADS * RWKV_N
RWKV_COLS = 2560
RWKV_LN_EPS = 64e-5
HGRN_HEADS = 6
HGRN_D = 128
HGRN_W = HGRN_HEADS * HGRN_D
HGRN_COL0 = RWKV_COLS
LRU_W = 512
LRU_COL0 = RWKV_COLS + 4 * HGRN_W
LRU_C = 8.0
IN_COLS = 6656
D_FF = 5632
NORM_EPS = 1e-6

SUBLANES = 8
LANES = 128
RWKV_CHUNK = 64
HGRN_CHUNK = 128
HGRN_LEVELS = 7
VMEM_LIMIT = 48 * 1024 * 1024
FFN_VMEM_LIMIT = 56 * 1024 * 1024


def _cp(n):
    return pltpu.CompilerParams(dimension_semantics=("arbitrary",) * n,
                                vmem_limit_bytes=VMEM_LIMIT)


def _dot(a, b):
    return jnp.dot(a, b, preferred_element_type=F32)


def _dot_nt(a, b):
    return lax.dot_general(a, b, (((1,), (1,)), ((), ())), preferred_element_type=F32)


def _split(x):
    hi = x.astype(BF16)
    lo = (x - hi.astype(F32)).astype(BF16)
    return hi, lo


def _split_dot(x, m):
    hi, lo = _split(x)
    return _dot(hi, m) + _dot(lo, m)


def _dot3(a, b, nt=False):
    f = _dot_nt if nt else _dot
    ah, al = _split(a)
    bh, bl = _split(b)
    return f(ah, bh) + (f(ah, bl) + f(al, bh))


def _softplus(z):
    return jnp.maximum(z, 0.0) + jnp.log1p(jnp.exp(-jnp.abs(z)))


def _expm1(z):
    return jnp.tanh(0.5 * z) * (jnp.exp(z) + 1.0)


def _prev_rows(x, hist, ks, long_seq):
    tm, c = x.shape
    if long_seq:
        ext = jnp.concatenate([hist, x], axis=0)
        return [pltpu.roll(ext, k, 0)[SUBLANES:] for k in ks]
    x3 = x.reshape(tm // SUBLANES, SUBLANES, c)
    h3 = hist.reshape(tm // SUBLANES, SUBLANES, c)
    pos = lax.broadcasted_iota(jnp.int32, x3.shape, 1)
    return [jnp.where(pos >= k, pltpu.roll(x3, k, 1), pltpu.roll(h3, k, 1)).reshape(tm, c)
            for k in ks]


def _in_proj_body(x_ref, nw_ref, w_ref, o_ref, xn_ref):
    @pl.when(pl.program_id(1) == 0)
    def _():
        x = x_ref[...]
        ms = jnp.mean(x * x, axis=-1, keepdims=True)
        xn_ref[...] = (x * lax.rsqrt(ms + NORM_EPS) * nw_ref[...]).astype(BF16)
    o_ref[...] = _dot(xn_ref[...], w_ref[...])


def _in_proj(x, nw, w, layer, tm, tn):
    n, d = x.shape
    c = w.shape[2]
    return pl.pallas_call(
        _in_proj_body, grid=(n // tm, c // tn),
        in_specs=[pl.BlockSpec((tm, d), lambda i, j: (i, 0)),
                  pl.BlockSpec((1, d), lambda i, j: (0, 0)),
                  pl.BlockSpec((None, d, tn), lambda i, j: (layer, 0, j))],
        out_specs=pl.BlockSpec((tm, tn), lambda i, j: (i, j)),
        out_shape=jax.ShapeDtypeStruct((n, c), F32),
        scratch_shapes=[pltpu.VMEM((tm, d), BF16)],
        compiler_params=_cp(2), name="in_proj")(x, nw, w)


def _out_proj_body(x_ref, ya_ref, yb_ref, yc_ref, w_ref, o_ref, y_ref):
    @pl.when(pl.program_id(1) == 0)
    def _():
        y_ref[:, 0:RWKV_W] = ya_ref[...].astype(BF16)
        y_ref[:, RWKV_W:RWKV_W + HGRN_W] = yb_ref[...].astype(BF16)
        y_ref[:, RWKV_W + HGRN_W:] = yc_ref[...].astype(BF16)
    o_ref[...] = x_ref[...] + _dot(y_ref[...], w_ref[...])


def _out_proj(x, ya, yb, yc, w, layer, tm, tn):
    n, d = x.shape
    return pl.pallas_call(
        _out_proj_body, grid=(n // tm, d // tn),
        in_specs=[pl.BlockSpec((tm, tn), lambda i, j: (i, j)),
                  pl.BlockSpec((tm, RWKV_W), lambda i, j: (i, 0)),
                  pl.BlockSpec((tm, HGRN_W), lambda i, j: (i, 0)),
                  pl.BlockSpec((tm, LRU_W), lambda i, j: (i, 0)),
                  pl.BlockSpec((None, d, tn), lambda i, j: (layer, 0, j))],
        out_specs=pl.BlockSpec((tm, tn), lambda i, j: (i, j)),
        out_shape=jax.ShapeDtypeStruct((n, d), F32),
        scratch_shapes=[pltpu.VMEM((tm, d), BF16)],
        compiler_params=_cp(2), name="out_proj")(x, ya, yb, yc, w)


def _ffn_body(*refs, long_seq, tps, final):
    x_ref, nw_ref, wg_ref, wv_ref, cw_ref, cb_ref, wd_ref, hist_ref = refs[:8]
    if final:
        fnw_ref, st_ref, on_ref, hn_ref, carry_ref, o_ref = refs[8:]
    else:
        o_ref, st_ref, hn_ref, carry_ref = refs[8:]
    i = pl.program_id(0)
    j = pl.program_id(1)

    @pl.when(j == 0)
    def _():
        x = x_ref[...]
        ms = jnp.mean(x * x, axis=-1, keepdims=True)
        hn_ref[...] = (x * lax.rsqrt(ms + NORM_EPS) * nw_ref[...]).astype(BF16)
        o_ref[...] = x

    hn = hn_ref[...]
    g = _dot(hn, wg_ref[...])
    v = _dot(hn, wv_ref[...])
    if long_seq:
        @pl.when(i % tps == 0)
        def _():
            carry_ref[j] = hist_ref[...]
        hist = carry_ref[j]
    else:
        hist = hist_ref[...]
    p1, p2 = _prev_rows(g, hist, (1, 2), long_seq)
    cw = cw_ref[...]
    gc = cb_ref[...] + cw[0:1] * p2 + cw[1:2] * p1 + cw[2:3] * g
    h = (gc * jax.nn.sigmoid(gc)) * v
    o_ref[...] += _dot(h.astype(BF16), wd_ref[...])
    if long_seq:
        carry_ref[j] = g[g.shape[0] - SUBLANES:]
        st_ref[...] = g[g.shape[0] - SUBLANES:]
    else:
        st_ref[...] = g

    if final:
        @pl.when(j == pl.num_programs(1) - 1)
        def _():
            y = o_ref[...]
            ms = jnp.mean(y * y, axis=-1, keepdims=True)
            on_ref[...] = y * lax.rsqrt(ms + NORM_EPS) * fnw_ref[...]


def _ffn(x, nw, w_up, cw, cb, w_down, hist, layer, seq_len, tm, tf, final_nw=None):
    n, d = x.shape
    final = final_nw is not None
    nf = D_FF // tf
    long_seq = seq_len > SUBLANES
    tps = max(seq_len // tm, 1)
    if long_seq:
        hist_spec = pl.BlockSpec((SUBLANES, tf), lambda i, j: (i // tps, j))
        st_spec = pl.BlockSpec((SUBLANES, tf), lambda i, j: (i, j))
        st_rows = n // tm * SUBLANES
    else:
        hist_spec = pl.BlockSpec((tm, tf), lambda i, j: (i, j))
        st_spec = pl.BlockSpec((tm, tf), lambda i, j: (i, j))
        st_rows = n
    body = functools.partial(_ffn_body, long_seq=long_seq, tps=tps, final=final)
    row_tile = pl.BlockSpec((tm, d), lambda i, j: (i, 0))
    wide = pl.BlockSpec((1, d), lambda i, j: (0, 0))
    x_tile = (pl.BlockSpec((tm, d), lambda i, j: (i, 0), pipeline_mode=pl.Buffered(1))
              if final else row_tile)
    in_specs = [x_tile, wide,
                pl.BlockSpec((None, d, tf), lambda i, j: (layer, 0, j)),
                pl.BlockSpec((None, d, tf), lambda i, j: (layer, 0, j + nf)),
                pl.BlockSpec((3, tf), lambda i, j: (0, j)),
                pl.BlockSpec((1, tf), lambda i, j: (0, j)),
                pl.BlockSpec((None, tf, d), lambda i, j: (layer, j, 0)),
                hist_spec]
    args = [x, nw, w_up, w_up, cw, cb, w_down, hist]
    out_specs = [row_tile, st_spec]
    out_shape = [jax.ShapeDtypeStruct((n, d), F32), jax.ShapeDtypeStruct((st_rows, D_FF), F32)]
    scratch = [pltpu.VMEM((tm, d), BF16), pltpu.VMEM((nf, SUBLANES, tf), F32)]
    if final:
        in_specs.append(wide)
        args.append(final_nw)
        out_specs.reverse()
        out_shape.reverse()
        scratch.append(pltpu.VMEM((tm, d), F32))
    return pl.pallas_call(
        body, grid=(n // tm, nf), in_specs=in_specs, out_specs=out_specs, out_shape=out_shape,
        scratch_shapes=scratch,
        compiler_params=pltpu.CompilerParams(dimension_semantics=("arbitrary",) * 2,
                                             vmem_limit_bytes=FFN_VMEM_LIMIT),
        name="ffn")(*args)


def _rwkv_pre_body(p_ref, hist_ref, mu_ref, w0_ref, w2_ref, a0_ref, a2_ref, g2_ref, kk_ref,
                   ka_ref, rk_ref, ones_ref,
                   r_ref, w_ref, k_ref, v_ref, na_ref, kb_ref, g_ref, bonus_ref, st_ref,
                   carry_ref, *, long_seq, tps):
    p = p_ref[...]
    if long_seq:
        @pl.when(pl.program_id(0) % tps == 0)
        def _():
            carry_ref[...] = hist_ref[...]
        hist = carry_ref[...]
    else:
        hist = hist_ref[...]
    (prev,) = _prev_rows(p, hist, (1,), long_seq)
    if long_seq:
        carry_ref[...] = p[p.shape[0] - SUBLANES:]
        st_ref[...] = p[p.shape[0] - SUBLANES:]
    else:
        st_ref[...] = p
    xs = p + (prev - p) * mu_ref[...]
    r = xs[:, 0:RWKV_W]
    k = xs[:, RWKV_W:2 * RWKV_W]
    v = xs[:, 2 * RWKV_W:3 * RWKV_W]
    xwa = xs[:, 3 * RWKV_W:3 * RWKV_W + 128]
    xg = xs[:, 3 * RWKV_W + 128:RWKV_COLS]
    zw = w0_ref[...] + _dot(jnp.tanh(xwa).astype(BF16), w2_ref[...])
    w_log = -_softplus(-zw) - 0.5
    decay = jnp.exp(-jnp.exp(w_log))
    a = jax.nn.sigmoid(a0_ref[...] + _dot(xwa.astype(BF16), a2_ref[...]))
    g = _dot(jax.nn.sigmoid(xg).astype(BF16), g2_ref[...])
    ones = ones_ref[...]
    kk = k * kk_ref[...]
    kk = kk / jnp.maximum(jnp.sqrt(_split_dot(kk * kk, ones)), 1e-12)
    kf = k * (1.0 + (a - 1.0) * ka_ref[...])
    r_ref[...] = r
    w_ref[...] = decay
    k_ref[...] = kf
    v_ref[...] = v
    na_ref[...] = -kk
    kb_ref[...] = kk * a
    g_ref[...] = g
    bonus_ref[...] = _split_dot(r * kf * rk_ref[...], ones) * v


def _rwkv_pre(proj, hist, lp, ones, seq_len, tm):
    n = proj.shape[0]
    long_seq = seq_len > SUBLANES
    tps = max(seq_len // tm, 1)
    row = lambda c: pl.BlockSpec((1, c), lambda i: (0, 0))
    full = lambda a, b: pl.BlockSpec((a, b), lambda i: (0, 0))
    if long_seq:
        hist_spec = pl.BlockSpec((SUBLANES, RWKV_COLS), lambda i: (i // tps, 0))
    else:
        hist_spec = pl.BlockSpec((tm, RWKV_COLS), lambda i: (i, 0))
    tile = pl.BlockSpec((tm, RWKV_W), lambda i: (i, 0))
    body = functools.partial(_rwkv_pre_body, long_seq=long_seq, tps=tps)
    outs = pl.pallas_call(
        body, grid=(n // tm,),
        in_specs=[pl.BlockSpec((tm, RWKV_COLS), lambda i: (i, 0)), hist_spec,
                  row(RWKV_COLS), row(RWKV_W), full(128, RWKV_W), row(RWKV_W),
                  full(128, RWKV_W), full(128, RWKV_W), row(RWKV_W), row(RWKV_W), row(RWKV_W),
                  full(RWKV_W, RWKV_W)],
        out_specs=[tile] * 8 + [hist_spec],
        out_shape=[jax.ShapeDtypeStruct((n, RWKV_W), F32)] * 8
                  + [jax.ShapeDtypeStruct(hist.shape, F32)],
        scratch_shapes=[pltpu.VMEM((SUBLANES, RWKV_COLS), F32)],
        compiler_params=_cp(1), name="rwkv_pre")(
            proj, hist, lp["mu"], lp["w0"], lp["w2p"], lp["a0"], lp["a2p"], lp["g2"],
            lp["k_k"], lp["k_a"], lp["r_k"], ones)
    return outs


def _rwkv_post_body(y_ref, bonus_ref, g_ref, lw_ref, lb_ref, ones_ref, o_ref):
    y = y_ref[...]
    ones = ones_ref[...]
    mean = _split_dot(y, ones) * (1.0 / RWKV_N)
    d = y - mean
    var = _split_dot(d * d, ones) * (1.0 / RWKV_N)
    yn = d * lax.rsqrt(var + RWKV_LN_EPS)
    o_ref[...] = (yn * lw_ref[...] + lb_ref[...] + bonus_ref[...]) * g_ref[...]


def _rwkv_post(y, bonus, g, lw, lb, ones, tm):
    n = y.shape[0]
    tile = pl.BlockSpec((tm, RWKV_W), lambda i: (i, 0))
    row = pl.BlockSpec((1, RWKV_W), lambda i: (0, 0))
    return pl.pallas_call(
        _rwkv_post_body, grid=(n // tm,),
        in_specs=[tile, tile, tile, row, row, pl.BlockSpec((RWKV_W, RWKV_W), lambda i: (0, 0))],
        out_specs=tile, out_shape=jax.ShapeDtypeStruct((n, RWKV_W), F32),
        compiler_params=_cp(1), name="rwkv_post")(y, bonus, g, lw, lb, ones)


def _rwkv_lanes_body(r_ref, w_ref, k_ref, v_ref, a_ref, b_ref, s0_ref, y_ref, sf_ref,
                     s_ref, xt_ref, y_buf):
    tb = pl.program_id(1)
    ninst = r_ref.shape[1]

    @pl.when(tb == 0)
    def _():
        for hh in range(2):
            s_ref[hh] = s0_ref[hh]

    def step(tl, carry):
        for idx, ref in enumerate((r_ref, w_ref, k_ref, v_ref, a_ref, b_ref)):
            xt_ref[idx] = ref[tl].T
        for hh in range(2):
            base = hh * RWKV_N
            heads = lambda i: xt_ref[i, base:base + RWKV_N, :]

            def row(rho, c):
                s = s_ref[hh, rho]
                sa = jnp.sum(s * heads(4), axis=0, keepdims=True)
                vrow = xt_ref[3, pl.ds(base + rho, 1), :]
                s = s * heads(1) + sa * heads(5) + vrow * heads(2)
                s_ref[hh, rho] = s
                y_buf[pl.ds(base + rho, 1), :] = jnp.sum(s * heads(0), axis=0, keepdims=True)
                return c

            lax.fori_loop(0, RWKV_N, row, 0, unroll=8)
        y_ref[tl] = y_buf[...].T
        return carry

    lax.fori_loop(0, SUBLANES, step, 0)

    @pl.when(tb == pl.num_programs(1) - 1)
    def _():
        for hh in range(2):
            sf_ref[hh] = s_ref[hh]


def _rwkv_lanes(xs, s0):
    steps, ninst, _ = xs[0].shape
    xspec = pl.BlockSpec((SUBLANES, ninst, 2 * RWKV_N), lambda h, t: (t, 0, h))
    sspec = pl.BlockSpec((2, RWKV_N, RWKV_N, ninst), lambda h, t: (h, 0, 0, 0))
    return pl.pallas_call(
        _rwkv_lanes_body, grid=(RWKV_HEADS // 2, steps // SUBLANES),
        in_specs=[xspec] * 6 + [sspec], out_specs=[xspec, sspec],
        out_shape=[jax.ShapeDtypeStruct((steps, ninst, RWKV_W), F32),
                   jax.ShapeDtypeStruct(s0.shape, F32)],
        scratch_shapes=[pltpu.VMEM((2, RWKV_N, RWKV_N, ninst), F32),
                        pltpu.VMEM((6, 2 * RWKV_N, ninst), F32),
                        pltpu.VMEM((2 * RWKV_N, ninst), F32)],
        compiler_params=_cp(2), name="rwkv_lanes")(*xs, s0)


RWKV_N_CHUNKS = 2


def _rwkv_n_body(w_ref, a_ref, b_ref, l_ref, n_ref):
    c = RWKV_CHUNK
    rows = RWKV_N_CHUNKS * c
    lmat = l_ref[...]
    strict = lax.broadcasted_iota(jnp.int32, (c, c), 1) < lax.broadcasted_iota(jnp.int32, (c, c), 0)
    lane = lax.broadcasted_iota(jnp.int32, (rows, 2 * RWKV_N), 1)
    pairs = range(RWKV_HEADS // 2)
    sls = [slice(p * 2 * RWKV_N, (p + 1) * 2 * RWKV_N) for p in pairs]
    lws = [jnp.log(w_ref[:, sl]) for sl in sls]
    cws = [_split_dot_left(lmat, lw) for lw in lws]
    ats = [_split(a_ref[:, sl] * jnp.exp(cw - lw)) for sl, cw, lw in zip(sls, cws, lws)]
    bts = [_split(b_ref[:, sl] * jnp.exp(-cw)) for sl, cw in zip(sls, cws)]
    for p in pairs:
        for hh in range(2):
            m = (lane < RWKV_N) if hh == 0 else (lane >= RWKV_N)
            ah = jnp.where(m, ats[p][0], jnp.zeros_like(ats[p][0]))
            al = jnp.where(m, ats[p][1], jnp.zeros_like(ats[p][1]))
            for ci in range(RWKV_N_CHUNKS):
                rs = slice(ci * c, (ci + 1) * c)
                bh, bl = bts[p][0][rs], bts[p][1][rs]
                n = _dot_nt(ah[rs], bh) + (_dot_nt(ah[rs], bl) + _dot_nt(al[rs], bh))
                n_ref[2 * p + hh, ci] = jnp.where(strict, n, 0.0)


def _rwkv_n(w, a, b, lmat2):
    n = w.shape[0]
    c = RWKV_CHUNK
    rows = RWKV_N_CHUNKS * c
    tile = pl.BlockSpec((rows, RWKV_W), lambda i: (i, 0))
    return pl.pallas_call(
        _rwkv_n_body, grid=(n // rows,),
        in_specs=[tile, tile, tile, pl.BlockSpec((rows, rows), lambda i: (0, 0))],
        out_specs=pl.BlockSpec((RWKV_HEADS, RWKV_N_CHUNKS, c, c), lambda i: (0, i, 0, 0)),
        out_shape=jax.ShapeDtypeStruct((RWKV_HEADS, n // c, c, c), F32),
        compiler_params=_cp(1), name="rwkv_n")(w, a, b, lmat2)


def _tri_solve_body(nt_ref, tt_ref):
    c = RWKV_CHUNK
    ninst = nt_ref.shape[2]
    tt_ref[...] = jnp.zeros_like(tt_ref)
    g = SUBLANES
    nb = 4
    sub = lax.broadcasted_iota(jnp.int32, (g, ninst), 0)
    for t0 in range(0, c, nb):
        ngrp = (t0 + nb - 1) // g + 1
        rs = [[jnp.where(sub + g * j == t0 + i, 1.0, 0.0).astype(F32) for j in range(ngrp)]
              for i in range(nb)]
        for sg in range(-(-t0 // g)):

            def acc(s, carry, sg=sg, t0=t0):
                ts = [tt_ref[s, j * g:(j + 1) * g, :] for j in range(sg + 1)]
                nrow = [nt_ref[t0 + i, pl.ds(s, 1), :] for i in range(nb)]
                return tuple(tuple(carry[i][j] + nrow[i] * ts[j] for j in range(sg + 1))
                             for i in range(nb))

            res = lax.fori_loop(sg * g, min(sg * g + g, t0), acc,
                                tuple(tuple(rs[i][:sg + 1]) for i in range(nb)))
            for i in range(nb):
                rs[i][:sg + 1] = list(res[i])
        for i in range(nb):
            for j in range(i):
                nij = nt_ref[t0 + i, t0 + j:t0 + j + 1, :]
                rs[i] = [x + nij * y for x, y in zip(rs[i], rs[j])]
            tt_ref[t0 + i, 0:ngrp * g, :] = jnp.concatenate(rs[i], axis=0)


def _tri_solve(nt):
    nh, c, _, ninst = nt.shape
    spec = pl.BlockSpec((None, c, c, ninst), lambda h: (h, 0, 0, 0))
    return pl.pallas_call(
        _tri_solve_body, grid=(nh,), in_specs=[spec], out_specs=spec,
        out_shape=jax.ShapeDtypeStruct(nt.shape, F32),
        compiler_params=_cp(1), name="tri_solve")(nt)


def _rwkv_chunk_body(r_ref, w_ref, k_ref, v_ref, a_ref, b_ref, t_ref, l_ref, y_ref, sf_ref, s_ref):
    c = RWKV_CHUNK
    pw = 2 * RWKV_N
    ci = pl.program_id(1)

    @pl.when(ci == 0)
    def _():
        s_ref[...] = jnp.zeros_like(s_ref)

    lmat = l_ref[...]
    ti = lax.broadcasted_iota(jnp.int32, (c, c), 0)
    si = lax.broadcasted_iota(jnp.int32, (c, c), 1)
    strict, incl = si < ti, si <= ti
    lane = lax.broadcasted_iota(jnp.int32, (c, pw), 1)
    lane2 = lax.broadcasted_iota(jnp.int32, (2 * c, pw), 1)
    r2 = lax.broadcasted_iota(jnp.int32, (pw, pw), 0)
    c2 = lax.broadcasted_iota(jnp.int32, (pw, pw), 1)
    same_head = (r2 < RWKV_N) == (c2 < RWKV_N)
    eye = r2 == c2
    bz = lambda x: jnp.zeros_like(x)
    pairs = range(RWKV_HEADS // 2)
    sls = [slice(p * pw, (p + 1) * pw) for p in pairs]
    halves = [(p, hh) for p in pairs for hh in range(2)]
    lws = [jnp.log(w_ref[:, sl]) for sl in sls]
    cws = [_split_dot_left(lmat, lw) for lw in lws]
    vs = [v_ref[:, sl] for sl in sls]
    wts = [jnp.exp(cw) for cw in cws]
    rts = [r_ref[:, sl] * wt for sl, wt in zip(sls, wts)]
    ats = [a_ref[:, sl] * jnp.exp(cw - lw) for sl, cw, lw in zip(sls, cws, lws)]
    lhs = [_split(jnp.concatenate([at, rt], axis=0)) for at, rt in zip(ats, rts)]
    bts = [_split(b_ref[:, sl] * jnp.exp(-cw)) for sl, cw in zip(sls, cws)]
    kts = [_split(k_ref[:, sl] * jnp.exp(-cw)) for sl, cw in zip(sls, cws)]
    a_ak, a_rk, a_rb = {}, {}, {}
    for p, hh in halves:
        m2 = (lane2 < RWKV_N) if hh == 0 else (lane2 >= RWKV_N)
        lh = jnp.where(m2, lhs[p][0], bz(lhs[p][0]))
        ll = jnp.where(m2, lhs[p][1], bz(lhs[p][1]))
        pk = _dot_nt(lh, kts[p][0]) + (_dot_nt(lh, kts[p][1]) + _dot_nt(ll, kts[p][0]))
        prb = (_dot_nt(lh[c:], bts[p][0])
               + (_dot_nt(lh[c:], bts[p][1]) + _dot_nt(ll[c:], bts[p][0])))
        a_ak[p, hh] = jnp.where(strict, pk[:c], 0.0)
        a_rk[p, hh] = jnp.where(incl, pk[c:], 0.0).astype(BF16)
        a_rb[p, hh] = jnp.where(incl, prb, 0.0).astype(BF16)
    vps = {ph: _dot3(a_ak[ph], vs[ph[0]]) for ph in halves}
    gs = {(p, hh): _dot3(t_ref[2 * p + hh], jnp.concatenate([ats[p], vps[p, hh]], axis=1))
          for p, hh in halves}
    pick = lambda x0, x1: jnp.where(lane < RWKV_N, x0, x1)
    ahats = [pick(gs[p, 0][:, :pw], gs[p, 1][:, :pw]) for p in pairs]
    vhats = [pick(gs[p, 0][:, pw:], gs[p, 1][:, pw:]) for p in pairs]
    ahb = [x.astype(BF16) for x in ahats]
    vhb = [x.astype(BF16) for x in vhats]
    vbs = [x.astype(BF16) for x in vs]
    rhats = [rts[p] + pick(_dot(a_rb[p, 0], ahb[p]), _dot(a_rb[p, 1], ahb[p])) for p in pairs]
    yhats = [pick(_dot(a_rb[p, 0], vhb[p]) + _dot(a_rk[p, 0], vbs[p]),
                  _dot(a_rb[p, 1], vhb[p]) + _dot(a_rk[p, 1], vbs[p])) for p in pairs]
    spts = [s_ref[p] for p in pairs]
    for p in pairs:
        y_ref[:, sls[p]] = _dot3(rhats[p], spts[p]) + yhats[p]
    wends = [jnp.exp(cw[c - 1:c] - cw) for cw in cws]
    bkts = [jnp.concatenate([b_ref[:, sl] * we, k_ref[:, sl] * we], axis=0).T
            for sl, we in zip(sls, wends)]
    mpts = [jnp.where(same_head, _dot3(bkts[p], jnp.concatenate([ahats[p], bz(ahats[p])], axis=0)), 0.0)
            + jnp.where(eye, wts[p][c - 1:c], 0.0) for p in pairs]
    zpts = [jnp.where(same_head, _dot3(bkts[p], jnp.concatenate([vhats[p], vs[p]], axis=0)), 0.0)
            for p in pairs]
    for p in pairs:
        s_ref[p] = _dot3(mpts[p], spts[p]) + zpts[p]

    @pl.when(ci == pl.num_programs(1) - 1)
    def _():
        sf_ref[...] = s_ref[...]


def _rwkv_chunk(xs, t4, lmat, nseq):
    n = xs[0].shape[0]
    c = RWKV_CHUNK
    nch = n // c // nseq
    pw = 2 * RWKV_N
    tile = pl.BlockSpec((c, RWKV_W), lambda b, i: (b * nch + i, 0))
    return pl.pallas_call(
        _rwkv_chunk_body, grid=(nseq, nch),
        in_specs=[tile] * 6 + [pl.BlockSpec((RWKV_HEADS, None, c, c), lambda b, i: (0, b * nch + i, 0, 0)),
                               pl.BlockSpec((c, c), lambda b, i: (0, 0))],
        out_specs=[tile, pl.BlockSpec((None, RWKV_HEADS // 2, pw, pw), lambda b, i: (b, 0, 0, 0))],
        out_shape=[jax.ShapeDtypeStruct((n, RWKV_W), F32),
                   jax.ShapeDtypeStruct((nseq, RWKV_HEADS // 2, pw, pw), F32)],
        scratch_shapes=[pltpu.VMEM((RWKV_HEADS // 2, pw, pw), F32)],
        compiler_params=_cp(2), name="rwkv_chunk")(*xs, t4, lmat)


def _hgrn_lower_bound(logits, layer):
    m = jnp.max(logits, axis=0, keepdims=True)
    e = jnp.exp(logits - m)
    gam = e / jnp.sum(e, axis=0, keepdims=True)
    cs = gam[0:1]
    for i in range(1, layer + 1):
        cs = cs + gam[i:i + 1]
    return cs - gam[0:1]


def _hgrn_level_matrix():
    t = np.arange(HGRN_CHUNK)[:, None]
    s = np.arange(HGRN_CHUNK)[None, :]
    mats = []
    for upper in (False, True):
        for e in range(1, HGRN_LEVELS + 1):
            same = (t >> e) == (s >> e)
            mats.append(same & ((s > t) if upper else (s <= t)))
    return jnp.asarray(np.concatenate(mats, axis=0).astype(np.float32), dtype=BF16)


def _hgrn_prompt_body(*refs, layer):
    npair = HGRN_HEADS // 2
    q_refs, f_refs, i_refs, g_refs = (refs[i * npair:(i + 1) * npair] for i in range(4))
    lg_ref, nw_ref, cm_ref, o_ref, sf_ref, s_ref = refs[4 * npair:]
    c = pl.program_id(1)

    @pl.when(c == 0)
    def _():
        s_ref[...] = jnp.zeros_like(s_ref)

    n = HGRN_CHUNK
    heads = range(HGRN_HEADS)
    col = lambda rs, h: rs[h // 2][:, (h % 2) * HGRN_D:(h % 2 + 1) * HGRN_D]
    hsl = lambda h: slice(h * HGRN_D, (h + 1) * HGRN_D)
    t = lax.broadcasted_iota(jnp.int32, (n, n), 0)
    s_ = lax.broadcasted_iota(jnp.int32, (n, n), 1)
    lb_all = _hgrn_lower_bound(lg_ref[...], layer)
    cm = cm_ref[...]
    ones = jnp.ones((n, n), BF16)
    fgs = [lb_all[:, hsl(h)] + (1.0 - lb_all[:, hsl(h)]) * jax.nn.sigmoid(col(f_refs, h))
           for h in heads]
    lfs = [jnp.log(fg) for fg in fgs]
    css = [_split_dot_left(cm, lf) for lf in lfs]
    cl = lambda h, e: css[h][(e - 1) * n:e * n]
    cu = lambda h, e: css[h][(HGRN_LEVELS + e - 1) * n:(HGRN_LEVELS + e) * n]
    qs = [col(q_refs, h) * jax.nn.sigmoid(col(q_refs, h)) for h in heads]
    ks = [1.0 - fg for fg in fgs]
    vbs = [col(i_refs, h).astype(BF16) for h in heads]
    acc = [jnp.where(t == s_, _dot_nt(qs[h].astype(BF16), ks[h].astype(BF16)), 0.0) for h in heads]
    for e in range(HGRN_LEVELS):
        mask = (((t >> e) & 1) == 1) & ((s_ >> e) == ((t >> e) - 1))
        for h in heads:
            qe = qs[h] * (fgs[h] if e == 0 else jnp.exp(cl(h, e)))
            ke = ks[h] if e == 0 else ks[h] * jnp.exp(cu(h, e))
            acc[h] = acc[h] + jnp.where(mask, _dot_nt(qe.astype(BF16), ke.astype(BF16)), 0.0)
    sts = [s_ref[h] for h in heads]
    os_ = [_dot(acc[h].astype(BF16), vbs[h])
           + _dot((qs[h] * jnp.exp(cl(h, HGRN_LEVELS))).astype(BF16), sts[h].astype(BF16))
           for h in heads]
    dks = [jnp.exp(_split_dot(lfs[h].T, ones)) for h in heads]
    for h in heads:
        kf = ks[h] * jnp.exp(cu(h, HGRN_LEVELS))
        s_ref[h] = dks[h] * sts[h] + _dot(kf.T.astype(BF16), vbs[h])
    for h in heads:
        o = os_[h]
        ms = jnp.mean(o * o, axis=-1, keepdims=True)
        gr = col(g_refs, h)
        o_ref[:, hsl(h)] = (o * lax.rsqrt(ms + NORM_EPS) * nw_ref[:, hsl(h)]) * (gr * jax.nn.sigmoid(gr))

    @pl.when(c == pl.num_programs(1) - 1)
    def _():
        sf_ref[...] = s_ref[...]


def _split_dot_left(m, x):
    hi, lo = _split(x)
    return _dot(m, hi) + _dot(m, lo)


def _hgrn_prompt(proj, logits, nw, cmat, nseq, seq_len, layer):
    n = proj.shape[0]
    nch = seq_len // HGRN_CHUNK
    wb = 2 * HGRN_D
    npair = HGRN_HEADS // 2
    cols = [pl.BlockSpec((HGRN_CHUNK, wb), lambda b, c, j=(HGRN_COL0 + g * HGRN_W) // wb + p:
                         (b * nch + c, j)) for g in range(4) for p in range(npair)]
    return pl.pallas_call(
        functools.partial(_hgrn_prompt_body, layer=layer),
        grid=(nseq, nch),
        in_specs=cols + [pl.BlockSpec(logits.shape, lambda b, c: (0, 0)),
                         pl.BlockSpec((1, HGRN_W), lambda b, c: (0, 0)),
                         pl.BlockSpec(cmat.shape, lambda b, c: (0, 0))],
        out_specs=[pl.BlockSpec((HGRN_CHUNK, HGRN_W), lambda b, c: (b * nch + c, 0)),
                   pl.BlockSpec((None, HGRN_HEADS, HGRN_D, HGRN_D), lambda b, c: (b, 0, 0, 0))],
        out_shape=[jax.ShapeDtypeStruct((n, HGRN_W), F32),
                   jax.ShapeDtypeStruct((nseq, HGRN_HEADS, HGRN_D, HGRN_D), F32)],
        scratch_shapes=[pltpu.VMEM((HGRN_HEADS, HGRN_D, HGRN_D), F32)],
        compiler_params=_cp(2), name="hgrn_prompt")(*([proj] * (4 * npair)), logits, nw, cmat)


HGRN_KT = 32


def _hgrn_lanes_body(q_ref, f_ref, i_ref, g_ref, lg_ref, nw_ref, s0_ref, o_ref, sf_ref,
                     s_ref, qt_ref, ft_ref, kt_ref, vt_ref, oacc_ref, *, layer):
    kt = pl.program_id(1)
    ninst = q_ref.shape[1]
    lb = _hgrn_lower_bound(lg_ref[...], layer)
    s_ref[...] = s0_ref[0]

    @pl.when(kt == 0)
    def _():
        oacc_ref[...] = jnp.zeros_like(oacc_ref)
        for tl in range(SUBLANES):
            qr = q_ref[tl]
            fg = lb + (1.0 - lb) * jax.nn.sigmoid(f_ref[tl])
            qt_ref[tl] = (qr * jax.nn.sigmoid(qr)).T
            ft_ref[tl] = fg.T
            kt_ref[tl] = (1.0 - fg).T
            vt_ref[tl] = i_ref[tl].T

    def step(tl, carry):
        vt = vt_ref[tl]

        def krow(kk, o):
            row = kt * HGRN_KT + kk
            s = ft_ref[tl, pl.ds(row, 1), :] * s_ref[kk] + kt_ref[tl, pl.ds(row, 1), :] * vt
            s_ref[kk] = s
            return o + qt_ref[tl, pl.ds(row, 1), :] * s

        oacc_ref[tl] = lax.fori_loop(0, HGRN_KT, krow, oacc_ref[tl], unroll=2)
        return carry

    lax.fori_loop(0, SUBLANES, step, 0)
    sf_ref[0] = s_ref[...]

    @pl.when(kt == pl.num_programs(1) - 1)
    def _():
        for tl in range(SUBLANES):
            o = oacc_ref[tl].T
            ms = jnp.mean(o * o, axis=-1, keepdims=True)
            gr = g_ref[tl]
            o_ref[tl] = (o * lax.rsqrt(ms + NORM_EPS) * nw_ref[...]) * (gr * jax.nn.sigmoid(gr))


def _hgrn_lanes(proj_t, logits, nw, s0, layer):
    steps, ninst, _ = proj_t.shape
    col = lambda off: pl.BlockSpec((steps, ninst, HGRN_D),
                                   lambda h, k, off=off: (0, 0, off // HGRN_D + h))
    sspec = pl.BlockSpec((1, HGRN_KT, HGRN_D, ninst), lambda h, k: (h, k, 0, 0))
    tbuf = pltpu.VMEM((steps, HGRN_D, ninst), F32)
    return pl.pallas_call(
        functools.partial(_hgrn_lanes_body, layer=layer),
        grid=(HGRN_HEADS, HGRN_D // HGRN_KT),
        in_specs=[col(HGRN_COL0), col(HGRN_COL0 + HGRN_W), col(HGRN_COL0 + 2 * HGRN_W),
                  col(HGRN_COL0 + 3 * HGRN_W),
                  pl.BlockSpec((logits.shape[0], HGRN_D), lambda h, k: (0, h)),
                  pl.BlockSpec((1, HGRN_D), lambda h, k: (0, h)), sspec],
        out_specs=[pl.BlockSpec((steps, ninst, HGRN_D), lambda h, k: (0, 0, h)), sspec],
        out_shape=[jax.ShapeDtypeStruct((steps, ninst, HGRN_W), F32),
                   jax.ShapeDtypeStruct(s0.shape, F32)],
        scratch_shapes=[pltpu.VMEM((HGRN_KT, HGRN_D, ninst), F32), tbuf, tbuf, tbuf, tbuf, tbuf],
        compiler_params=_cp(2), name="hgrn_lanes")(
            proj_t, proj_t, proj_t, proj_t, logits, nw, s0)


def _rglru_body(xb_ref, gate_ref, chist_ref, hinit_ref, cw_ref, cb_ref, wa_ref, ba_ref, wx_ref,
                bx_ref, lam_ref, y_ref, cst_ref, hst_ref, ccarry_ref, hcarry_ref,
                *, long_seq, tps):
    x = xb_ref[...]
    tm, c = x.shape
    if long_seq:
        @pl.when(pl.program_id(0) % tps == 0)
        def _():
            ccarry_ref[...] = chist_ref[...]
            hcarry_ref[...] = hinit_ref[...]
        hist = ccarry_ref[...]
    else:
        hist = chist_ref[...]
    p1, p2, p3 = _prev_rows(x, hist, (1, 2, 3), long_seq)
    cw = cw_ref[...]
    xc = cb_ref[...] + cw[0:1] * p3 + cw[1:2] * p2 + cw[2:3] * p1 + cw[3:4] * x
    xcb = xc.astype(BF16)
    r = jax.nn.sigmoid(_dot(xcb, wa_ref[...]) + ba_ref[...])
    ig = jax.nn.sigmoid(_dot(xcb, wx_ref[...]) + bx_ref[...])
    log_a = (-LRU_C) * r * _softplus(-lam_ref[...])
    a = jnp.exp(log_a)
    h = jnp.sqrt(-_expm1(2.0 * log_a)) * (ig * xc)
    if long_seq:
        pos = lax.broadcasted_iota(jnp.int32, (tm, c), 0)
        k = 1
        while k < tm:
            keep = pos >= k
            h = h + a * jnp.where(keep, pltpu.roll(h, k, 0), 0.0)
            a = a * jnp.where(keep, pltpu.roll(a, k, 0), 1.0)
            k *= 2
        h = h + a * hcarry_ref[SUBLANES - 1:SUBLANES, :]
        hcarry_ref[...] = h[tm - SUBLANES:]
        ccarry_ref[...] = x[tm - SUBLANES:]
        cst_ref[...] = x[tm - SUBLANES:]
        hst_ref[...] = h[tm - SUBLANES:]
    else:
        shp = (tm // SUBLANES, SUBLANES, c)
        h3, a3 = h.reshape(shp), a.reshape(shp)
        pos = lax.broadcasted_iota(jnp.int32, shp, 1)
        k = 1
        while k < SUBLANES:
            keep = pos >= k
            h3 = h3 + a3 * jnp.where(keep, pltpu.roll(h3, k, 1), 0.0)
            a3 = a3 * jnp.where(keep, pltpu.roll(a3, k, 1), 1.0)
            k *= 2
        h = (h3 + a3 * hinit_ref[...].reshape(shp)).reshape(tm, c)
        cst_ref[...] = x
        hst_ref[...] = h
    y_ref[...] = h * jax.nn.gelu(gate_ref[...])


def _rglru(proj, chist, hinit, lp, seq_len, tm):
    n = proj.shape[0]
    long_seq = seq_len > SUBLANES
    tps = max(seq_len // tm, 1)
    c0 = LRU_COL0 // LRU_W
    if long_seq:
        hs = pl.BlockSpec((SUBLANES, LRU_W), lambda i: (i // tps, 0))
    else:
        hs = pl.BlockSpec((tm, LRU_W), lambda i: (i, 0))
    row = pl.BlockSpec((1, LRU_W), lambda i: (0, 0))
    sq = pl.BlockSpec((LRU_W, LRU_W), lambda i: (0, 0))
    return pl.pallas_call(
        functools.partial(_rglru_body, long_seq=long_seq, tps=tps), grid=(n // tm,),
        in_specs=[pl.BlockSpec((tm, LRU_W), lambda i: (i, c0)),
                  pl.BlockSpec((tm, LRU_W), lambda i: (i, c0 + 1)), hs, hs,
                  pl.BlockSpec((4, LRU_W), lambda i: (0, 0)), row, sq, row, sq, row, row],
        out_specs=[pl.BlockSpec((tm, LRU_W), lambda i: (i, 0)), hs, hs],
        out_shape=[jax.ShapeDtypeStruct((n, LRU_W), F32),
                   jax.ShapeDtypeStruct(chist.shape, F32),
                   jax.ShapeDtypeStruct(chist.shape, F32)],
        scratch_shapes=[pltpu.VMEM((SUBLANES, LRU_W), F32), pltpu.VMEM((SUBLANES, LRU_W), F32)],
        compiler_params=_cp(1), name="rglru")(
            proj, proj, chist, hinit, lp["lru_cw"], lp["lru_cb"], lp["lru_wa"], lp["lru_ba"],
            lp["lru_wx"], lp["lru_bx"], lp["lru_lam"])


def _hist(state):
    nseq, k, c = state.shape
    return jnp.pad(state, ((0, 0), (SUBLANES - k, 0), (0, 0))).reshape(nseq * SUBLANES, c)


def _block_diag(w):
    h, a, b = w.shape
    eye = jnp.eye(h, dtype=w.dtype)
    return (eye[:, None, :, None] * w[:, :, None, :]).reshape(h * a, h * b)


def _layer(x3, st, lp, consts, layer, final_nw):
    nseq, seq_len, d = x3.shape
    n = nseq * seq_len
    long_seq = seq_len > SUBLANES
    s_rw, shift_rw, s_hg, h_lru, buf_lru, buf_ffn = st
    x = x3.reshape(n, d)
    ones = consts["ones"]

    proj = _in_proj(x, lp["norm_mix"], consts["w_in"], layer, tm=min(n, 1024), tn=512)

    tm_pre = min(n, seq_len if long_seq else n, 256)
    outs = _rwkv_pre(proj, _hist(shift_rw[:, None, :]), lp, ones, seq_len, tm_pre)
    r, w, k, v, na, kb, g, bonus, shift_out = outs
    new_shift = shift_out.reshape(nseq, SUBLANES, RWKV_COLS)[:, SUBLANES - 1]
    if long_seq:
        c = RWKV_CHUNK
        n4 = _rwkv_n(w, na, kb, consts["lmat2"])
        t4 = _tri_solve(n4.transpose(0, 2, 3, 1)).transpose(0, 3, 1, 2)
        y, spt = _rwkv_chunk((r, w, k, v, na, kb), t4, consts["lmat"], nseq)
        sp = spt.reshape(nseq, RWKV_HEADS // 2, 2, RWKV_N, 2, RWKV_N)
        new_s_rw = jnp.stack([sp[:, :, 0, :, 0, :], sp[:, :, 1, :, 1, :]], axis=2)
        new_s_rw = new_s_rw.reshape(nseq, RWKV_HEADS, RWKV_N, RWKV_N).transpose(0, 1, 3, 2)
    else:
        tmaj = lambda t: t.reshape(nseq, seq_len, RWKV_W).transpose(1, 0, 2)
        s0 = s_rw.transpose(1, 2, 3, 0)
        yb, sf = _rwkv_lanes([tmaj(t) for t in (r, w, k, v, na, kb)], s0)
        y = yb.transpose(1, 0, 2).reshape(n, RWKV_W)
        new_s_rw = sf.transpose(3, 0, 1, 2)
    y_rw = _rwkv_post(y, bonus, g, lp["ln_w"], lp["ln_b"], ones, tm=min(n, 512))

    if long_seq:
        y_hg, new_s_hg = _hgrn_prompt(proj, consts["lb_logits"], lp["hgrn_nw"], consts["cmat"],
                                      nseq, seq_len, layer)
    else:
        proj_t = proj.reshape(nseq, seq_len, IN_COLS).transpose(1, 0, 2)
        s0 = s_hg.transpose(1, 2, 3, 0)
        o_t, sf = _hgrn_lanes(proj_t, consts["lb_logits"], lp["hgrn_nw"], s0, layer)
        y_hg = o_t.transpose(1, 0, 2).reshape(n, HGRN_W)
        new_s_hg = sf.transpose(3, 0, 1, 2)

    if long_seq:
        hinit = _hist(h_lru[:, None, :])
        tm_lru = min(seq_len, 256)
    else:
        hinit = jnp.broadcast_to(h_lru[:, None, :], (nseq, seq_len, LRU_W)).reshape(n, LRU_W)
        tm_lru = min(n, 512)
    y_lru, cst, hst = _rglru(proj, _hist(buf_lru), hinit, lp, seq_len, tm_lru)
    new_buf_lru = cst.reshape(nseq, SUBLANES, LRU_W)[:, SUBLANES - 3:]
    new_h_lru = hst.reshape(nseq, SUBLANES, LRU_W)[:, SUBLANES - 1]

    x = _out_proj(x, y_rw, y_hg, y_lru, consts["w_out"], layer, tm=min(n, 1024), tn=512)
    tm_ffn = min(n, seq_len if long_seq else n, 1024)
    outs = _ffn(x, lp["norm_ffn"], consts["w_up"], lp["ffn_cw"], lp["ffn_cb"], consts["w_down"],
                _hist(buf_ffn), layer, seq_len, tm=tm_ffn, tf=512, final_nw=final_nw)
    x, fst = (outs[1], outs[0]) if final_nw is not None else outs
    new_buf_ffn = fst.reshape(nseq, -1, SUBLANES, D_FF)[:, -1, SUBLANES - 2:]
    return x.reshape(nseq, seq_len, d), (new_s_rw, new_shift, new_s_hg, new_h_lru, new_buf_lru,
                                         new_buf_ffn)


def kernel(x_prompt, x_sample, state_rwkv, state_rwkv_shift, state_hgrn, state_rglru, cache_rglru_conv, cache_ffn_conv, norm_mix, w_in, rwkv_mu, rwkv_w0, rwkv_w2, rwkv_a0, rwkv_a2, rwkv_g2, rwkv_k_k, rwkv_k_a, rwkv_r_k, rwkv_ln_w, rwkv_ln_b, hgrn_lb_logits, hgrn_norm_w, rglru_conv_w, rglru_conv_b, rglru_wa, rglru_ba, rglru_wx, rglru_bx, rglru_lambda, w_out, norm_ffn, ffn_w_up, ffn_conv_w, ffn_conv_b, ffn_w_down, norm_final):
    depth = w_in.shape[0]
    nb, nt, _ = x_prompt.shape
    db = x_sample.shape[0]
    consts = {
        "ones": jnp.asarray(np.kron(np.eye(RWKV_HEADS), np.ones((RWKV_N, RWKV_N))), dtype=BF16),
        "cmat": _hgrn_level_matrix(),
        "lmat": jnp.asarray(np.tril(np.ones((RWKV_CHUNK, RWKV_CHUNK), np.float32)), dtype=BF16),
        "lmat2": jnp.asarray(np.kron(np.eye(RWKV_N_CHUNKS), np.tril(np.ones((RWKV_CHUNK, RWKV_CHUNK)))),
                             dtype=BF16),
        "lb_logits": hgrn_lb_logits,
        "w_in": w_in.astype(BF16), "w_out": w_out.astype(BF16),
        "w_up": ffn_w_up.astype(BF16), "w_down": ffn_w_down.astype(BF16),
    }
    rowv = lambda a: a.reshape(1, -1)
    x_p, x_s = x_prompt, x_sample
    new_p, new_s = [], []
    for l in range(depth):
        lp = {
            "norm_mix": rowv(norm_mix[l]), "mu": rowv(rwkv_mu[l]),
            "w0": rowv(rwkv_w0[l]),
            "w2p": jnp.pad(rwkv_w2[l], ((0, 64), (0, 0))).astype(BF16),
            "a0": rowv(rwkv_a0[l]),
            "a2p": jnp.pad(rwkv_a2[l], ((64, 0), (0, 0))).astype(BF16),
            "g2": rwkv_g2[l].astype(BF16), "k_k": rowv(rwkv_k_k[l]), "k_a": rowv(rwkv_k_a[l]),
            "r_k": rowv(rwkv_r_k[l]), "ln_w": rowv(rwkv_ln_w[l]), "ln_b": rowv(rwkv_ln_b[l]),
            "hgrn_nw": rowv(hgrn_norm_w[l]),
            "lru_cw": rglru_conv_w[l], "lru_cb": rowv(rglru_conv_b[l]),
            "lru_wa": _block_diag(rglru_wa[l]).astype(BF16), "lru_ba": rowv(rglru_ba[l]),
            "lru_wx": _block_diag(rglru_wx[l]).astype(BF16), "lru_bx": rowv(rglru_bx[l]),
            "lru_lam": rowv(rglru_lambda[l]),
            "norm_ffn": rowv(norm_ffn[l]), "ffn_cw": ffn_conv_w[l],
            "ffn_cb": rowv(ffn_conv_b[l]),
        }
        zero = lambda *s: jnp.zeros(s, F32)
        st_p = (None, zero(nb, RWKV_COLS), None, zero(nb, LRU_W), zero(nb, 3, LRU_W),
                zero(nb, 2, D_FF))
        final_nw = rowv(norm_final) if l == depth - 1 else None
        x_p, sp = _layer(x_p, st_p, lp, consts, l, final_nw)
        st_s = (state_rwkv[l], state_rwkv_shift[l], state_hgrn[l], state_rglru[l],
                cache_rglru_conv[l], cache_ffn_conv[l])
        x_s, ss = _layer(x_s, st_s, lp, consts, l, final_nw)
        new_p.append(sp)
        new_s.append(ss)
    stack = lambda sts: [jnp.stack(s, axis=0) for s in zip(*sts)]
    return (x_p, x_s, *stack(new_p), *stack(new_s))
```

```python
import functools

import numpy as np
import jax
import jax.numpy as jnp
from jax import lax
from jax.experimental import pallas as pl
from jax.experimental.pallas import tpu as pltpu

F32 = jnp.float32
BF16 = jnp.bfloat16

D_MODEL = 2048
RWKV_HEADS = 12
RWKV_N = 64
RWKV_W = RWKV_HEADS * RWKV_N
RWKV_COLS = 2560
RWKV_LN_EPS = 64e-5
HGRN_HEADS = 6
HGRN_D = 128
HGRN_W = HGRN_HEADS * HGRN_D
HGRN_COL0 = RWKV_COLS
LRU_W = 512
LRU_COL0 = RWKV_COLS + 4 * HGRN_W
LRU_C = 8.0
IN_COLS = 6656
D_FF = 5632
NORM_EPS = 1e-6

SUBLANES = 8
LANES = 128
RWKV_CHUNK = 64
HGRN_CHUNK = 128
HGRN_LEVELS = 7
VMEM_LIMIT = 48 * 1024 * 1024
FFN_VMEM_LIMIT = 56 * 1024 * 1024


def _cp(n):
    return pltpu.CompilerParams(dimension_semantics=("arbitrary",) * n,
                                vmem_limit_bytes=VMEM_LIMIT)


def _dot(a, b):
    return jnp.dot(a, b, preferred_element_type=F32)


def _dot_nt(a, b):
    return lax.dot_general(a, b, (((1,), (1,)), ((), ())), preferred_element_type=F32)


def _split(x):
    hi = x.astype(BF16)
    lo = (x - hi.astype(F32)).astype(BF16)
    return hi, lo


def _split_dot(x, m):
    hi, lo = _split(x)
    return _dot(hi, m) + _dot(lo, m)


def _dot3(a, b, nt=False):
    f = _dot_nt if nt else _dot
    ah, al = _split(a)
    bh, bl = _split(b)
    return f(ah, bh) + (f(ah, bl) + f(al, bh))


def _softplus(z):
    return jnp.maximum(z, 0.0) + jnp.log1p(jnp.exp(-jnp.abs(z)))


def _expm1(z):
    return jnp.tanh(0.5 * z) * (jnp.exp(z) + 1.0)


def _prev_rows(x, hist, ks, long_seq):
    tm, c = x.shape
    if long_seq:
        ext = jnp.concatenate([hist, x], axis=0)
        return [pltpu.roll(ext, k, 0)[SUBLANES:] for k in ks]
    x3 = x.reshape(tm // SUBLANES, SUBLANES, c)
    h3 = hist.reshape(tm // SUBLANES, SUBLANES, c)
    pos = lax.broadcasted_iota(jnp.int32, x3.shape, 1)
    return [jnp.where(pos >= k, pltpu.roll(x3, k, 1), pltpu.roll(h3, k, 1)).reshape(tm, c)
            for k in ks]


def _in_proj_body(x_ref, nw_ref, w_ref, o_ref, xn_ref):
    @pl.when(pl.program_id(1) == 0)
    def _():
        x = x_ref[...]
        ms = jnp.mean(x * x, axis=-1, keepdims=True)
        xn_ref[...] = (x * lax.rsqrt(ms + NORM_EPS) * nw_ref[...]).astype(BF16)
    o_ref[...] = _dot(xn_ref[...], w_ref[...])


def _in_proj(x, nw, w, layer, tm, tn):
    n, d = x.shape
    c = w.shape[2]
    return pl.pallas_call(
        _in_proj_body, grid=(n // tm, c // tn),
        in_specs=[pl.BlockSpec((tm, d), lambda i, j: (i, 0)),
                  pl.BlockSpec((1, d), lambda i, j: (0, 0)),
                  pl.BlockSpec((None, d, tn), lambda i, j: (layer, 0, j))],
        out_specs=pl.BlockSpec((tm, tn), lambda i, j: (i, j)),
        out_shape=jax.ShapeDtypeStruct((n, c), F32),
        scratch_shapes=[pltpu.VMEM((tm, d), BF16)],
        compiler_params=_cp(2), name="in_proj")(x, nw, w)


def _out_proj_body(x_ref, ya_ref, yb_ref, yc_ref, w_ref, o_ref, y_ref):
    @pl.when(pl.program_id(1) == 0)
    def _():
        y_ref[:, 0:RWKV_W] = ya_ref[...].astype(BF16)
        y_ref[:, RWKV_W:RWKV_W + HGRN_W] = yb_ref[...].astype(BF16)
        y_ref[:, RWKV_W + HGRN_W:] = yc_ref[...].astype(BF16)
    o_ref[...] = x_ref[...] + _dot(y_ref[...], w_ref[...])


def _out_proj(x, ya, yb, yc, w, layer, tm, tn):
    n, d = x.shape
    return pl.pallas_call(
        _out_proj_body, grid=(n // tm, d // tn),
        in_specs=[pl.BlockSpec((tm, tn), lambda i, j: (i, j)),
                  pl.BlockSpec((tm, RWKV_W), lambda i, j: (i, 0)),
                  pl.BlockSpec((tm, HGRN_W), lambda i, j: (i, 0)),
                  pl.BlockSpec((tm, LRU_W), lambda i, j: (i, 0)),
                  pl.BlockSpec((None, d, tn), lambda i, j: (layer, 0, j))],
        out_specs=pl.BlockSpec((tm, tn), lambda i, j: (i, j)),
        out_shape=jax.ShapeDtypeStruct((n, d), F32),
        scratch_shapes=[pltpu.VMEM((tm, d), BF16)],
        compiler_params=_cp(2), name="out_proj")(x, ya, yb, yc, w)


def _ffn_body(*refs, long_seq, tps, final):
    x_ref, nw_ref, wg_ref, wv_ref, cw_ref, cb_ref, wd_ref, hist_ref = refs[:8]
    if final:
        fnw_ref, o_ref, st_ref, hn_ref, carry_ref = refs[8:]
    else:
        o_ref, st_ref, hn_ref, carry_ref = refs[8:]
    i = pl.program_id(0)
    j = pl.program_id(1)

    @pl.when(j == 0)
    def _():
        x = x_ref[...]
        ms = jnp.mean(x * x, axis=-1, keepdims=True)
        hn_ref[...] = (x * lax.rsqrt(ms + NORM_EPS) * nw_ref[...]).astype(BF16)
        o_ref[...] = x

    hn = hn_ref[...]
    g = _dot(hn, wg_ref[...])
    v = _dot(hn, wv_ref[...])
    if long_seq:
        @pl.when(i % tps == 0)
        def _():
            carry_ref[j] = hist_ref[...]
        hist = carry_ref[j]
    else:
        hist = hist_ref[...]
    p1, p2 = _prev_rows(g, hist, (1, 2), long_seq)
    cw = cw_ref[...]
    gc = cb_ref[...] + cw[0:1] * p2 + cw[1:2] * p1 + cw[2:3] * g
    h = (gc * jax.nn.sigmoid(gc)) * v
    o_ref[...] += _dot(h.astype(BF16), wd_ref[...])
    if long_seq:
        carry_ref[j] = g[g.shape[0] - SUBLANES:]
        st_ref[...] = g[g.shape[0] - SUBLANES:]
    else:
        st_ref[...] = g

    if final:
        @pl.when(j == pl.num_programs(1) - 1)
        def _():
            y = o_ref[...]
            ms = jnp.mean(y * y, axis=-1, keepdims=True)
            o_ref[...] = y * lax.rsqrt(ms + NORM_EPS) * fnw_ref[...]


def _ffn(x, nw, w_up, cw, cb, w_down, hist, layer, seq_len, tm, tf, final_nw=None):
    n, d = x.shape
    final = final_nw is not None
    nf = D_FF // tf
    long_seq = seq_len > SUBLANES
    tps = max(seq_len // tm, 1)
    if long_seq:
        hist_spec = pl.BlockSpec((SUBLANES, tf), lambda i, j: (i // tps, j))
        st_spec = pl.BlockSpec((SUBLANES, tf), lambda i, j: (i, j))
        st_rows = n // tm * SUBLANES
    else:
        hist_spec = pl.BlockSpec((tm, tf), lambda i, j: (i, j))
        st_spec = pl.BlockSpec((tm, tf), lambda i, j: (i, j))
        st_rows = n
    body = functools.partial(_ffn_body, long_seq=long_seq, tps=tps, final=final)
    row_tile = pl.BlockSpec((tm, d), lambda i, j: (i, 0))
    wide = pl.BlockSpec((1, d), lambda i, j: (0, 0))
    in_specs = [row_tile, wide,
                pl.BlockSpec((None, d, tf), lambda i, j: (layer, 0, j)),
                pl.BlockSpec((None, d, tf), lambda i, j: (layer, 0, j + nf)),
                pl.BlockSpec((3, tf), lambda i, j: (0, j)),
                pl.BlockSpec((1, tf), lambda i, j: (0, j)),
                pl.BlockSpec((None, tf, d), lambda i, j: (layer, j, 0)),
                hist_spec]
    args = [x, nw, w_up, w_up, cw, cb, w_down, hist]
    out_specs = [row_tile, st_spec]
    out_shape = [jax.ShapeDtypeStruct((n, d), F32), jax.ShapeDtypeStruct((st_rows, D_FF), F32)]
    scratch = [pltpu.VMEM((tm, d), BF16), pltpu.VMEM((nf, SUBLANES, tf), F32)]
    if final:
        in_specs.append(wide)
        args.append(final_nw)
    return pl.pallas_call(
        body, grid=(n // tm, nf), in_specs=in_specs, out_specs=out_specs, out_shape=out_shape,
        scratch_shapes=scratch,
        compiler_params=pltpu.CompilerParams(dimension_semantics=("arbitrary",) * 2,
                                             vmem_limit_bytes=FFN_VMEM_LIMIT),
        name="ffn")(*args)


def _rwkv_pre_body(p_ref, hist_ref, mu_ref, w0_ref, w2_ref, a0_ref, a2_ref, g2_ref, kk_ref,
                   ka_ref, rk_ref, ones_ref,
                   r_ref, w_ref, k_ref, v_ref, na_ref, kb_ref, g_ref, bonus_ref, st_ref,
                   carry_ref, *, long_seq, tps):
    p = p_ref[...]
    if long_seq:
        @pl.when(pl.program_id(0) % tps == 0)
        def _():
            carry_ref[...] = hist_ref[...]
        hist = carry_ref[...]
    else:
        hist = hist_ref[...]
    (prev,) = _prev_rows(p, hist, (1,), long_seq)
    if long_seq:
        carry_ref[...] = p[p.shape[0] - SUBLANES:]
        st_ref[...] = p[p.shape[0] - SUBLANES:]
    else:
        st_ref[...] = p
    xs = p + (prev - p) * mu_ref[...]
    r = xs[:, 0:RWKV_W]
    k = xs[:, RWKV_W:2 * RWKV_W]
    v = xs[:, 2 * RWKV_W:3 * RWKV_W]
    xwa = xs[:, 3 * RWKV_W:3 * RWKV_W + 128]
    xg = xs[:, 3 * RWKV_W + 128:RWKV_COLS]
    zw = w0_ref[...] + _dot(jnp.tanh(xwa).astype(BF16), w2_ref[...])
    w_log = -_softplus(-zw) - 0.5
    decay = jnp.exp(-jnp.exp(w_log))
    a = jax.nn.sigmoid(a0_ref[...] + _dot(xwa.astype(BF16), a2_ref[...]))
    g = _dot(jax.nn.sigmoid(xg).astype(BF16), g2_ref[...])
    ones = ones_ref[...]
    kk = k * kk_ref[...]
    kk = kk / jnp.maximum(jnp.sqrt(_split_dot(kk * kk, ones)), 1e-12)
    kf = k * (1.0 + (a - 1.0) * ka_ref[...])
    r_ref[...] = r
    w_ref[...] = decay
    k_ref[...] = kf
    v_ref[...] = v
    na_ref[...] = -kk
    kb_ref[...] = kk * a
    g_ref[...] = g
    bonus_ref[...] = _split_dot(r * kf * rk_ref[...], ones) * v


def _rwkv_pre(proj, hist, lp, ones, seq_len, tm):
    n = proj.shape[0]
    long_seq = seq_len > SUBLANES
    tps = max(seq_len // tm, 1)
    row = lambda c: pl.BlockSpec((1, c), lambda i: (0, 0))
    full = lambda a, b: pl.BlockSpec((a, b), lambda i: (0, 0))
    if long_seq:
        hist_spec = pl.BlockSpec((SUBLANES, RWKV_COLS), lambda i: (i // tps, 0))
    else:
        hist_spec = pl.BlockSpec((tm, RWKV_COLS), lambda i: (i, 0))
    tile = pl.BlockSpec((tm, RWKV_W), lambda i: (i, 0))
    body = functools.partial(_rwkv_pre_body, long_seq=long_seq, tps=tps)
    outs = pl.pallas_call(
        body, grid=(n // tm,),
        in_specs=[pl.BlockSpec((tm, RWKV_COLS), lambda i: (i, 0)), hist_spec,
                  row(RWKV_COLS), row(RWKV_W), full(128, RWKV_W), row(RWKV_W),
                  full(128, RWKV_W), full(128, RWKV_W), row(RWKV_W), row(RWKV_W), row(RWKV_W),
                  full(RWKV_W, RWKV_W)],
        out_specs=[tile] * 8 + [hist_spec],
        out_shape=[jax.ShapeDtypeStruct((n, RWKV_W), F32)] * 8
                  + [jax.ShapeDtypeStruct(hist.shape, F32)],
        scratch_shapes=[pltpu.VMEM((SUBLANES, RWKV_COLS), F32)],
        compiler_params=_cp(1), name="rwkv_pre")(
            proj, hist, lp["mu"], lp["w0"], lp["w2p"], lp["a0"], lp["a2p"], lp["g2"],
            lp["k_k"], lp["k_a"], lp["r_k"], ones)
    return outs


def _rwkv_post_body(y_ref, bonus_ref, g_ref, lw_ref, lb_ref, ones_ref, o_ref):
    y = y_ref[...]
    ones = ones_ref[...]
    mean = _split_dot(y, ones) * (1.0 / RWKV_N)
    d = y - mean
    var = _split_dot(d * d, ones) * (1.0 / RWKV_N)
    yn = d * lax.rsqrt(var + RWKV_LN_EPS)
    o_ref[...] = (yn * lw_ref[...] + lb_ref[...] + bonus_ref[...]) * g_ref[...]


def _rwkv_post(y, bonus, g, lw, lb, ones, tm):
    n = y.shape[0]
    tile = pl.BlockSpec((tm, RWKV_W), lambda i: (i, 0))
    row = pl.BlockSpec((1, RWKV_W), lambda i: (0, 0))
    return pl.pallas_call(
        _rwkv_post_body, grid=(n // tm,),
        in_specs=[tile, tile, tile, row, row, pl.BlockSpec((RWKV_W, RWKV_W), lambda i: (0, 0))],
        out_specs=tile, out_shape=jax.ShapeDtypeStruct((n, RWKV_W), F32),
        compiler_params=_cp(1), name="rwkv_post")(y, bonus, g, lw, lb, ones)


def _rwkv_lanes_body(r_ref, w_ref, k_ref, v_ref, a_ref, b_ref, s0_ref, y_ref, sf_ref,
                     s_ref, xt_ref, y_buf):
    tb = pl.program_id(1)
    ninst = r_ref.shape[1]

    @pl.when(tb == 0)
    def _():
        for hh in range(2):
            s_ref[hh] = s0_ref[hh]

    def step(tl, carry):
        for idx, ref in enumerate((r_ref, w_ref, k_ref, v_ref, a_ref, b_ref)):
            xt_ref[idx] = ref[tl].T
        for hh in range(2):
            base = hh * RWKV_N
            heads = lambda i: xt_ref[i, base:base + RWKV_N, :]

            def row(rho, c):
                s = s_ref[hh, rho]
                sa = jnp.sum(s * heads(4), axis=0, keepdims=True)
                vrow = xt_ref[3, pl.ds(base + rho, 1), :]
                s = s * heads(1) + sa * heads(5) + vrow * heads(2)
                s_ref[hh, rho] = s
                y_buf[pl.ds(base + rho, 1), :] = jnp.sum(s * heads(0), axis=0, keepdims=True)
                return c

            lax.fori_loop(0, RWKV_N, row, 0, unroll=8)
        y_ref[tl] = y_buf[...].T
        return carry

    lax.fori_loop(0, SUBLANES, step, 0)

    @pl.when(tb == pl.num_programs(1) - 1)
    def _():
        for hh in range(2):
            sf_ref[hh] = s_ref[hh]


def _rwkv_lanes(xs, s0):
    steps, ninst, _ = xs[0].shape
    xspec = pl.BlockSpec((SUBLANES, ninst, 2 * RWKV_N), lambda h, t: (t, 0, h))
    sspec = pl.BlockSpec((2, RWKV_N, RWKV_N, ninst), lambda h, t: (h, 0, 0, 0))
    return pl.pallas_call(
        _rwkv_lanes_body, grid=(RWKV_HEADS // 2, steps // SUBLANES),
        in_specs=[xspec] * 6 + [sspec], out_specs=[xspec, sspec],
        out_shape=[jax.ShapeDtypeStruct((steps, ninst, RWKV_W), F32),
                   jax.ShapeDtypeStruct(s0.shape, F32)],
        scratch_shapes=[pltpu.VMEM((2, RWKV_N, RWKV_N, ninst), F32),
                        pltpu.VMEM((6, 2 * RWKV_N, ninst), F32),
                        pltpu.VMEM((2 * RWKV_N, ninst), F32)],
        compiler_params=_cp(2), name="rwkv_lanes")(*xs, s0)


RWKV_N_CHUNKS = 2
RWKV_C_CHUNKS = 2


def _rwkv_n_body(w_ref, a_ref, b_ref, l_ref, n_ref):
    c = RWKV_CHUNK
    rows = RWKV_N_CHUNKS * c
    lmat = l_ref[...]
    strict = lax.broadcasted_iota(jnp.int32, (c, c), 1) < lax.broadcasted_iota(jnp.int32, (c, c), 0)
    lane = lax.broadcasted_iota(jnp.int32, (rows, 2 * RWKV_N), 1)
    pairs = range(RWKV_HEADS // 2)
    sls = [slice(p * 2 * RWKV_N, (p + 1) * 2 * RWKV_N) for p in pairs]
    lws = [jnp.log(w_ref[:, sl]) for sl in sls]
    cws = [_split_dot_left(lmat, lw) for lw in lws]
    ats = [_split(a_ref[:, sl] * jnp.exp(cw - lw)) for sl, cw, lw in zip(sls, cws, lws)]
    bts = [_split(b_ref[:, sl] * jnp.exp(-cw)) for sl, cw in zip(sls, cws)]
    for p in pairs:
        for hh in range(2):
            m = (lane < RWKV_N) if hh == 0 else (lane >= RWKV_N)
            ah = jnp.where(m, ats[p][0], jnp.zeros_like(ats[p][0]))
            al = jnp.where(m, ats[p][1], jnp.zeros_like(ats[p][1]))
            for ci in range(RWKV_N_CHUNKS):
                rs = slice(ci * c, (ci + 1) * c)
                bh, bl = bts[p][0][rs], bts[p][1][rs]
                n = _dot_nt(ah[rs], bh) + (_dot_nt(ah[rs], bl) + _dot_nt(al[rs], bh))
                n_ref[2 * p + hh, ci] = jnp.where(strict, n, 0.0)


def _rwkv_n(w, a, b, lmat2):
    n = w.shape[0]
    c = RWKV_CHUNK
    rows = RWKV_N_CHUNKS * c
    tile = pl.BlockSpec((rows, RWKV_W), lambda i: (i, 0))
    return pl.pallas_call(
        _rwkv_n_body, grid=(n // rows,),
        in_specs=[tile, tile, tile, pl.BlockSpec((rows, rows), lambda i: (0, 0))],
        out_specs=pl.BlockSpec((RWKV_HEADS, RWKV_N_CHUNKS, c, c), lambda i: (0, i, 0, 0)),
        out_shape=jax.ShapeDtypeStruct((RWKV_HEADS, n // c, c, c), F32),
        compiler_params=_cp(1), name="rwkv_n")(w, a, b, lmat2)


def _tri_solve_body(nt_ref, tt_ref):
    c = RWKV_CHUNK
    ninst = nt_ref.shape[2]
    tt_ref[...] = jnp.zeros_like(tt_ref)
    g = SUBLANES
    nb = 4
    sub = lax.broadcasted_iota(jnp.int32, (g, ninst), 0)
    for t0 in range(0, c, nb):
        ngrp = (t0 + nb - 1) // g + 1
        rs = [[jnp.where(sub + g * j == t0 + i, 1.0, 0.0).astype(F32) for j in range(ngrp)]
              for i in range(nb)]
        for sg in range(-(-t0 // g)):

            def acc(s, carry, sg=sg, t0=t0):
                ts = [tt_ref[s, j * g:(j + 1) * g, :] for j in range(sg + 1)]
                nrow = [nt_ref[t0 + i, pl.ds(s, 1), :] for i in range(nb)]
                return tuple(tuple(carry[i][j] + nrow[i] * ts[j] for j in range(sg + 1))
                             for i in range(nb))

            res = lax.fori_loop(sg * g, min(sg * g + g, t0), acc,
                                tuple(tuple(rs[i][:sg + 1]) for i in range(nb)))
            for i in range(nb):
                rs[i][:sg + 1] = list(res[i])
        for i in range(nb):
            for j in range(i):
                nij = nt_ref[t0 + i, t0 + j:t0 + j + 1, :]
                rs[i] = [x + nij * y for x, y in zip(rs[i], rs[j])]
            tt_ref[t0 + i, 0:ngrp * g, :] = jnp.concatenate(rs[i], axis=0)


def _tri_solve(nt):
    nh, c, _, ninst = nt.shape
    spec = pl.BlockSpec((None, c, c, ninst), lambda h: (h, 0, 0, 0))
    return pl.pallas_call(
        _tri_solve_body, grid=(nh,), in_specs=[spec], out_specs=spec,
        out_shape=jax.ShapeDtypeStruct(nt.shape, F32),
        compiler_params=_cp(1), name="tri_solve")(nt)


def _rwkv_chunk_body(r_ref, w_ref, k_ref, v_ref, a_ref, b_ref, t_ref, l_ref, y_ref, sf_ref, s_ref):
    c = RWKV_CHUNK
    pw = 2 * RWKV_N
    ci = pl.program_id(1)

    @pl.when(ci == 0)
    def _():
        s_ref[...] = jnp.zeros_like(s_ref)

    lmat = l_ref[...]
    ti = lax.broadcasted_iota(jnp.int32, (c, c), 0)
    si = lax.broadcasted_iota(jnp.int32, (c, c), 1)
    strict, incl = si < ti, si <= ti
    lane = lax.broadcasted_iota(jnp.int32, (c, pw), 1)
    lane2 = lax.broadcasted_iota(jnp.int32, (2 * c, pw), 1)
    r2 = lax.broadcasted_iota(jnp.int32, (pw, pw), 0)
    c2 = lax.broadcasted_iota(jnp.int32, (pw, pw), 1)
    same_head = (r2 < RWKV_N) == (c2 < RWKV_N)
    eye = r2 == c2
    bz = lambda x: jnp.zeros_like(x)
    pairs = range(RWKV_HEADS // 2)
    units = [(q, p) for q in range(RWKV_C_CHUNKS) for p in pairs]
    at = lambda ref, u: ref[u[0] * c:(u[0] + 1) * c, u[1] * pw:(u[1] + 1) * pw]
    halves = [(u, hh) for u in units for hh in range(2)]
    lws = {u: jnp.log(at(w_ref, u)) for u in units}
    cws = {u: _split_dot_left(lmat, lws[u]) for u in units}
    vs = {u: at(v_ref, u) for u in units}
    wts = {u: jnp.exp(cws[u]) for u in units}
    rts = {u: at(r_ref, u) * wts[u] for u in units}
    ats = {u: at(a_ref, u) * jnp.exp(cws[u] - lws[u]) for u in units}
    lhs = {u: _split(jnp.concatenate([ats[u], rts[u]], axis=0)) for u in units}
    bts = {u: _split(at(b_ref, u) * jnp.exp(-cws[u])) for u in units}
    kts = {u: _split(at(k_ref, u) * jnp.exp(-cws[u])) for u in units}
    a_ak, a_rk, a_rb = {}, {}, {}
    for u, hh in halves:
        m2 = (lane2 < RWKV_N) if hh == 0 else (lane2 >= RWKV_N)
        lh = jnp.where(m2, lhs[u][0], bz(lhs[u][0]))
        ll = jnp.where(m2, lhs[u][1], bz(lhs[u][1]))
        pak = (_dot_nt(lh[:c], kts[u][0])
               + (_dot_nt(lh[:c], kts[u][1]) + _dot_nt(ll[:c], kts[u][0])))
        a_ak[u, hh] = jnp.where(strict, pak, 0.0)
        a_rk[u, hh] = jnp.where(incl, _dot_nt(lh[c:], kts[u][0]), 0.0).astype(BF16)
        a_rb[u, hh] = jnp.where(incl, _dot_nt(lh[c:], bts[u][0]), 0.0).astype(BF16)
    vps = {uh: _dot3(a_ak[uh], vs[uh[0]]) for uh in halves}
    gs = {(u, hh): _dot3(t_ref[2 * u[1] + hh, u[0]], jnp.concatenate([ats[u], vps[u, hh]], axis=1))
          for u, hh in halves}
    pick = lambda x0, x1: jnp.where(lane < RWKV_N, x0, x1)
    ahats = {u: pick(gs[u, 0][:, :pw], gs[u, 1][:, :pw]) for u in units}
    vhats = {u: pick(gs[u, 0][:, pw:], gs[u, 1][:, pw:]) for u in units}
    ahb = {u: ahats[u].astype(BF16) for u in units}
    vhb = {u: vhats[u].astype(BF16) for u in units}
    vbs = {u: vs[u].astype(BF16) for u in units}
    rhats = {u: (rts[u] + pick(_dot(a_rb[u, 0], ahb[u]), _dot(a_rb[u, 1], ahb[u]))).astype(BF16)
             for u in units}
    yhats = {u: pick(_dot(a_rb[u, 0], vhb[u]) + _dot(a_rk[u, 0], vbs[u]),
                     _dot(a_rb[u, 1], vhb[u]) + _dot(a_rk[u, 1], vbs[u])) for u in units}
    wends = {u: jnp.exp(cws[u][c - 1:c] - cws[u]) for u in units}
    bkts = {u: jnp.concatenate([at(b_ref, u) * wends[u], at(k_ref, u) * wends[u]], axis=0).T
            for u in units}
    mpts = {u: jnp.where(same_head, _dot3(bkts[u], jnp.concatenate([ahats[u], bz(ahats[u])], axis=0)), 0.0)
            + jnp.where(eye, wts[u][c - 1:c], 0.0) for u in units}
    zpts = {u: jnp.where(same_head, _dot3(bkts[u], jnp.concatenate([vhats[u], vs[u]], axis=0)), 0.0)
            for u in units}
    spts = [s_ref[p] for p in pairs]
    for q in range(RWKV_C_CHUNKS):
        for p in pairs:
            y_ref[q * c:(q + 1) * c, p * pw:(p + 1) * pw] = (
                _dot(rhats[q, p], spts[p].astype(BF16)) + yhats[q, p])
        spts = [_dot3(mpts[q, p], spts[p]) + zpts[q, p] for p in pairs]
    for p in pairs:
        s_ref[p] = spts[p]

    @pl.when(ci == pl.num_programs(1) - 1)
    def _():
        sf_ref[...] = s_ref[...]


def _rwkv_chunk(xs, t4, lmat, nseq):
    n = xs[0].shape[0]
    c = RWKV_CHUNK
    q = RWKV_C_CHUNKS
    nch = n // (q * c) // nseq
    pw = 2 * RWKV_N
    tile = pl.BlockSpec((q * c, RWKV_W), lambda b, i: (b * nch + i, 0))
    return pl.pallas_call(
        _rwkv_chunk_body, grid=(nseq, nch),
        in_specs=[tile] * 6 + [pl.BlockSpec((RWKV_HEADS, q, c, c), lambda b, i: (0, b * nch + i, 0, 0)),
                               pl.BlockSpec((c, c), lambda b, i: (0, 0))],
        out_specs=[tile, pl.BlockSpec((None, RWKV_HEADS // 2, pw, pw), lambda b, i: (b, 0, 0, 0))],
        out_shape=[jax.ShapeDtypeStruct((n, RWKV_W), F32),
                   jax.ShapeDtypeStruct((nseq, RWKV_HEADS // 2, pw, pw), F32)],
        scratch_shapes=[pltpu.VMEM((RWKV_HEADS // 2, pw, pw), F32)],
        compiler_params=_cp(2), name="rwkv_chunk")(*xs, t4, lmat)


def _hgrn_lower_bound(logits, layer):
    m = jnp.max(logits, axis=0, keepdims=True)
    e = jnp.exp(logits - m)
    gam = e / jnp.sum(e, axis=0, keepdims=True)
    cs = gam[0:1]
    for i in range(1, layer + 1):
        cs = cs + gam[i:i + 1]
    return cs - gam[0:1]


def _hgrn_level_matrix():
    t = np.arange(HGRN_CHUNK)[:, None]
    s = np.arange(HGRN_CHUNK)[None, :]
    mats = []
    for upper in (False, True):
        for e in (1, 2):
            same = (t >> e) == (s >> e)
            mats.append(same & ((s > t) if upper else (s <= t)))
    mats.append(s <= t)
    return jnp.asarray(np.concatenate(mats, axis=0).astype(np.float32), dtype=BF16)


def _hgrn_prompt_body(*refs, layer):
    npair = HGRN_HEADS // 2
    q_refs, f_refs, i_refs, g_refs = (refs[i * npair:(i + 1) * npair] for i in range(4))
    lg_ref, nw_ref, cm_ref, o_ref, sf_ref, s_ref = refs[4 * npair:]
    c = pl.program_id(1)

    @pl.when(c == 0)
    def _():
        s_ref[...] = jnp.zeros_like(s_ref)

    n = HGRN_CHUNK
    heads = range(HGRN_HEADS)
    col = lambda rs, h: rs[h // 2][:, (h % 2) * HGRN_D:(h % 2 + 1) * HGRN_D]
    hsl = lambda h: slice(h * HGRN_D, (h + 1) * HGRN_D)
    t = lax.broadcasted_iota(jnp.int32, (n, n), 0)
    s_ = lax.broadcasted_iota(jnp.int32, (n, n), 1)
    lb_all = _hgrn_lower_bound(lg_ref[...], layer)
    cm = cm_ref[...]
    ones = jnp.ones((n, n), BF16)
    fgs = [lb_all[:, hsl(h)] + (1.0 - lb_all[:, hsl(h)]) * jax.nn.sigmoid(col(f_refs, h))
           for h in heads]
    lfs = [jnp.log(fg) for fg in fgs]
    css = [_split_dot_left(cm, lf) for lf in lfs]
    bs = [cs[4 * n:5 * n] for cs in css]

    def cl(h, e):
        if e <= 2:
            return css[h][(e - 1) * n:e * n]
        if e == HGRN_LEVELS:
            return bs[h]
        m = 1 << e
        b3 = bs[h].reshape(n // m, m, HGRN_D)
        ends = jnp.broadcast_to(b3[:, m - 1:m, :], b3.shape)
        before = jnp.concatenate([jnp.zeros_like(ends[:1]), ends[:-1]], axis=0)
        return (b3 - before).reshape(n, HGRN_D)

    def cu(h, e):
        if e <= 2:
            return css[h][(2 + e - 1) * n:(2 + e) * n]
        if e == HGRN_LEVELS:
            return bs[h][n - 1:n] - bs[h]
        m = 1 << e
        b3 = bs[h].reshape(n // m, m, HGRN_D)
        return (jnp.broadcast_to(b3[:, m - 1:m, :], b3.shape) - b3).reshape(n, HGRN_D)
    qs = [col(q_refs, h) * jax.nn.sigmoid(col(q_refs, h)) for h in heads]
    ks = [1.0 - fg for fg in fgs]
    vbs = [col(i_refs, h).astype(BF16) for h in heads]
    acc = [jnp.where(t == s_, _dot_nt(qs[h].astype(BF16), ks[h].astype(BF16)), 0.0) for h in heads]
    for e in range(HGRN_LEVELS):
        mask = (((t >> e) & 1) == 1) & ((s_ >> e) == ((t >> e) - 1))
        for h in heads:
            qe = qs[h] * (fgs[h] if e == 0 else jnp.exp(cl(h, e)))
            ke = ks[h] if e == 0 else ks[h] * jnp.exp(cu(h, e))
            acc[h] = acc[h] + jnp.where(mask, _dot_nt(qe.astype(BF16), ke.astype(BF16)), 0.0)
    sts = [s_ref[h] for h in heads]
    os_ = [_dot(acc[h].astype(BF16), vbs[h])
           + _dot((qs[h] * jnp.exp(cl(h, HGRN_LEVELS))).astype(BF16), sts[h].astype(BF16))
           for h in heads]
    dks = [jnp.exp(_split_dot(lfs[h].T, ones)) for h in heads]
    for h in heads:
        kf = ks[h] * jnp.exp(cu(h, HGRN_LEVELS))
        s_ref[h] = dks[h] * sts[h] + _dot(kf.T.astype(BF16), vbs[h])
    for h in heads:
        o = os_[h]
        ms = jnp.mean(o * o, axis=-1, keepdims=True)
        gr = col(g_refs, h)
        o_ref[:, hsl(h)] = (o * lax.rsqrt(ms + NORM_EPS) * nw_ref[:, hsl(h)]) * (gr * jax.nn.sigmoid(gr))

    @pl.when(c == pl.num_programs(1) - 1)
    def _():
        sf_ref[...] = s_ref[...]


def _split_dot_left(m, x):
    hi, lo = _split(x)
    return _dot(m, hi) + _dot(m, lo)


def _hgrn_prompt(proj, logits, nw, cmat, nseq, seq_len, layer):
    n = proj.shape[0]
    nch = seq_len // HGRN_CHUNK
    wb = 2 * HGRN_D
    npair = HGRN_HEADS // 2
    cols = [pl.BlockSpec((HGRN_CHUNK, wb), lambda b, c, j=(HGRN_COL0 + g * HGRN_W) // wb + p:
                         (b * nch + c, j)) for g in range(4) for p in range(npair)]
    return pl.pallas_call(
        functools.partial(_hgrn_prompt_body, layer=layer),
        grid=(nseq, nch),
        in_specs=cols + [pl.BlockSpec(logits.shape, lambda b, c: (0, 0)),
                         pl.BlockSpec((1, HGRN_W), lambda b, c: (0, 0)),
                         pl.BlockSpec(cmat.shape, lambda b, c: (0, 0))],
        out_specs=[pl.BlockSpec((HGRN_CHUNK, HGRN_W), lambda b, c: (b * nch + c, 0)),
                   pl.BlockSpec((None, HGRN_HEADS, HGRN_D, HGRN_D), lambda b, c: (b, 0, 0, 0))],
        out_shape=[jax.ShapeDtypeStruct((n, HGRN_W), F32),
                   jax.ShapeDtypeStruct((nseq, HGRN_HEADS, HGRN_D, HGRN_D), F32)],
        scratch_shapes=[pltpu.VMEM((HGRN_HEADS, HGRN_D, HGRN_D), F32)],
        compiler_params=_cp(2), name="hgrn_prompt")(*([proj] * (4 * npair)), logits, nw, cmat)


HGRN_KT = 32


def _hgrn_lanes_body(q_ref, f_ref, i_ref, g_ref, lg_ref, nw_ref, s0_ref, o_ref, sf_ref,
                     s_ref, qt_ref, ft_ref, kt_ref, vt_ref, oacc_ref, *, layer):
    kt = pl.program_id(1)
    ninst = q_ref.shape[1]
    lb = _hgrn_lower_bound(lg_ref[...], layer)
    s_ref[...] = s0_ref[0]

    @pl.when(kt == 0)
    def _():
        oacc_ref[...] = jnp.zeros_like(oacc_ref)
        for tl in range(SUBLANES):
            qr = q_ref[tl]
            fg = lb + (1.0 - lb) * jax.nn.sigmoid(f_ref[tl])
            qt_ref[tl] = (qr * jax.nn.sigmoid(qr)).T
            ft_ref[tl] = fg.T
            kt_ref[tl] = (1.0 - fg).T
            vt_ref[tl] = i_ref[tl].T

    def step(tl, carry):
        vt = vt_ref[tl]

        def krow(kk, o):
            row = kt * HGRN_KT + kk
            s = ft_ref[tl, pl.ds(row, 1), :] * s_ref[kk] + kt_ref[tl, pl.ds(row, 1), :] * vt
            s_ref[kk] = s
            return o + qt_ref[tl, pl.ds(row, 1), :] * s

        oacc_ref[tl] = lax.fori_loop(0, HGRN_KT, krow, oacc_ref[tl], unroll=2)
        return carry

    lax.fori_loop(0, SUBLANES, step, 0)
    sf_ref[0] = s_ref[...]

    @pl.when(kt == pl.num_programs(1) - 1)
    def _():
        for tl in range(SUBLANES):
            o = oacc_ref[tl].T
            ms = jnp.mean(o * o, axis=-1, keepdims=True)
            gr = g_ref[tl]
            o_ref[tl] = (o * lax.rsqrt(ms + NORM_EPS) * nw_ref[...]) * (gr * jax.nn.sigmoid(gr))


def _hgrn_lanes(proj_t, logits, nw, s0, layer):
    steps, ninst, _ = proj_t.shape
    col = lambda off: pl.BlockSpec((steps, ninst, HGRN_D),
                                   lambda h, k, off=off: (0, 0, off // HGRN_D + h))
    sspec = pl.BlockSpec((1, HGRN_KT, HGRN_D, ninst), lambda h, k: (h, k, 0, 0))
    tbuf = pltpu.VMEM((steps, HGRN_D, ninst), F32)
    return pl.pallas_call(
        functools.partial(_hgrn_lanes_body, layer=layer),
        grid=(HGRN_HEADS, HGRN_D // HGRN_KT),
        in_specs=[col(HGRN_COL0), col(HGRN_COL0 + HGRN_W), col(HGRN_COL0 + 2 * HGRN_W),
                  col(HGRN_COL0 + 3 * HGRN_W),
                  pl.BlockSpec((logits.shape[0], HGRN_D), lambda h, k: (0, h)),
                  pl.BlockSpec((1, HGRN_D), lambda h, k: (0, h)), sspec],
        out_specs=[pl.BlockSpec((steps, ninst, HGRN_D), lambda h, k: (0, 0, h)), sspec],
        out_shape=[jax.ShapeDtypeStruct((steps, ninst, HGRN_W), F32),
                   jax.ShapeDtypeStruct(s0.shape, F32)],
        scratch_shapes=[pltpu.VMEM((HGRN_KT, HGRN_D, ninst), F32), tbuf, tbuf, tbuf, tbuf, tbuf],
        compiler_params=_cp(2), name="hgrn_lanes")(
            proj_t, proj_t, proj_t, proj_t, logits, nw, s0)


def _rglru_body(xb_ref, gate_ref, chist_ref, hinit_ref, cw_ref, cb_ref, wa_ref, ba_ref, wx_ref,
                bx_ref, lam_ref, y_ref, cst_ref, hst_ref, ccarry_ref, hcarry_ref,
                *, long_seq, tps):
    x = xb_ref[...]
    tm, c = x.shape
    if long_seq:
        @pl.when(pl.program_id(0) % tps == 0)
        def _():
            ccarry_ref[...] = chist_ref[...]
            hcarry_ref[...] = hinit_ref[...]
        hist = ccarry_ref[...]
    else:
        hist = chist_ref[...]
    p1, p2, p3 = _prev_rows(x, hist, (1, 2, 3), long_seq)
    cw = cw_ref[...]
    xc = cb_ref[...] + cw[0:1] * p3 + cw[1:2] * p2 + cw[2:3] * p1 + cw[3:4] * x
    xcb = xc.astype(BF16)
    r = jax.nn.sigmoid(_dot(xcb, wa_ref[...]) + ba_ref[...])
    ig = jax.nn.sigmoid(_dot(xcb, wx_ref[...]) + bx_ref[...])
    log_a = (-LRU_C) * r * _softplus(-lam_ref[...])
    a = jnp.exp(log_a)
    h = jnp.sqrt(-_expm1(2.0 * log_a)) * (ig * xc)
    if long_seq:
        pos = lax.broadcasted_iota(jnp.int32, (tm, c), 0)
        k = 1
        while k < tm:
            keep = pos >= k
            h = h + a * jnp.where(keep, pltpu.roll(h, k, 0), 0.0)
            a = a * jnp.where(keep, pltpu.roll(a, k, 0), 1.0)
            k *= 2
        h = h + a * hcarry_ref[SUBLANES - 1:SUBLANES, :]
        hcarry_ref[...] = h[tm - SUBLANES:]
        ccarry_ref[...] = x[tm - SUBLANES:]
        cst_ref[...] = x[tm - SUBLANES:]
        hst_ref[...] = h[tm - SUBLANES:]
    else:
        shp = (tm // SUBLANES, SUBLANES, c)
        h3, a3 = h.reshape(shp), a.reshape(shp)
        pos = lax.broadcasted_iota(jnp.int32, shp, 1)
        k = 1
        while k < SUBLANES:
            keep = pos >= k
            h3 = h3 + a3 * jnp.where(keep, pltpu.roll(h3, k, 1), 0.0)
            a3 = a3 * jnp.where(keep, pltpu.roll(a3, k, 1), 1.0)
            k *= 2
        h = (h3 + a3 * hinit_ref[...].reshape(shp)).reshape(tm, c)
        cst_ref[...] = x
        hst_ref[...] = h
    y_ref[...] = h * jax.nn.gelu(gate_ref[...])


def _rglru(proj, chist, hinit, lp, seq_len, tm):
    n = proj.shape[0]
    long_seq = seq_len > SUBLANES
    tps = max(seq_len // tm, 1)
    c0 = LRU_COL0 // LRU_W
    if long_seq:
        hs = pl.BlockSpec((SUBLANES, LRU_W), lambda i: (i // tps, 0))
    else:
        hs = pl.BlockSpec((tm, LRU_W), lambda i: (i, 0))
    row = pl.BlockSpec((1, LRU_W), lambda i: (0, 0))
    sq = pl.BlockSpec((LRU_W, LRU_W), lambda i: (0, 0))
    return pl.pallas_call(
        functools.partial(_rglru_body, long_seq=long_seq, tps=tps), grid=(n // tm,),
        in_specs=[pl.BlockSpec((tm, LRU_W), lambda i: (i, c0)),
                  pl.BlockSpec((tm, LRU_W), lambda i: (i, c0 + 1)), hs, hs,
                  pl.BlockSpec((4, LRU_W), lambda i: (0, 0)), row, sq, row, sq, row, row],
        out_specs=[pl.BlockSpec((tm, LRU_W), lambda i: (i, 0)), hs, hs],
        out_shape=[jax.ShapeDtypeStruct((n, LRU_W), F32),
                   jax.ShapeDtypeStruct(chist.shape, F32),
                   jax.ShapeDtypeStruct(chist.shape, F32)],
        scratch_shapes=[pltpu.VMEM((SUBLANES, LRU_W), F32), pltpu.VMEM((SUBLANES, LRU_W), F32)],
        compiler_params=_cp(1), name="rglru")(
            proj, proj, chist, hinit, lp["lru_cw"], lp["lru_cb"], lp["lru_wa"], lp["lru_ba"],
            lp["lru_wx"], lp["lru_bx"], lp["lru_lam"])


def _hist(state):
    nseq, k, c = state.shape
    return jnp.pad(state, ((0, 0), (SUBLANES - k, 0), (0, 0))).reshape(nseq * SUBLANES, c)


def _block_diag(w):
    h, a, b = w.shape
    eye = jnp.eye(h, dtype=w.dtype)
    return (eye[:, None, :, None] * w[:, :, None, :]).reshape(h * a, h * b)


def _layer(x3, st, lp, consts, layer, final_nw):
    nseq, seq_len, d = x3.shape
    n = nseq * seq_len
    long_seq = seq_len > SUBLANES
    s_rw, shift_rw, s_hg, h_lru, buf_lru, buf_ffn = st
    x = x3.reshape(n, d)
    ones = consts["ones"]

    proj = _in_proj(x, lp["norm_mix"], consts["w_in"], layer, tm=min(n, 1024), tn=512)

    tm_pre = min(n, seq_len if long_seq else n, 256)
    outs = _rwkv_pre(proj, _hist(shift_rw[:, None, :]), lp, ones, seq_len, tm_pre)
    r, w, k, v, na, kb, g, bonus, shift_out = outs
    new_shift = shift_out.reshape(nseq, SUBLANES, RWKV_COLS)[:, SUBLANES - 1]
    if long_seq:
        c = RWKV_CHUNK
        n4 = _rwkv_n(w, na, kb, consts["lmat2"])
        t4 = _tri_solve(n4.transpose(0, 2, 3, 1)).transpose(0, 3, 1, 2)
        y, spt = _rwkv_chunk((r, w, k, v, na, kb), t4, consts["lmat"], nseq)
        sp = spt.reshape(nseq, RWKV_HEADS // 2, 2, RWKV_N, 2, RWKV_N)
        new_s_rw = jnp.stack([sp[:, :, 0, :, 0, :], sp[:, :, 1, :, 1, :]], axis=2)
        new_s_rw = new_s_rw.reshape(nseq, RWKV_HEADS, RWKV_N, RWKV_N).transpose(0, 1, 3, 2)
    else:
        tmaj = lambda t: t.reshape(nseq, seq_len, RWKV_W).transpose(1, 0, 2)
        s0 = s_rw.transpose(1, 2, 3, 0)
        yb, sf = _rwkv_lanes([tmaj(t) for t in (r, w, k, v, na, kb)], s0)
        y = yb.transpose(1, 0, 2).reshape(n, RWKV_W)
        new_s_rw = sf.transpose(3, 0, 1, 2)
    y_rw = _rwkv_post(y, bonus, g, lp["ln_w"], lp["ln_b"], ones, tm=min(n, 512))

    if long_seq:
        y_hg, new_s_hg = _hgrn_prompt(proj, consts["lb_logits"], lp["hgrn_nw"], consts["cmat"],
                                      nseq, seq_len, layer)
    else:
        proj_t = proj.reshape(nseq, seq_len, IN_COLS).transpose(1, 0, 2)
        s0 = s_hg.transpose(1, 2, 3, 0)
        o_t, sf = _hgrn_lanes(proj_t, consts["lb_logits"], lp["hgrn_nw"], s0, layer)
        y_hg = o_t.transpose(1, 0, 2).reshape(n, HGRN_W)
        new_s_hg = sf.transpose(3, 0, 1, 2)

    if long_seq:
        hinit = _hist(h_lru[:, None, :])
        tm_lru = min(seq_len, 256)
    else:
        hinit = jnp.broadcast_to(h_lru[:, None, :], (nseq, seq_len, LRU_W)).reshape(n, LRU_W)
        tm_lru = min(n, 512)
    y_lru, cst, hst = _rglru(proj, _hist(buf_lru), hinit, lp, seq_len, tm_lru)
    new_buf_lru = cst.reshape(nseq, SUBLANES, LRU_W)[:, SUBLANES - 3:]
    new_h_lru = hst.reshape(nseq, SUBLANES, LRU_W)[:, SUBLANES - 1]

    x = _out_proj(x, y_rw, y_hg, y_lru, consts["w_out"], layer, tm=min(n, 1024), tn=512)
    tm_ffn = min(n, seq_len if long_seq else n, 1024)
    outs = _ffn(x, lp["norm_ffn"], consts["w_up"], lp["ffn_cw"], lp["ffn_cb"], consts["w_down"],
                _hist(buf_ffn), layer, seq_len, tm=tm_ffn, tf=512, final_nw=final_nw)
    x, fst = outs
    new_buf_ffn = fst.reshape(nseq, -1, SUBLANES, D_FF)[:, -1, SUBLANES - 2:]
    return x.reshape(nseq, seq_len, d), (new_s_rw, new_shift, new_s_hg, new_h_lru, new_buf_lru,
                                         new_buf_ffn)


def kernel(x_prompt, x_sample, state_rwkv, state_rwkv_shift, state_hgrn, state_rglru, cache_rglru_conv, cache_ffn_conv, norm_mix, w_in, rwkv_mu, rwkv_w0, rwkv_w2, rwkv_a0, rwkv_a2, rwkv_g2, rwkv_k_k, rwkv_k_a, rwkv_r_k, rwkv_ln_w, rwkv_ln_b, hgrn_lb_logits, hgrn_norm_w, rglru_conv_w, rglru_conv_b, rglru_wa, rglru_ba, rglru_wx, rglru_bx, rglru_lambda, w_out, norm_ffn, ffn_w_up, ffn_conv_w, ffn_conv_b, ffn_w_down, norm_final):
    depth = w_in.shape[0]
    nb, nt, _ = x_prompt.shape
    db = x_sample.shape[0]
    consts = {
        "ones": jnp.asarray(np.kron(np.eye(RWKV_HEADS), np.ones((RWKV_N, RWKV_N))), dtype=BF16),
        "cmat": _hgrn_level_matrix(),
        "lmat": jnp.asarray(np.tril(np.ones((RWKV_CHUNK, RWKV_CHUNK), np.float32)), dtype=BF16),
        "lmat2": jnp.asarray(np.kron(np.eye(RWKV_N_CHUNKS), np.tril(np.ones((RWKV_CHUNK, RWKV_CHUNK)))),
                             dtype=BF16),
        "lb_logits": hgrn_lb_logits,
        "w_in": w_in.astype(BF16), "w_out": w_out.astype(BF16),
        "w_up": ffn_w_up.astype(BF16), "w_down": ffn_w_down.astype(BF16),
    }
    rowv = lambda a: a.reshape(1, -1)
    x_p, x_s = x_prompt, x_sample
    new_p, new_s = [], []
    for l in range(depth):
        lp = {
            "norm_mix": rowv(norm_mix[l]), "mu": rowv(rwkv_mu[l]),
            "w0": rowv(rwkv_w0[l]),
            "w2p": jnp.pad(rwkv_w2[l], ((0, 64), (0, 0))).astype(BF16),
            "a0": rowv(rwkv_a0[l]),
            "a2p": jnp.pad(rwkv_a2[l], ((64, 0), (0, 0))).astype(BF16),
            "g2": rwkv_g2[l].astype(BF16), "k_k": rowv(rwkv_k_k[l]), "k_a": rowv(rwkv_k_a[l]),
            "r_k": rowv(rwkv_r_k[l]), "ln_w": rowv(rwkv_ln_w[l]), "ln_b": rowv(rwkv_ln_b[l]),
            "hgrn_nw": rowv(hgrn_norm_w[l]),
            "lru_cw": rglru_conv_w[l], "lru_cb": rowv(rglru_conv_b[l]),
            "lru_wa": _block_diag(rglru_wa[l]).astype(BF16), "lru_ba": rowv(rglru_ba[l]),
            "lru_wx": _block_diag(rglru_wx[l]).astype(BF16), "lru_bx": rowv(rglru_bx[l]),
            "lru_lam": rowv(rglru_lambda[l]),
            "norm_ffn": rowv(norm_ffn[l]), "ffn_cw": ffn_conv_w[l],
            "ffn_cb": rowv(ffn_conv_b[l]),
        }
        zero = lambda *s: jnp.zeros(s, F32)
        st_p = (None, zero(nb, RWKV_COLS), None, zero(nb, LRU_W), zero(nb, 3, LRU_W),
                zero(nb, 2, D_FF))
        final_nw = rowv(norm_final) if l == depth - 1 else None
        x_p, sp = _layer(x_p, st_p, lp, consts, l, final_nw)
        st_s = (state_rwkv[l], state_rwkv_shift[l], state_hgrn[l], state_rglru[l],
                cache_rglru_conv[l], cache_ffn_conv[l])
        x_s, ss = _layer(x_s, st_s, lp, consts, l, final_nw)
        new_p.append(sp)
        new_s.append(ss)
    stack = lambda sts: [jnp.stack(s, axis=0) for s in zip(*sts)]
    return (x_p, x_s, *stack(new_p), *stack(new_s))
```

```python
import functools

import numpy as np
import jax
import jax.numpy as jnp
from jax import lax
from jax.experimental import pallas as pl
from jax.experimental.pallas import tpu as pltpu

F32 = jnp.float32
BF16 = jnp.bfloat16

D_MODEL = 2048
RWKV_HEADS = 12
RWKV_N = 64
RWKV_W = RWKV_HEADS * RWKV_N
RWKV_COLS = 2560
RWKV_LN_EPS = 64e-5
HGRN_HEADS = 6
HGRN_D = 128
HGRN_W = HGRN_HEADS * HGRN_D
HGRN_COL0 = RWKV_COLS
LRU_W = 512
LRU_COL0 = RWKV_COLS + 4 * HGRN_W
LRU_C = 8.0
IN_COLS = 6656
D_FF = 5632
NORM_EPS = 1e-6

SUBLANES = 8
LANES = 128
RWKV_CHUNK = 64
HGRN_CHUNK = 128
HGRN_LEVELS = 7
VMEM_LIMIT = 48 * 1024 * 1024
BIG_VMEM_LIMIT = 56 * 1024 * 1024


def _cp(n, limit=VMEM_LIMIT):
    return pltpu.CompilerParams(dimension_semantics=("arbitrary",) * n, vmem_limit_bytes=limit)


def _dot(a, b):
    return jnp.dot(a, b, preferred_element_type=F32)


def _dot_nt(a, b):
    return lax.dot_general(a, b, (((1,), (1,)), ((), ())), preferred_element_type=F32)


def _split(x):
    hi = x.astype(BF16)
    lo = (x - hi.astype(F32)).astype(BF16)
    return hi, lo


def _split_dot(x, m):
    hi, lo = _split(x)
    return _dot(hi, m) + _dot(lo, m)


def _dot3(a, b, nt=False):
    f = _dot_nt if nt else _dot
    ah, al = _split(a)
    bh, bl = _split(b)
    return f(ah, bh) + (f(ah, bl) + f(al, bh))


def _softplus(z):
    return jnp.maximum(z, 0.0) + jnp.log1p(jnp.exp(-jnp.abs(z)))


def _expm1(z):
    return jnp.tanh(0.5 * z) * (jnp.exp(z) + 1.0)


def _prev_rows(x, hist, ks, long_seq):
    tm, c = x.shape
    if long_seq:
        ext = jnp.concatenate([hist, x], axis=0)
        return [pltpu.roll(ext, k, 0)[SUBLANES:] for k in ks]
    x3 = x.reshape(tm // SUBLANES, SUBLANES, c)
    h3 = hist.reshape(tm // SUBLANES, SUBLANES, c)
    pos = lax.broadcasted_iota(jnp.int32, x3.shape, 1)
    return [jnp.where(pos >= k, pltpu.roll(x3, k, 1), pltpu.roll(h3, k, 1)).reshape(tm, c)
            for k in ks]


def _cast_body(x_ref, o_ref):
    o_ref[...] = x_ref[...].astype(o_ref.dtype)


def _cast_bf16(w, tr):
    depth, r, c = w.shape
    spec = pl.BlockSpec((None, tr, c), lambda l, i: (l, i, 0))
    return pl.pallas_call(
        _cast_body, grid=(depth, r // tr), in_specs=[spec], out_specs=spec,
        out_shape=jax.ShapeDtypeStruct(w.shape, BF16),
        compiler_params=_cp(2), name="cast_bf16")(w)


def _in_proj_body(x_ref, nw_ref, w_ref, o_ref, xn_ref):
    @pl.when(pl.program_id(1) == 0)
    def _():
        x = x_ref[...]
        ms = jnp.mean(x * x, axis=-1, keepdims=True)
        xn_ref[...] = (x * lax.rsqrt(ms + NORM_EPS) * nw_ref[...]).astype(BF16)
    o_ref[...] = _dot(xn_ref[...], w_ref[...])


def _in_proj(x, nw, w, layer, tm, tn):
    n, d = x.shape
    c = w.shape[2]
    return pl.pallas_call(
        _in_proj_body, grid=(n // tm, c // tn),
        in_specs=[pl.BlockSpec((tm, d), lambda i, j: (i, 0)),
                  pl.BlockSpec((1, d), lambda i, j: (0, 0)),
                  pl.BlockSpec((None, d, tn), lambda i, j: (layer, 0, j))],
        out_specs=pl.BlockSpec((tm, tn), lambda i, j: (i, j)),
        out_shape=jax.ShapeDtypeStruct((n, c), F32),
        scratch_shapes=[pltpu.VMEM((tm, d), BF16)],
        compiler_params=_cp(2, BIG_VMEM_LIMIT), name="in_proj")(x, nw, w)


def _out_proj_body(x_ref, ya_ref, yb_ref, yc_ref, w_ref, o_ref, y_ref):
    @pl.when(pl.program_id(1) == 0)
    def _():
        y_ref[:, 0:RWKV_W] = ya_ref[...].astype(BF16)
        y_ref[:, RWKV_W:RWKV_W + HGRN_W] = yb_ref[...].astype(BF16)
        y_ref[:, RWKV_W + HGRN_W:] = yc_ref[...].astype(BF16)
    o_ref[...] = x_ref[...] + _dot(y_ref[...], w_ref[...])


def _out_proj(x, ya, yb, yc, w, layer, tm, tn):
    n, d = x.shape
    return pl.pallas_call(
        _out_proj_body, grid=(n // tm, d // tn),
        in_specs=[pl.BlockSpec((tm, tn), lambda i, j: (i, j)),
                  pl.BlockSpec((tm, RWKV_W), lambda i, j: (i, 0)),
                  pl.BlockSpec((tm, HGRN_W), lambda i, j: (i, 0)),
                  pl.BlockSpec((tm, LRU_W), lambda i, j: (i, 0)),
                  pl.BlockSpec((None, d, tn), lambda i, j: (layer, 0, j))],
        out_specs=pl.BlockSpec((tm, tn), lambda i, j: (i, j)),
        out_shape=jax.ShapeDtypeStruct((n, d), F32),
        scratch_shapes=[pltpu.VMEM((tm, d), BF16)],
        compiler_params=_cp(2), name="out_proj")(x, ya, yb, yc, w)


def _ffn_body(*refs, long_seq, tps, final):
    x_ref, nw_ref, wg_ref, wv_ref, cw_ref, cb_ref, wd_ref, hist_ref = refs[:8]
    if final:
        fnw_ref, o_ref, st_ref, hn_ref, carry_ref = refs[8:]
    else:
        o_ref, st_ref, hn_ref, carry_ref = refs[8:]
    i = pl.program_id(0)
    j = pl.program_id(1)

    @pl.when(j == 0)
    def _():
        x = x_ref[...]
        ms = jnp.mean(x * x, axis=-1, keepdims=True)
        hn_ref[...] = (x * lax.rsqrt(ms + NORM_EPS) * nw_ref[...]).astype(BF16)
        o_ref[...] = x

    hn = hn_ref[...]
    if long_seq:
        @pl.when(i % tps == 0)
        def _():
            carry_ref[j] = hist_ref[...]
        hist = carry_ref[j]
    else:
        hist = hist_ref[...]
    tf = wg_ref.shape[1]
    halves = [slice(q * (tf // 2), (q + 1) * (tf // 2)) for q in range(2)]
    gs = [_dot(hn, wg_ref[:, sl]) for sl in halves]
    vs = [_dot(hn, wv_ref[:, sl]) for sl in halves]
    cw = cw_ref[...]
    acc = None
    for sl, g, v in zip(halves, gs, vs):
        p1, p2 = _prev_rows(g, hist[:, sl], (1, 2), long_seq)
        gc = cb_ref[:, sl] + cw[0:1, sl] * p2 + cw[1:2, sl] * p1 + cw[2:3, sl] * g
        h = (gc * jax.nn.sigmoid(gc)) * v
        d = _dot(h.astype(BF16), wd_ref[sl, :])
        acc = d if acc is None else acc + d
        tail = g if not long_seq else g[g.shape[0] - SUBLANES:]
        st_ref[:, sl] = tail
        if long_seq:
            carry_ref[j, :, sl] = tail
    o_ref[...] += acc

    if final:
        @pl.when(j == pl.num_programs(1) - 1)
        def _():
            y = o_ref[...]
            ms = jnp.mean(y * y, axis=-1, keepdims=True)
            o_ref[...] = y * lax.rsqrt(ms + NORM_EPS) * fnw_ref[...]


def _ffn(x, nw, w_up, cw, cb, w_down, hist, layer, seq_len, tm, tf, final_nw=None):
    n, d = x.shape
    final = final_nw is not None
    nf = D_FF // tf
    long_seq = seq_len > SUBLANES
    tps = max(seq_len // tm, 1)
    if long_seq:
        hist_spec = pl.BlockSpec((SUBLANES, tf), lambda i, j: (i // tps, j))
        st_spec = pl.BlockSpec((SUBLANES, tf), lambda i, j: (i, j))
        st_rows = n // tm * SUBLANES
    else:
        hist_spec = pl.BlockSpec((tm, tf), lambda i, j: (i, j))
        st_spec = pl.BlockSpec((tm, tf), lambda i, j: (i, j))
        st_rows = n
    body = functools.partial(_ffn_body, long_seq=long_seq, tps=tps, final=final)
    row_tile = pl.BlockSpec((tm, d), lambda i, j: (i, 0))
    wide = pl.BlockSpec((1, d), lambda i, j: (0, 0))
    in_specs = [row_tile, wide,
                pl.BlockSpec((None, d, tf), lambda i, j: (layer, 0, j)),
                pl.BlockSpec((None, d, tf), lambda i, j: (layer, 0, j + nf)),
                pl.BlockSpec((3, tf), lambda i, j: (0, j)),
                pl.BlockSpec((1, tf), lambda i, j: (0, j)),
                pl.BlockSpec((None, tf, d), lambda i, j: (layer, j, 0)),
                hist_spec]
    args = [x, nw, w_up, w_up, cw, cb, w_down, hist]
    out_specs = [row_tile, st_spec]
    out_shape = [jax.ShapeDtypeStruct((n, d), F32), jax.ShapeDtypeStruct((st_rows, D_FF), F32)]
    scratch = [pltpu.VMEM((tm, d), BF16), pltpu.VMEM((nf, SUBLANES, tf), F32)]
    if final:
        in_specs.append(wide)
        args.append(final_nw)
    return pl.pallas_call(
        body, grid=(n // tm, nf), in_specs=in_specs, out_specs=out_specs, out_shape=out_shape,
        scratch_shapes=scratch,
        compiler_params=pltpu.CompilerParams(dimension_semantics=("arbitrary",) * 2,
                                             vmem_limit_bytes=BIG_VMEM_LIMIT),
        name="ffn")(*args)


def _rwkv_pre_body(p_ref, hist_ref, mu_ref, w0_ref, w2_ref, a0_ref, a2_ref, g2_ref, kk_ref,
                   ka_ref, rk_ref, ones_ref,
                   r_ref, w_ref, k_ref, v_ref, na_ref, kb_ref, g_ref, bonus_ref, st_ref,
                   carry_ref, *, long_seq, tps):
    p = p_ref[...]
    if long_seq:
        @pl.when(pl.program_id(0) % tps == 0)
        def _():
            carry_ref[...] = hist_ref[...]
        hist = carry_ref[...]
    else:
        hist = hist_ref[...]
    (prev,) = _prev_rows(p, hist, (1,), long_seq)
    if long_seq:
        carry_ref[...] = p[p.shape[0] - SUBLANES:]
        st_ref[...] = p[p.shape[0] - SUBLANES:]
    else:
        st_ref[...] = p
    xs = p + (prev - p) * mu_ref[...]
    r = xs[:, 0:RWKV_W]
    k = xs[:, RWKV_W:2 * RWKV_W]
    v = xs[:, 2 * RWKV_W:3 * RWKV_W]
    xwa = xs[:, 3 * RWKV_W:3 * RWKV_W + 128]
    xg = xs[:, 3 * RWKV_W + 128:RWKV_COLS]
    zw = w0_ref[...] + _dot(jnp.tanh(xwa).astype(BF16), w2_ref[...])
    w_log = -_softplus(-zw) - 0.5
    decay = jnp.exp(-jnp.exp(w_log))
    a = jax.nn.sigmoid(a0_ref[...] + _dot(xwa.astype(BF16), a2_ref[...]))
    g = _dot(jax.nn.sigmoid(xg).astype(BF16), g2_ref[...])
    ones = ones_ref[...]
    kk = k * kk_ref[...]
    kk = kk / jnp.maximum(jnp.sqrt(_split_dot(kk * kk, ones)), 1e-12)
    kf = k * (1.0 + (a - 1.0) * ka_ref[...])
    r_ref[...] = r
    w_ref[...] = decay
    k_ref[...] = kf
    v_ref[...] = v
    na_ref[...] = -kk
    kb_ref[...] = kk * a
    g_ref[...] = g
    bonus_ref[...] = _split_dot(r * kf * rk_ref[...], ones) * v


def _rwkv_pre(proj, hist, lp, ones, seq_len, tm):
    n = proj.shape[0]
    long_seq = seq_len > SUBLANES
    tps = max(seq_len // tm, 1)
    row = lambda c: pl.BlockSpec((1, c), lambda i: (0, 0))
    full = lambda a, b: pl.BlockSpec((a, b), lambda i: (0, 0))
    if long_seq:
        hist_spec = pl.BlockSpec((SUBLANES, RWKV_COLS), lambda i: (i // tps, 0))
    else:
        hist_spec = pl.BlockSpec((tm, RWKV_COLS), lambda i: (i, 0))
    tile = pl.BlockSpec((tm, RWKV_W), lambda i: (i, 0))
    body = functools.partial(_rwkv_pre_body, long_seq=long_seq, tps=tps)
    outs = pl.pallas_call(
        body, grid=(n // tm,),
        in_specs=[pl.BlockSpec((tm, RWKV_COLS), lambda i: (i, 0)), hist_spec,
                  row(RWKV_COLS), row(RWKV_W), full(128, RWKV_W), row(RWKV_W),
                  full(128, RWKV_W), full(128, RWKV_W), row(RWKV_W), row(RWKV_W), row(RWKV_W),
                  full(RWKV_W, RWKV_W)],
        out_specs=[tile] * 8 + [hist_spec],
        out_shape=[jax.ShapeDtypeStruct((n, RWKV_W), F32)] * 8
                  + [jax.ShapeDtypeStruct(hist.shape, F32)],
        scratch_shapes=[pltpu.VMEM((SUBLANES, RWKV_COLS), F32)],
        compiler_params=_cp(1), name="rwkv_pre")(
            proj, hist, lp["mu"], lp["w0"], lp["w2p"], lp["a0"], lp["a2p"], lp["g2"],
            lp["k_k"], lp["k_a"], lp["r_k"], ones)
    return outs


def _rwkv_post_body(y_ref, bonus_ref, g_ref, lw_ref, lb_ref, ones_ref, o_ref):
    y = y_ref[...]
    ones = ones_ref[...]
    mean = _split_dot(y, ones) * (1.0 / RWKV_N)
    d = y - mean
    var = _split_dot(d * d, ones) * (1.0 / RWKV_N)
    yn = d * lax.rsqrt(var + RWKV_LN_EPS)
    o_ref[...] = (yn * lw_ref[...] + lb_ref[...] + bonus_ref[...]) * g_ref[...]


def _rwkv_post(y, bonus, g, lw, lb, ones, tm):
    n = y.shape[0]
    tile = pl.BlockSpec((tm, RWKV_W), lambda i: (i, 0))
    row = pl.BlockSpec((1, RWKV_W), lambda i: (0, 0))
    return pl.pallas_call(
        _rwkv_post_body, grid=(n // tm,),
        in_specs=[tile, tile, tile, row, row, pl.BlockSpec((RWKV_W, RWKV_W), lambda i: (0, 0))],
        out_specs=tile, out_shape=jax.ShapeDtypeStruct((n, RWKV_W), F32),
        compiler_params=_cp(1), name="rwkv_post")(y, bonus, g, lw, lb, ones)


def _rwkv_lanes_body(r_ref, w_ref, k_ref, v_ref, a_ref, b_ref, s0_ref, y_ref, sf_ref,
                     s_ref, xt_ref, y_buf):
    tb = pl.program_id(1)
    ninst = r_ref.shape[1]

    @pl.when(tb == 0)
    def _():
        for hh in range(2):
            s_ref[hh] = s0_ref[hh]

    def step(tl, carry):
        for idx, ref in enumerate((r_ref, w_ref, k_ref, v_ref, a_ref, b_ref)):
            xt_ref[idx] = ref[tl].T
        for hh in range(2):
            base = hh * RWKV_N
            heads = lambda i: xt_ref[i, base:base + RWKV_N, :]

            def row(rho, c):
                s = s_ref[hh, rho]
                sa = jnp.sum(s * heads(4), axis=0, keepdims=True)
                vrow = xt_ref[3, pl.ds(base + rho, 1), :]
                s = s * heads(1) + sa * heads(5) + vrow * heads(2)
                s_ref[hh, rho] = s
                y_buf[pl.ds(base + rho, 1), :] = jnp.sum(s * heads(0), axis=0, keepdims=True)
                return c

            lax.fori_loop(0, RWKV_N, row, 0, unroll=8)
        y_ref[tl] = y_buf[...].T
        return carry

    lax.fori_loop(0, SUBLANES, step, 0)

    @pl.when(tb == pl.num_programs(1) - 1)
    def _():
        for hh in range(2):
            sf_ref[hh] = s_ref[hh]


def _rwkv_lanes(xs, s0):
    steps, ninst, _ = xs[0].shape
    xspec = pl.BlockSpec((SUBLANES, ninst, 2 * RWKV_N), lambda h, t: (t, 0, h))
    sspec = pl.BlockSpec((2, RWKV_N, RWKV_N, ninst), lambda h, t: (h, 0, 0, 0))
    return pl.pallas_call(
        _rwkv_lanes_body, grid=(RWKV_HEADS // 2, steps // SUBLANES),
        in_specs=[xspec] * 6 + [sspec], out_specs=[xspec, sspec],
        out_shape=[jax.ShapeDtypeStruct((steps, ninst, RWKV_W), F32),
                   jax.ShapeDtypeStruct(s0.shape, F32)],
        scratch_shapes=[pltpu.VMEM((2, RWKV_N, RWKV_N, ninst), F32),
                        pltpu.VMEM((6, 2 * RWKV_N, ninst), F32),
                        pltpu.VMEM((2 * RWKV_N, ninst), F32)],
        compiler_params=_cp(2), name="rwkv_lanes")(*xs, s0)


RWKV_N_CHUNKS = 2
RWKV_C_CHUNKS = 2


def _rwkv_n_body(w_ref, a_ref, b_ref, l_ref, n_ref):
    c = RWKV_CHUNK
    rows = RWKV_N_CHUNKS * c
    lmat = l_ref[...]
    strict = lax.broadcasted_iota(jnp.int32, (c, c), 1) < lax.broadcasted_iota(jnp.int32, (c, c), 0)
    lane = lax.broadcasted_iota(jnp.int32, (rows, 2 * RWKV_N), 1)
    pairs = range(RWKV_HEADS // 2)
    sls = [slice(p * 2 * RWKV_N, (p + 1) * 2 * RWKV_N) for p in pairs]
    lws = [jnp.log(w_ref[:, sl]) for sl in sls]
    cws = [_split_dot_left(lmat, lw) for lw in lws]
    ats = [_split(a_ref[:, sl] * jnp.exp(cw - lw)) for sl, cw, lw in zip(sls, cws, lws)]
    bts = [_split(b_ref[:, sl] * jnp.exp(-cw)) for sl, cw in zip(sls, cws)]
    for p in pairs:
        for hh in range(2):
            m = (lane < RWKV_N) if hh == 0 else (lane >= RWKV_N)
            ah = jnp.where(m, ats[p][0], jnp.zeros_like(ats[p][0]))
            al = jnp.where(m, ats[p][1], jnp.zeros_like(ats[p][1]))
            for ci in range(RWKV_N_CHUNKS):
                rs = slice(ci * c, (ci + 1) * c)
                bh, bl = bts[p][0][rs], bts[p][1][rs]
                n = _dot_nt(ah[rs], bh) + (_dot_nt(ah[rs], bl) + _dot_nt(al[rs], bh))
                n_ref[2 * p + hh, ci] = jnp.where(strict, n, 0.0)


def _rwkv_n(w, a, b, lmat2):
    n = w.shape[0]
    c = RWKV_CHUNK
    rows = RWKV_N_CHUNKS * c
    tile = pl.BlockSpec((rows, RWKV_W), lambda i: (i, 0))
    return pl.pallas_call(
        _rwkv_n_body, grid=(n // rows,),
        in_specs=[tile, tile, tile, pl.BlockSpec((rows, rows), lambda i: (0, 0))],
        out_specs=pl.BlockSpec((RWKV_HEADS, RWKV_N_CHUNKS, c, c), lambda i: (0, i, 0, 0)),
        out_shape=jax.ShapeDtypeStruct((RWKV_HEADS, n // c, c, c), F32),
        compiler_params=_cp(1), name="rwkv_n")(w, a, b, lmat2)


def _tri_solve_body(nt_ref, tt_ref):
    c = RWKV_CHUNK
    ninst = nt_ref.shape[2]
    tt_ref[...] = jnp.zeros_like(tt_ref)
    g = SUBLANES
    nb = 4
    sub = lax.broadcasted_iota(jnp.int32, (g, ninst), 0)
    for t0 in range(0, c, nb):
        ngrp = (t0 + nb - 1) // g + 1
        rs = [[jnp.where(sub + g * j == t0 + i, 1.0, 0.0).astype(F32) for j in range(ngrp)]
              for i in range(nb)]
        for sg in range(-(-t0 // g)):

            def acc(s, carry, sg=sg, t0=t0):
                ts = [tt_ref[s, j * g:(j + 1) * g, :] for j in range(sg + 1)]
                nrow = [nt_ref[t0 + i, pl.ds(s, 1), :] for i in range(nb)]
                return tuple(tuple(carry[i][j] + nrow[i] * ts[j] for j in range(sg + 1))
                             for i in range(nb))

            res = lax.fori_loop(sg * g, min(sg * g + g, t0), acc,
                                tuple(tuple(rs[i][:sg + 1]) for i in range(nb)))
            for i in range(nb):
                rs[i][:sg + 1] = list(res[i])
        for i in range(nb):
            for j in range(i):
                nij = nt_ref[t0 + i, t0 + j:t0 + j + 1, :]
                rs[i] = [x + nij * y for x, y in zip(rs[i], rs[j])]
            tt_ref[t0 + i, 0:ngrp * g, :] = jnp.concatenate(rs[i], axis=0)


def _tri_solve(nt):
    nh, c, _, ninst = nt.shape
    spec = pl.BlockSpec((None, c, c, ninst), lambda h: (h, 0, 0, 0))
    return pl.pallas_call(
        _tri_solve_body, grid=(nh,), in_specs=[spec], out_specs=spec,
        out_shape=jax.ShapeDtypeStruct(nt.shape, F32),
        compiler_params=_cp(1), name="tri_solve")(nt)


def _rwkv_chunk_body(r_ref, w_ref, k_ref, v_ref, a_ref, b_ref, t_ref, l_ref, y_ref, sf_ref, s_ref):
    c = RWKV_CHUNK
    pw = 2 * RWKV_N
    ci = pl.program_id(1)

    @pl.when(ci == 0)
    def _():
        s_ref[...] = jnp.zeros_like(s_ref)

    lmat = l_ref[...]
    ti = lax.broadcasted_iota(jnp.int32, (c, c), 0)
    si = lax.broadcasted_iota(jnp.int32, (c, c), 1)
    strict, incl = si < ti, si <= ti
    lane = lax.broadcasted_iota(jnp.int32, (c, pw), 1)
    lane2 = lax.broadcasted_iota(jnp.int32, (2 * c, pw), 1)
    r2 = lax.broadcasted_iota(jnp.int32, (pw, pw), 0)
    c2 = lax.broadcasted_iota(jnp.int32, (pw, pw), 1)
    same_head = (r2 < RWKV_N) == (c2 < RWKV_N)
    eye = r2 == c2
    bz = lambda x: jnp.zeros_like(x)
    pairs = range(RWKV_HEADS // 2)
    units = [(q, p) for q in range(RWKV_C_CHUNKS) for p in pairs]
    at = lambda ref, u: ref[u[0] * c:(u[0] + 1) * c, u[1] * pw:(u[1] + 1) * pw]
    halves = [(u, hh) for u in units for hh in range(2)]
    lws = {u: jnp.log(at(w_ref, u)) for u in units}
    cws = {u: _split_dot_left(lmat, lws[u]) for u in units}
    vs = {u: at(v_ref, u) for u in units}
    wts = {u: jnp.exp(cws[u]) for u in units}
    rts = {u: at(r_ref, u) * wts[u] for u in units}
    ats = {u: at(a_ref, u) * jnp.exp(cws[u] - lws[u]) for u in units}
    lhs = {u: _split(jnp.concatenate([ats[u], rts[u]], axis=0)) for u in units}
    bts = {u: _split(at(b_ref, u) * jnp.exp(-cws[u])) for u in units}
    kts = {u: _split(at(k_ref, u) * jnp.exp(-cws[u])) for u in units}
    a_ak, a_rk, a_rb = {}, {}, {}
    for u, hh in halves:
        m2 = (lane2 < RWKV_N) if hh == 0 else (lane2 >= RWKV_N)
        lh = jnp.where(m2, lhs[u][0], bz(lhs[u][0]))
        ll = jnp.where(m2, lhs[u][1], bz(lhs[u][1]))
        pak = (_dot_nt(lh[:c], kts[u][0])
               + (_dot_nt(lh[:c], kts[u][1]) + _dot_nt(ll[:c], kts[u][0])))
        a_ak[u, hh] = jnp.where(strict, pak, 0.0)
        a_rk[u, hh] = jnp.where(incl, _dot_nt(lh[c:], kts[u][0]), 0.0).astype(BF16)
        a_rb[u, hh] = jnp.where(incl, _dot_nt(lh[c:], bts[u][0]), 0.0).astype(BF16)
    vps = {uh: _dot3(a_ak[uh], vs[uh[0]]) for uh in halves}
    gs = {(u, hh): _dot3(t_ref[2 * u[1] + hh, u[0]], jnp.concatenate([ats[u], vps[u, hh]], axis=1))
          for u, hh in halves}
    pick = lambda x0, x1: jnp.where(lane < RWKV_N, x0, x1)
    ahats = {u: pick(gs[u, 0][:, :pw], gs[u, 1][:, :pw]) for u in units}
    vhats = {u: pick(gs[u, 0][:, pw:], gs[u, 1][:, pw:]) for u in units}
    ahb = {u: ahats[u].astype(BF16) for u in units}
    vhb = {u: vhats[u].astype(BF16) for u in units}
    vbs = {u: vs[u].astype(BF16) for u in units}
    rhats = {u: (rts[u] + pick(_dot(a_rb[u, 0], ahb[u]), _dot(a_rb[u, 1], ahb[u]))).astype(BF16)
             for u in units}
    yhats = {u: pick(_dot(a_rb[u, 0], vhb[u]) + _dot(a_rk[u, 0], vbs[u]),
                     _dot(a_rb[u, 1], vhb[u]) + _dot(a_rk[u, 1], vbs[u])) for u in units}
    wends = {u: jnp.exp(cws[u][c - 1:c] - cws[u]) for u in units}
    bkts = {u: jnp.concatenate([at(b_ref, u) * wends[u], at(k_ref, u) * wends[u]], axis=0).T
            for u in units}
    mpts = {u: jnp.where(same_head, _dot3(bkts[u], jnp.concatenate([ahats[u], bz(ahats[u])], axis=0)), 0.0)
            + jnp.where(eye, wts[u][c - 1:c], 0.0) for u in units}
    zpts = {u: jnp.where(same_head, _dot3(bkts[u], jnp.concatenate([vhats[u], vs[u]], axis=0)), 0.0)
            for u in units}
    spts = [s_ref[p] for p in pairs]
    for q in range(RWKV_C_CHUNKS):
        for p in pairs:
            y_ref[q * c:(q + 1) * c, p * pw:(p + 1) * pw] = (
                _dot(rhats[q, p], spts[p].astype(BF16)) + yhats[q, p])
        spts = [_dot3(mpts[q, p], spts[p]) + zpts[q, p] for p in pairs]
    for p in pairs:
        s_ref[p] = spts[p]

    @pl.when(ci == pl.num_programs(1) - 1)
    def _():
        sf_ref[...] = s_ref[...]


def _rwkv_chunk(xs, t4, lmat, nseq):
    n = xs[0].shape[0]
    c = RWKV_CHUNK
    q = RWKV_C_CHUNKS
    nch = n // (q * c) // nseq
    pw = 2 * RWKV_N
    tile = pl.BlockSpec((q * c, RWKV_W), lambda b, i: (b * nch + i, 0))
    return pl.pallas_call(
        _rwkv_chunk_body, grid=(nseq, nch),
        in_specs=[tile] * 6 + [pl.BlockSpec((RWKV_HEADS, q, c, c), lambda b, i: (0, b * nch + i, 0, 0)),
                               pl.BlockSpec((c, c), lambda b, i: (0, 0))],
        out_specs=[tile, pl.BlockSpec((None, RWKV_HEADS // 2, pw, pw), lambda b, i: (b, 0, 0, 0))],
        out_shape=[jax.ShapeDtypeStruct((n, RWKV_W), F32),
                   jax.ShapeDtypeStruct((nseq, RWKV_HEADS // 2, pw, pw), F32)],
        scratch_shapes=[pltpu.VMEM((RWKV_HEADS // 2, pw, pw), F32)],
        compiler_params=_cp(2), name="rwkv_chunk")(*xs, t4, lmat)


def _hgrn_lower_bound(logits, layer):
    m = jnp.max(logits, axis=0, keepdims=True)
    e = jnp.exp(logits - m)
    gam = e / jnp.sum(e, axis=0, keepdims=True)
    cs = gam[0:1]
    for i in range(1, layer + 1):
        cs = cs + gam[i:i + 1]
    return cs - gam[0:1]


def _hgrn_level_matrix():
    t = np.arange(HGRN_CHUNK)[:, None]
    s = np.arange(HGRN_CHUNK)[None, :]
    mats = []
    for upper in (False, True):
        for e in (1, 2):
            same = (t >> e) == (s >> e)
            mats.append(same & ((s > t) if upper else (s <= t)))
    mats.append(s <= t)
    return jnp.asarray(np.concatenate(mats, axis=0).astype(np.float32), dtype=BF16)


def _hgrn_prompt_body(*refs, layer):
    npair = HGRN_HEADS // 2
    q_refs, f_refs, i_refs, g_refs = (refs[i * npair:(i + 1) * npair] for i in range(4))
    lg_ref, nw_ref, cm_ref, o_ref, sf_ref, s_ref = refs[4 * npair:]
    c = pl.program_id(1)

    @pl.when(c == 0)
    def _():
        s_ref[...] = jnp.zeros_like(s_ref)

    n = HGRN_CHUNK
    heads = range(HGRN_HEADS)
    col = lambda rs, h: rs[h // 2][:, (h % 2) * HGRN_D:(h % 2 + 1) * HGRN_D]
    hsl = lambda h: slice(h * HGRN_D, (h + 1) * HGRN_D)
    t = lax.broadcasted_iota(jnp.int32, (n, n), 0)
    s_ = lax.broadcasted_iota(jnp.int32, (n, n), 1)
    lb_all = _hgrn_lower_bound(lg_ref[...], layer)
    cm = cm_ref[...]
    ones = jnp.ones((n, n), BF16)
    fgs = [lb_all[:, hsl(h)] + (1.0 - lb_all[:, hsl(h)]) * jax.nn.sigmoid(col(f_refs, h))
           for h in heads]
    lfs = [jnp.log(fg) for fg in fgs]
    css = [_split_dot_left(cm, lf) for lf in lfs]
    bs = [cs[4 * n:5 * n] for cs in css]

    def cl(h, e):
        if e <= 2:
            return css[h][(e - 1) * n:e * n]
        if e == HGRN_LEVELS:
            return bs[h]
        m = 1 << e
        b3 = bs[h].reshape(n // m, m, HGRN_D)
        ends = jnp.broadcast_to(b3[:, m - 1:m, :], b3.shape)
        before = jnp.concatenate([jnp.zeros_like(ends[:1]), ends[:-1]], axis=0)
        return (b3 - before).reshape(n, HGRN_D)

    def cu(h, e):
        if e <= 2:
            return css[h][(2 + e - 1) * n:(2 + e) * n]
        if e == HGRN_LEVELS:
            return bs[h][n - 1:n] - bs[h]
        m = 1 << e
        b3 = bs[h].reshape(n // m, m, HGRN_D)
        return (jnp.broadcast_to(b3[:, m - 1:m, :], b3.shape) - b3).reshape(n, HGRN_D)
    qs = [col(q_refs, h) * jax.nn.sigmoid(col(q_refs, h)) for h in heads]
    ks = [1.0 - fg for fg in fgs]
    vbs = [col(i_refs, h).astype(BF16) for h in heads]
    acc = [jnp.where(t == s_, _dot_nt(qs[h].astype(BF16), ks[h].astype(BF16)), 0.0) for h in heads]
    for e in range(HGRN_LEVELS):
        mask = (((t >> e) & 1) == 1) & ((s_ >> e) == ((t >> e) - 1))
        for h in heads:
            qe = qs[h] * (fgs[h] if e == 0 else jnp.exp(cl(h, e)))
            ke = ks[h] if e == 0 else ks[h] * jnp.exp(cu(h, e))
            acc[h] = acc[h] + jnp.where(mask, _dot_nt(qe.astype(BF16), ke.astype(BF16)), 0.0)
    sts = [s_ref[h] for h in heads]
    os_ = [_dot(acc[h].astype(BF16), vbs[h])
           + _dot((qs[h] * jnp.exp(cl(h, HGRN_LEVELS))).astype(BF16), sts[h].astype(BF16))
           for h in heads]
    dks = [jnp.exp(_split_dot(lfs[h].T, ones)) for h in heads]
    for h in heads:
        kf = ks[h] * jnp.exp(cu(h, HGRN_LEVELS))
        s_ref[h] = dks[h] * sts[h] + _dot(kf.T.astype(BF16), vbs[h])
    for h in heads:
        o = os_[h]
        ms = jnp.mean(o * o, axis=-1, keepdims=True)
        gr = col(g_refs, h)
        o_ref[:, hsl(h)] = (o * lax.rsqrt(ms + NORM_EPS) * nw_ref[:, hsl(h)]) * (gr * jax.nn.sigmoid(gr))

    @pl.when(c == pl.num_programs(1) - 1)
    def _():
        sf_ref[...] = s_ref[...]


def _split_dot_left(m, x):
    hi, lo = _split(x)
    return _dot(m, hi) + _dot(m, lo)


def _hgrn_prompt(proj, logits, nw, cmat, nseq, seq_len, layer):
    n = proj.shape[0]
    nch = seq_len // HGRN_CHUNK
    wb = 2 * HGRN_D
    npair = HGRN_HEADS // 2
    cols = [pl.BlockSpec((HGRN_CHUNK, wb), lambda b, c, j=(HGRN_COL0 + g * HGRN_W) // wb + p:
                         (b * nch + c, j)) for g in range(4) for p in range(npair)]
    return pl.pallas_call(
        functools.partial(_hgrn_prompt_body, layer=layer),
        grid=(nseq, nch),
        in_specs=cols + [pl.BlockSpec(logits.shape, lambda b, c: (0, 0)),
                         pl.BlockSpec((1, HGRN_W), lambda b, c: (0, 0)),
                         pl.BlockSpec(cmat.shape, lambda b, c: (0, 0))],
        out_specs=[pl.BlockSpec((HGRN_CHUNK, HGRN_W), lambda b, c: (b * nch + c, 0)),
                   pl.BlockSpec((None, HGRN_HEADS, HGRN_D, HGRN_D), lambda b, c: (b, 0, 0, 0))],
        out_shape=[jax.ShapeDtypeStruct((n, HGRN_W), F32),
                   jax.ShapeDtypeStruct((nseq, HGRN_HEADS, HGRN_D, HGRN_D), F32)],
        scratch_shapes=[pltpu.VMEM((HGRN_HEADS, HGRN_D, HGRN_D), F32)],
        compiler_params=_cp(2), name="hgrn_prompt")(*([proj] * (4 * npair)), logits, nw, cmat)


HGRN_KT = 32


def _hgrn_lanes_body(q_ref, f_ref, i_ref, g_ref, lg_ref, nw_ref, s0_ref, o_ref, sf_ref,
                     s_ref, qt_ref, ft_ref, kt_ref, vt_ref, oacc_ref, *, layer):
    kt = pl.program_id(1)
    ninst = q_ref.shape[1]
    lb = _hgrn_lower_bound(lg_ref[...], layer)
    s_ref[...] = s0_ref[0]

    @pl.when(kt == 0)
    def _():
        oacc_ref[...] = jnp.zeros_like(oacc_ref)
        for tl in range(SUBLANES):
            qr = q_ref[tl]
            fg = lb + (1.0 - lb) * jax.nn.sigmoid(f_ref[tl])
            qt_ref[tl] = (qr * jax.nn.sigmoid(qr)).T
            ft_ref[tl] = fg.T
            kt_ref[tl] = (1.0 - fg).T
            vt_ref[tl] = i_ref[tl].T

    def step(tl, carry):
        vt = vt_ref[tl]

        def krow(kk, o):
            row = kt * HGRN_KT + kk
            s = ft_ref[tl, pl.ds(row, 1), :] * s_ref[kk] + kt_ref[tl, pl.ds(row, 1), :] * vt
            s_ref[kk] = s
            return o + qt_ref[tl, pl.ds(row, 1), :] * s

        oacc_ref[tl] = lax.fori_loop(0, HGRN_KT, krow, oacc_ref[tl], unroll=2)
        return carry

    lax.fori_loop(0, SUBLANES, step, 0)
    sf_ref[0] = s_ref[...]

    @pl.when(kt == pl.num_programs(1) - 1)
    def _():
        for tl in range(SUBLANES):
            o = oacc_ref[tl].T
            ms = jnp.mean(o * o, axis=-1, keepdims=True)
            gr = g_ref[tl]
            o_ref[tl] = (o * lax.rsqrt(ms + NORM_EPS) * nw_ref[...]) * (gr * jax.nn.sigmoid(gr))


def _hgrn_lanes(proj_t, logits, nw, s0, layer):
    steps, ninst, _ = proj_t.shape
    col = lambda off: pl.BlockSpec((steps, ninst, HGRN_D),
                                   lambda h, k, off=off: (0, 0, off // HGRN_D + h))
    sspec = pl.BlockSpec((1, HGRN_KT, HGRN_D, ninst), lambda h, k: (h, k, 0, 0))
    tbuf = pltpu.VMEM((steps, HGRN_D, ninst), F32)
    return pl.pallas_call(
        functools.partial(_hgrn_lanes_body, layer=layer),
        grid=(HGRN_HEADS, HGRN_D // HGRN_KT),
        in_specs=[col(0), col(HGRN_W), col(2 * HGRN_W), col(3 * HGRN_W),
                  pl.BlockSpec((logits.shape[0], HGRN_D), lambda h, k: (0, h)),
                  pl.BlockSpec((1, HGRN_D), lambda h, k: (0, h)), sspec],
        out_specs=[pl.BlockSpec((steps, ninst, HGRN_D), lambda h, k: (0, 0, h)), sspec],
        out_shape=[jax.ShapeDtypeStruct((steps, ninst, HGRN_W), F32),
                   jax.ShapeDtypeStruct(s0.shape, F32)],
        scratch_shapes=[pltpu.VMEM((HGRN_KT, HGRN_D, ninst), F32), tbuf, tbuf, tbuf, tbuf, tbuf],
        compiler_params=_cp(2), name="hgrn_lanes")(
            proj_t, proj_t, proj_t, proj_t, logits, nw, s0)


def _rglru_body(xb_ref, gate_ref, chist_ref, hinit_ref, cw_ref, cb_ref, wa_ref, ba_ref, wx_ref,
                bx_ref, lam_ref, y_ref, cst_ref, hst_ref, ccarry_ref, hcarry_ref,
                *, long_seq, tps):
    x = xb_ref[...]
    tm, c = x.shape
    if long_seq:
        @pl.when(pl.program_id(0) % tps == 0)
        def _():
            ccarry_ref[...] = chist_ref[...]
            hcarry_ref[...] = hinit_ref[...]
        hist = ccarry_ref[...]
    else:
        hist = chist_ref[...]
    p1, p2, p3 = _prev_rows(x, hist, (1, 2, 3), long_seq)
    cw = cw_ref[...]
    xc = cb_ref[...] + cw[0:1] * p3 + cw[1:2] * p2 + cw[2:3] * p1 + cw[3:4] * x
    xcb = xc.astype(BF16)
    r = jax.nn.sigmoid(_dot(xcb, wa_ref[...]) + ba_ref[...])
    ig = jax.nn.sigmoid(_dot(xcb, wx_ref[...]) + bx_ref[...])
    log_a = (-LRU_C) * r * _softplus(-lam_ref[...])
    a = jnp.exp(log_a)
    h = jnp.sqrt(-_expm1(2.0 * log_a)) * (ig * xc)
    if long_seq:
        pos = lax.broadcasted_iota(jnp.int32, (tm, c), 0)
        k = 1
        while k < tm:
            keep = pos >= k
            h = h + a * jnp.where(keep, pltpu.roll(h, k, 0), 0.0)
            a = a * jnp.where(keep, pltpu.roll(a, k, 0), 1.0)
            k *= 2
        h = h + a * hcarry_ref[SUBLANES - 1:SUBLANES, :]
        hcarry_ref[...] = h[tm - SUBLANES:]
        ccarry_ref[...] = x[tm - SUBLANES:]
        cst_ref[...] = x[tm - SUBLANES:]
        hst_ref[...] = h[tm - SUBLANES:]
    else:
        shp = (tm // SUBLANES, SUBLANES, c)
        h3, a3 = h.reshape(shp), a.reshape(shp)
        pos = lax.broadcasted_iota(jnp.int32, shp, 1)
        k = 1
        while k < SUBLANES:
            keep = pos >= k
            h3 = h3 + a3 * jnp.where(keep, pltpu.roll(h3, k, 1), 0.0)
            a3 = a3 * jnp.where(keep, pltpu.roll(a3, k, 1), 1.0)
            k *= 2
        h = (h3 + a3 * hinit_ref[...].reshape(shp)).reshape(tm, c)
        cst_ref[...] = x
        hst_ref[...] = h
    y_ref[...] = h * jax.nn.gelu(gate_ref[...])


def _rglru(proj, chist, hinit, lp, seq_len, tm):
    n = proj.shape[0]
    long_seq = seq_len > SUBLANES
    tps = max(seq_len // tm, 1)
    c0 = LRU_COL0 // LRU_W
    if long_seq:
        hs = pl.BlockSpec((SUBLANES, LRU_W), lambda i: (i // tps, 0))
    else:
        hs = pl.BlockSpec((tm, LRU_W), lambda i: (i, 0))
    row = pl.BlockSpec((1, LRU_W), lambda i: (0, 0))
    sq = pl.BlockSpec((LRU_W, LRU_W), lambda i: (0, 0))
    return pl.pallas_call(
        functools.partial(_rglru_body, long_seq=long_seq, tps=tps), grid=(n // tm,),
        in_specs=[pl.BlockSpec((tm, LRU_W), lambda i: (i, c0)),
                  pl.BlockSpec((tm, LRU_W), lambda i: (i, c0 + 1)), hs, hs,
                  pl.BlockSpec((4, LRU_W), lambda i: (0, 0)), row, sq, row, sq, row, row],
        out_specs=[pl.BlockSpec((tm, LRU_W), lambda i: (i, 0)), hs, hs],
        out_shape=[jax.ShapeDtypeStruct((n, LRU_W), F32),
                   jax.ShapeDtypeStruct(chist.shape, F32),
                   jax.ShapeDtypeStruct(chist.shape, F32)],
        scratch_shapes=[pltpu.VMEM((SUBLANES, LRU_W), F32), pltpu.VMEM((SUBLANES, LRU_W), F32)],
        compiler_params=_cp(1), name="rglru")(
            proj, proj, chist, hinit, lp["lru_cw"], lp["lru_cb"], lp["lru_wa"], lp["lru_ba"],
            lp["lru_wx"], lp["lru_bx"], lp["lru_lam"])


def _hist(state):
    nseq, k, c = state.shape
    return jnp.pad(state, ((0, 0), (SUBLANES - k, 0), (0, 0))).reshape(nseq * SUBLANES, c)


def _block_diag(w):
    h, a, b = w.shape
    eye = jnp.eye(h, dtype=w.dtype)
    return (eye[:, None, :, None] * w[:, :, None, :]).reshape(h * a, h * b)


def _layer(x3, st, lp, consts, layer, final_nw):
    nseq, seq_len, d = x3.shape
    n = nseq * seq_len
    long_seq = seq_len > SUBLANES
    s_rw, shift_rw, s_hg, h_lru, buf_lru, buf_ffn = st
    x = x3.reshape(n, d)
    ones = consts["ones"]

    proj = _in_proj(x, lp["norm_mix"], consts["w_in"], layer, tm=min(n, 1024), tn=IN_COLS // 4)

    tm_pre = min(n, seq_len if long_seq else n, 256)
    outs = _rwkv_pre(proj, _hist(shift_rw[:, None, :]), lp, ones, seq_len, tm_pre)
    r, w, k, v, na, kb, g, bonus, shift_out = outs
    new_shift = shift_out.reshape(nseq, SUBLANES, RWKV_COLS)[:, SUBLANES - 1]
    if long_seq:
        c = RWKV_CHUNK
        n4 = _rwkv_n(w, na, kb, consts["lmat2"])
        t4 = _tri_solve(n4.transpose(0, 2, 3, 1)).transpose(0, 3, 1, 2)
        y, spt = _rwkv_chunk((r, w, k, v, na, kb), t4, consts["lmat"], nseq)
        sp = spt.reshape(nseq, RWKV_HEADS // 2, 2, RWKV_N, 2, RWKV_N)
        new_s_rw = jnp.stack([sp[:, :, 0, :, 0, :], sp[:, :, 1, :, 1, :]], axis=2)
        new_s_rw = new_s_rw.reshape(nseq, RWKV_HEADS, RWKV_N, RWKV_N).transpose(0, 1, 3, 2)
    else:
        tmaj = lambda t: t.reshape(nseq, seq_len, RWKV_W).transpose(1, 0, 2)
        s0 = s_rw.transpose(1, 2, 3, 0)
        yb, sf = _rwkv_lanes([tmaj(t) for t in (r, w, k, v, na, kb)], s0)
        y = yb.transpose(1, 0, 2).reshape(n, RWKV_W)
        new_s_rw = sf.transpose(3, 0, 1, 2)
    y_rw = _rwkv_post(y, bonus, g, lp["ln_w"], lp["ln_b"], ones, tm=min(n, 512))

    if long_seq:
        y_hg, new_s_hg = _hgrn_prompt(proj, consts["lb_logits"], lp["hgrn_nw"], consts["cmat"],
                                      nseq, seq_len, layer)
    else:
        proj_t = proj[:, HGRN_COL0:LRU_COL0].reshape(nseq, seq_len, 4 * HGRN_W).transpose(1, 0, 2)
        s0 = s_hg.transpose(1, 2, 3, 0)
        o_t, sf = _hgrn_lanes(proj_t, consts["lb_logits"], lp["hgrn_nw"], s0, layer)
        y_hg = o_t.transpose(1, 0, 2).reshape(n, HGRN_W)
        new_s_hg = sf.transpose(3, 0, 1, 2)

    if long_seq:
        hinit = _hist(h_lru[:, None, :])
        tm_lru = min(seq_len, 256)
    else:
        hinit = jnp.broadcast_to(h_lru[:, None, :], (nseq, seq_len, LRU_W)).reshape(n, LRU_W)
        tm_lru = min(n, 512)
    y_lru, cst, hst = _rglru(proj, _hist(buf_lru), hinit, lp, seq_len, tm_lru)
    new_buf_lru = cst.reshape(nseq, SUBLANES, LRU_W)[:, SUBLANES - 3:]
    new_h_lru = hst.reshape(nseq, SUBLANES, LRU_W)[:, SUBLANES - 1]

    x = _out_proj(x, y_rw, y_hg, y_lru, consts["w_out"], layer, tm=min(n, 1024), tn=512)
    tm_ffn = min(n, seq_len if long_seq else n, 1024)
    outs = _ffn(x, lp["norm_ffn"], consts["w_up"], lp["ffn_cw"], lp["ffn_cb"], consts["w_down"],
                _hist(buf_ffn), layer, seq_len, tm=tm_ffn, tf=512, final_nw=final_nw)
    x, fst = outs
    new_buf_ffn = fst.reshape(nseq, -1, SUBLANES, D_FF)[:, -1, SUBLANES - 2:]
    return x.reshape(nseq, seq_len, d), (new_s_rw, new_shift, new_s_hg, new_h_lru, new_buf_lru,
                                         new_buf_ffn)


def kernel(x_prompt, x_sample, state_rwkv, state_rwkv_shift, state_hgrn, state_rglru, cache_rglru_conv, cache_ffn_conv, norm_mix, w_in, rwkv_mu, rwkv_w0, rwkv_w2, rwkv_a0, rwkv_a2, rwkv_g2, rwkv_k_k, rwkv_k_a, rwkv_r_k, rwkv_ln_w, rwkv_ln_b, hgrn_lb_logits, hgrn_norm_w, rglru_conv_w, rglru_conv_b, rglru_wa, rglru_ba, rglru_wx, rglru_bx, rglru_lambda, w_out, norm_ffn, ffn_w_up, ffn_conv_w, ffn_conv_b, ffn_w_down, norm_final):
    depth = w_in.shape[0]
    nb, nt, _ = x_prompt.shape
    db = x_sample.shape[0]
    consts = {
        "ones": jnp.asarray(np.kron(np.eye(RWKV_HEADS), np.ones((RWKV_N, RWKV_N))), dtype=BF16),
        "cmat": _hgrn_level_matrix(),
        "lmat": jnp.asarray(np.tril(np.ones((RWKV_CHUNK, RWKV_CHUNK), np.float32)), dtype=BF16),
        "lmat2": jnp.asarray(np.kron(np.eye(RWKV_N_CHUNKS), np.tril(np.ones((RWKV_CHUNK, RWKV_CHUNK)))),
                             dtype=BF16),
        "lb_logits": hgrn_lb_logits,
        "w_in": _cast_bf16(w_in, 256), "w_out": w_out.astype(BF16),
        "w_up": ffn_w_up.astype(BF16), "w_down": ffn_w_down.astype(BF16),
    }
    rowv = lambda a: a.reshape(1, -1)
    x_p, x_s = x_prompt, x_sample
    new_p, new_s = [], []
    for l in range(depth):
        lp = {
            "norm_mix": rowv(norm_mix[l]), "mu": rowv(rwkv_mu[l]),
            "w0": rowv(rwkv_w0[l]),
            "w2p": jnp.pad(rwkv_w2[l], ((0, 64), (0, 0))).astype(BF16),
            "a0": rowv(rwkv_a0[l]),
            "a2p": jnp.pad(rwkv_a2[l], ((64, 0), (0, 0))).astype(BF16),
            "g2": rwkv_g2[l].astype(BF16), "k_k": rowv(rwkv_k_k[l]), "k_a": rowv(rwkv_k_a[l]),
            "r_k": rowv(rwkv_r_k[l]), "ln_w": rowv(rwkv_ln_w[l]), "ln_b": rowv(rwkv_ln_b[l]),
            "hgrn_nw": rowv(hgrn_norm_w[l]),
            "lru_cw": rglru_conv_w[l], "lru_cb": rowv(rglru_conv_b[l]),
            "lru_wa": _block_diag(rglru_wa[l]).astype(BF16), "lru_ba": rowv(rglru_ba[l]),
            "lru_wx": _block_diag(rglru_wx[l]).astype(BF16), "lru_bx": rowv(rglru_bx[l]),
            "lru_lam": rowv(rglru_lambda[l]),
            "norm_ffn": rowv(norm_ffn[l]), "ffn_cw": ffn_conv_w[l],
            "ffn_cb": rowv(ffn_conv_b[l]),
        }
        zero = lambda *s: jnp.zeros(s, F32)
        st_p = (None, zero(nb, RWKV_COLS), None, zero(nb, LRU_W), zero(nb, 3, LRU_W),
                zero(nb, 2, D_FF))
        final_nw = rowv(norm_final) if l == depth - 1 else None
        x_p, sp = _layer(x_p, st_p, lp, consts, l, final_nw)
        st_s = (state_rwkv[l], state_rwkv_shift[l], state_hgrn[l], state_rglru[l],
                cache_rglru_conv[l], cache_ffn_conv[l])
        x_s, ss = _layer(x_s, st_s, lp, consts, l, final_nw)
        new_p.append(sp)
        new_s.append(ss)
    stack = lambda sts: [jnp.stack(s, axis=0) for s in zip(*sts)]
    return (x_p, x_s, *stack(new_p), *stack(new_s))
```

```python
import functools

import numpy as np
import jax
import jax.numpy as jnp
from jax import lax
from jax.experimental import pallas as pl
from jax.experimental.pallas import tpu as pltpu

F32 = jnp.float32
BF16 = jnp.bfloat16

D_MODEL = 2048
RWKV_HEADS = 12
RWKV_N = 64
RWKV_W = RWKV_HEADS * RWKV_N
RWKV_COLS = 2560
RWKV_LN_EPS = 64e-5
HGRN_HEADS = 6
HGRN_D = 128
HGRN_W = HGRN_HEADS * HGRN_D
HGRN_COL0 = RWKV_COLS
LRU_W = 512
LRU_COL0 = RWKV_COLS + 4 * HGRN_W
LRU_C = 8.0
IN_COLS = 6656
D_FF = 5632
NORM_EPS = 1e-6

SUBLANES = 8
LANES = 128
RWKV_CHUNK = 64
HGRN_CHUNK = 128
HGRN_LEVELS = 7
VMEM_LIMIT = 48 * 1024 * 1024
BIG_VMEM_LIMIT = 56 * 1024 * 1024


def _cp(n, limit=VMEM_LIMIT):
    return pltpu.CompilerParams(dimension_semantics=("arbitrary",) * n, vmem_limit_bytes=limit)


def _dot(a, b):
    return jnp.dot(a, b, preferred_element_type=F32)


def _dot_nt(a, b):
    return lax.dot_general(a, b, (((1,), (1,)), ((), ())), preferred_element_type=F32)


def _split(x):
    hi = x.astype(BF16)
    lo = (x - hi.astype(F32)).astype(BF16)
    return hi, lo


def _split_dot(x, m):
    hi, lo = _split(x)
    return _dot(hi, m) + _dot(lo, m)


def _dot3(a, b, nt=False):
    f = _dot_nt if nt else _dot
    ah, al = _split(a)
    bh, bl = _split(b)
    return f(ah, bh) + (f(ah, bl) + f(al, bh))


def _softplus(z):
    return jnp.maximum(z, 0.0) + jnp.log1p(jnp.exp(-jnp.abs(z)))


def _expm1(z):
    return jnp.tanh(0.5 * z) * (jnp.exp(z) + 1.0)


def _prev_rows(x, hist, ks, long_seq):
    tm, c = x.shape
    if long_seq:
        ext = jnp.concatenate([hist, x], axis=0)
        return [pltpu.roll(ext, k, 0)[SUBLANES:] for k in ks]
    x3 = x.reshape(tm // SUBLANES, SUBLANES, c)
    h3 = hist.reshape(tm // SUBLANES, SUBLANES, c)
    pos = lax.broadcasted_iota(jnp.int32, x3.shape, 1)
    return [jnp.where(pos >= k, pltpu.roll(x3, k, 1), pltpu.roll(h3, k, 1)).reshape(tm, c)
            for k in ks]


def _in_proj_body(x_ref, nw_ref, w_ref, o_ref, xn_ref):
    @pl.when(pl.program_id(1) == 0)
    def _():
        x = x_ref[...]
        ms = jnp.mean(x * x, axis=-1, keepdims=True)
        xn_ref[...] = (x * lax.rsqrt(ms + NORM_EPS) * nw_ref[...]).astype(BF16)
    o_ref[...] = _dot(xn_ref[...], w_ref[...])


def _in_proj(x, nw, w, layer, tm, tn):
    n, d = x.shape
    c = w.shape[2]
    return pl.pallas_call(
        _in_proj_body, grid=(n // tm, c // tn),
        in_specs=[pl.BlockSpec((tm, d), lambda i, j: (i, 0)),
                  pl.BlockSpec((1, d), lambda i, j: (0, 0)),
                  pl.BlockSpec((None, d, tn), lambda i, j: (layer, 0, j))],
        out_specs=pl.BlockSpec((tm, tn), lambda i, j: (i, j)),
        out_shape=jax.ShapeDtypeStruct((n, c), F32),
        scratch_shapes=[pltpu.VMEM((tm, d), BF16)],
        compiler_params=_cp(2, BIG_VMEM_LIMIT), name="in_proj")(x, nw, w)


def _out_proj_body(x_ref, ya_ref, yb_ref, yc_ref, w_ref, o_ref, y_ref):
    @pl.when(pl.program_id(1) == 0)
    def _():
        y_ref[:, 0:RWKV_W] = ya_ref[...].astype(BF16)
        y_ref[:, RWKV_W:RWKV_W + HGRN_W] = yb_ref[...].astype(BF16)
        y_ref[:, RWKV_W + HGRN_W:] = yc_ref[...].astype(BF16)
    o_ref[...] = x_ref[...] + _dot(y_ref[...], w_ref[...])


def _out_proj(x, ya, yb, yc, w, layer, tm, tn):
    n, d = x.shape
    return pl.pallas_call(
        _out_proj_body, grid=(n // tm, d // tn),
        in_specs=[pl.BlockSpec((tm, tn), lambda i, j: (i, j)),
                  pl.BlockSpec((tm, RWKV_W), lambda i, j: (i, 0)),
                  pl.BlockSpec((tm, HGRN_W), lambda i, j: (i, 0)),
                  pl.BlockSpec((tm, LRU_W), lambda i, j: (i, 0)),
                  pl.BlockSpec((None, d, tn), lambda i, j: (layer, 0, j))],
        out_specs=pl.BlockSpec((tm, tn), lambda i, j: (i, j)),
        out_shape=jax.ShapeDtypeStruct((n, d), F32),
        scratch_shapes=[pltpu.VMEM((tm, d), BF16)],
        compiler_params=_cp(2), name="out_proj")(x, ya, yb, yc, w)


def _ffn_body(*refs, long_seq, tps, final):
    x_ref, nw_ref, wg_ref, wv_ref, cw_ref, cb_ref, wd_ref, hist_ref = refs[:8]
    if final:
        fnw_ref, o_ref, st_ref, hn_ref, carry_ref = refs[8:]
    else:
        o_ref, st_ref, hn_ref, carry_ref = refs[8:]
    i = pl.program_id(0)
    j = pl.program_id(1)

    @pl.when(j == 0)
    def _():
        x = x_ref[...]
        ms = jnp.mean(x * x, axis=-1, keepdims=True)
        hn_ref[...] = (x * lax.rsqrt(ms + NORM_EPS) * nw_ref[...]).astype(BF16)
        o_ref[...] = x

    hn = hn_ref[...]
    if long_seq:
        @pl.when(i % tps == 0)
        def _():
            carry_ref[j] = hist_ref[...]
        hist = carry_ref[j]
    else:
        hist = hist_ref[...]
    tf = wg_ref.shape[1]
    halves = [slice(q * (tf // 2), (q + 1) * (tf // 2)) for q in range(2)]
    gs = [_dot(hn, wg_ref[:, sl]) for sl in halves]
    vs = [_dot(hn, wv_ref[:, sl]) for sl in halves]
    cw = cw_ref[...]
    acc = None
    for sl, g, v in zip(halves, gs, vs):
        p1, p2 = _prev_rows(g, hist[:, sl], (1, 2), long_seq)
        gc = cb_ref[:, sl] + cw[0:1, sl] * p2 + cw[1:2, sl] * p1 + cw[2:3, sl] * g
        h = (gc * jax.nn.sigmoid(gc)) * v
        d = _dot(h.astype(BF16), wd_ref[sl, :])
        acc = d if acc is None else acc + d
        tail = g if not long_seq else g[g.shape[0] - SUBLANES:]
        st_ref[:, sl] = tail
        if long_seq:
            carry_ref[j, :, sl] = tail
    o_ref[...] += acc

    if final:
        @pl.when(j == pl.num_programs(1) - 1)
        def _():
            y = o_ref[...]
            ms = jnp.mean(y * y, axis=-1, keepdims=True)
            o_ref[...] = y * lax.rsqrt(ms + NORM_EPS) * fnw_ref[...]


def _ffn(x, nw, w_up, cw, cb, w_down, hist, layer, seq_len, tm, tf, final_nw=None):
    n, d = x.shape
    final = final_nw is not None
    nf = D_FF // tf
    long_seq = seq_len > SUBLANES
    tps = max(seq_len // tm, 1)
    if long_seq:
        hist_spec = pl.BlockSpec((SUBLANES, tf), lambda i, j: (i // tps, j))
        st_spec = pl.BlockSpec((SUBLANES, tf), lambda i, j: (i, j))
        st_rows = n // tm * SUBLANES
    else:
        hist_spec = pl.BlockSpec((tm, tf), lambda i, j: (i, j))
        st_spec = pl.BlockSpec((tm, tf), lambda i, j: (i, j))
        st_rows = n
    body = functools.partial(_ffn_body, long_seq=long_seq, tps=tps, final=final)
    row_tile = pl.BlockSpec((tm, d), lambda i, j: (i, 0))
    wide = pl.BlockSpec((1, d), lambda i, j: (0, 0))
    in_specs = [row_tile, wide,
                pl.BlockSpec((None, d, tf), lambda i, j: (layer, 0, j)),
                pl.BlockSpec((None, d, tf), lambda i, j: (layer, 0, j + nf)),
                pl.BlockSpec((3, tf), lambda i, j: (0, j)),
                pl.BlockSpec((1, tf), lambda i, j: (0, j)),
                pl.BlockSpec((None, tf, d), lambda i, j: (layer, j, 0)),
                hist_spec]
    args = [x, nw, w_up, w_up, cw, cb, w_down, hist]
    out_specs = [row_tile, st_spec]
    out_shape = [jax.ShapeDtypeStruct((n, d), F32), jax.ShapeDtypeStruct((st_rows, D_FF), F32)]
    scratch = [pltpu.VMEM((tm, d), BF16), pltpu.VMEM((nf, SUBLANES, tf), F32)]
    if final:
        in_specs.append(wide)
        args.append(final_nw)
    return pl.pallas_call(
        body, grid=(n // tm, nf), in_specs=in_specs, out_specs=out_specs, out_shape=out_shape,
        scratch_shapes=scratch,
        compiler_params=pltpu.CompilerParams(dimension_semantics=("arbitrary",) * 2,
                                             vmem_limit_bytes=BIG_VMEM_LIMIT),
        name="ffn")(*args)


def _rwkv_pre_body(*refs, long_seq, tps, with_n):
    (p_ref, hist_ref, mu_ref, w0_ref, w2_ref, a0_ref, a2_ref, g2_ref, kk_ref, ka_ref, rk_ref,
     ones_ref) = refs[:12]
    refs = refs[12:]
    if with_n:
        l_ref, refs = refs[0], refs[1:]
    r_ref, w_ref, k_ref, v_ref, na_ref, kb_ref, g_ref, bonus_ref, st_ref = refs[:9]
    refs = refs[9:]
    if with_n:
        n_ref, refs = refs[0], refs[1:]
    (carry_ref,) = refs
    p = p_ref[...]
    if long_seq:
        @pl.when(pl.program_id(0) % tps == 0)
        def _():
            carry_ref[...] = hist_ref[...]
        hist = carry_ref[...]
    else:
        hist = hist_ref[...]
    (prev,) = _prev_rows(p, hist, (1,), long_seq)
    if long_seq:
        carry_ref[...] = p[p.shape[0] - SUBLANES:]
        st_ref[...] = p[p.shape[0] - SUBLANES:]
    else:
        st_ref[...] = p
    xs = p + (prev - p) * mu_ref[...]
    r = xs[:, 0:RWKV_W]
    k = xs[:, RWKV_W:2 * RWKV_W]
    v = xs[:, 2 * RWKV_W:3 * RWKV_W]
    xwa = xs[:, 3 * RWKV_W:3 * RWKV_W + 128]
    xg = xs[:, 3 * RWKV_W + 128:RWKV_COLS]
    zw = w0_ref[...] + _dot(jnp.tanh(xwa).astype(BF16), w2_ref[...])
    w_log = -_softplus(-zw) - 0.5
    decay = jnp.exp(-jnp.exp(w_log))
    a = jax.nn.sigmoid(a0_ref[...] + _dot(xwa.astype(BF16), a2_ref[...]))
    g = _dot(jax.nn.sigmoid(xg).astype(BF16), g2_ref[...])
    ones = ones_ref[...]
    kk = k * kk_ref[...]
    kk = kk / jnp.maximum(jnp.sqrt(_split_dot(kk * kk, ones)), 1e-12)
    kf = k * (1.0 + (a - 1.0) * ka_ref[...])
    r_ref[...] = r
    w_ref[...] = decay
    k_ref[...] = kf
    v_ref[...] = v
    na_ref[...] = -kk
    kb_ref[...] = kk * a
    g_ref[...] = g
    bonus_ref[...] = _split_dot(r * kf * rk_ref[...], ones) * v
    if with_n:
        c = RWKV_CHUNK
        pw = 2 * RWKV_N
        lw = jnp.log(decay)
        cw = _split_dot_left(l_ref[...], lw)
        at = _split(-kk * jnp.exp(cw - lw))
        bt = _split(kk * a * jnp.exp(-cw))
        strict = (lax.broadcasted_iota(jnp.int32, (c, c), 1)
                  < lax.broadcasted_iota(jnp.int32, (c, c), 0))
        lane = lax.broadcasted_iota(jnp.int32, (p.shape[0], pw), 1)
        for pr in range(RWKV_HEADS // 2):
            sl = slice(pr * pw, (pr + 1) * pw)
            for hh in range(2):
                m = (lane < RWKV_N) if hh == 0 else (lane >= RWKV_N)
                ah = jnp.where(m, at[0][:, sl], jnp.zeros_like(at[0][:, sl]))
                al = jnp.where(m, at[1][:, sl], jnp.zeros_like(at[1][:, sl]))
                for ci in range(p.shape[0] // c):
                    rs = slice(ci * c, (ci + 1) * c)
                    bh, bl = bt[0][rs, sl], bt[1][rs, sl]
                    nm = _dot_nt(ah[rs], bh) + (_dot_nt(ah[rs], bl) + _dot_nt(al[rs], bh))
                    n_ref[2 * pr + hh, ci] = jnp.where(strict, nm, 0.0)


def _rwkv_pre(proj, hist, lp, ones, seq_len, tm, lmat=None):
    n = proj.shape[0]
    long_seq = seq_len > SUBLANES
    tps = max(seq_len // tm, 1)
    with_n = lmat is not None
    row = lambda c: pl.BlockSpec((1, c), lambda i: (0, 0))
    full = lambda a, b: pl.BlockSpec((a, b), lambda i: (0, 0))
    if long_seq:
        hist_spec = pl.BlockSpec((SUBLANES, RWKV_COLS), lambda i: (i // tps, 0))
    else:
        hist_spec = pl.BlockSpec((tm, RWKV_COLS), lambda i: (i, 0))
    tile = pl.BlockSpec((tm, RWKV_W), lambda i: (i, 0))
    body = functools.partial(_rwkv_pre_body, long_seq=long_seq, tps=tps, with_n=with_n)
    in_specs = [pl.BlockSpec((tm, RWKV_COLS), lambda i: (i, 0)), hist_spec,
                row(RWKV_COLS), row(RWKV_W), full(128, RWKV_W), row(RWKV_W),
                full(128, RWKV_W), full(128, RWKV_W), row(RWKV_W), row(RWKV_W), row(RWKV_W),
                full(RWKV_W, RWKV_W)]
    args = [proj, hist, lp["mu"], lp["w0"], lp["w2p"], lp["a0"], lp["a2p"], lp["g2"],
            lp["k_k"], lp["k_a"], lp["r_k"], ones]
    out_specs = [tile] * 8 + [hist_spec]
    out_shape = [jax.ShapeDtypeStruct((n, RWKV_W), F32)] * 8 + [jax.ShapeDtypeStruct(hist.shape, F32)]
    if with_n:
        c = RWKV_CHUNK
        in_specs.append(full(tm, tm))
        args.append(lmat)
        out_specs.append(pl.BlockSpec((RWKV_HEADS, tm // c, c, c), lambda i: (0, i, 0, 0)))
        out_shape.append(jax.ShapeDtypeStruct((RWKV_HEADS, n // c, c, c), F32))
    return pl.pallas_call(
        body, grid=(n // tm,), in_specs=in_specs, out_specs=out_specs, out_shape=out_shape,
        scratch_shapes=[pltpu.VMEM((SUBLANES, RWKV_COLS), F32)],
        compiler_params=_cp(1), name="rwkv_pre")(*args)


def _rwkv_post_body(y_ref, bonus_ref, g_ref, lw_ref, lb_ref, ones_ref, o_ref):
    y = y_ref[...]
    ones = ones_ref[...]
    mean = _split_dot(y, ones) * (1.0 / RWKV_N)
    d = y - mean
    var = _split_dot(d * d, ones) * (1.0 / RWKV_N)
    yn = d * lax.rsqrt(var + RWKV_LN_EPS)
    o_ref[...] = (yn * lw_ref[...] + lb_ref[...] + bonus_ref[...]) * g_ref[...]


def _rwkv_post(y, bonus, g, lw, lb, ones, tm):
    n = y.shape[0]
    tile = pl.BlockSpec((tm, RWKV_W), lambda i: (i, 0))
    row = pl.BlockSpec((1, RWKV_W), lambda i: (0, 0))
    return pl.pallas_call(
        _rwkv_post_body, grid=(n // tm,),
        in_specs=[tile, tile, tile, row, row, pl.BlockSpec((RWKV_W, RWKV_W), lambda i: (0, 0))],
        out_specs=tile, out_shape=jax.ShapeDtypeStruct((n, RWKV_W), F32),
        compiler_params=_cp(1), name="rwkv_post")(y, bonus, g, lw, lb, ones)


def _rwkv_lanes_body(r_ref, w_ref, k_ref, v_ref, a_ref, b_ref, s0_ref, y_ref, sf_ref,
                     s_ref, xt_ref, y_buf):
    tb = pl.program_id(1)
    ninst = r_ref.shape[1]

    @pl.when(tb == 0)
    def _():
        for hh in range(2):
            s_ref[hh] = s0_ref[hh]

    def step(tl, carry):
        for idx, ref in enumerate((r_ref, w_ref, k_ref, v_ref, a_ref, b_ref)):
            xt_ref[idx] = ref[tl].T
        for hh in range(2):
            base = hh * RWKV_N
            heads = lambda i: xt_ref[i, base:base + RWKV_N, :]

            def row(rho, c):
                s = s_ref[hh, rho]
                sa = jnp.sum(s * heads(4), axis=0, keepdims=True)
                vrow = xt_ref[3, pl.ds(base + rho, 1), :]
                s = s * heads(1) + sa * heads(5) + vrow * heads(2)
                s_ref[hh, rho] = s
                y_buf[pl.ds(base + rho, 1), :] = jnp.sum(s * heads(0), axis=0, keepdims=True)
                return c

            lax.fori_loop(0, RWKV_N, row, 0, unroll=8)
        y_ref[tl] = y_buf[...].T
        return carry

    lax.fori_loop(0, SUBLANES, step, 0)

    @pl.when(tb == pl.num_programs(1) - 1)
    def _():
        for hh in range(2):
            sf_ref[hh] = s_ref[hh]


def _rwkv_lanes(xs, s0):
    steps, ninst, _ = xs[0].shape
    xspec = pl.BlockSpec((SUBLANES, ninst, 2 * RWKV_N), lambda h, t: (t, 0, h))
    sspec = pl.BlockSpec((2, RWKV_N, RWKV_N, ninst), lambda h, t: (h, 0, 0, 0))
    return pl.pallas_call(
        _rwkv_lanes_body, grid=(RWKV_HEADS // 2, steps // SUBLANES),
        in_specs=[xspec] * 6 + [sspec], out_specs=[xspec, sspec],
        out_shape=[jax.ShapeDtypeStruct((steps, ninst, RWKV_W), F32),
                   jax.ShapeDtypeStruct(s0.shape, F32)],
        scratch_shapes=[pltpu.VMEM((2, RWKV_N, RWKV_N, ninst), F32),
                        pltpu.VMEM((6, 2 * RWKV_N, ninst), F32),
                        pltpu.VMEM((2 * RWKV_N, ninst), F32)],
        compiler_params=_cp(2), name="rwkv_lanes")(*xs, s0)


RWKV_C_CHUNKS = 2


def _tri_solve_body(nt_ref, tt_ref):
    c = RWKV_CHUNK
    ninst = nt_ref.shape[2]
    tt_ref[...] = jnp.zeros_like(tt_ref)
    g = SUBLANES
    nb = 4
    sub = lax.broadcasted_iota(jnp.int32, (g, ninst), 0)
    for t0 in range(0, c, nb):
        ngrp = (t0 + nb - 1) // g + 1
        rs = [[jnp.where(sub + g * j == t0 + i, 1.0, 0.0).astype(F32) for j in range(ngrp)]
              for i in range(nb)]
        for sg in range(-(-t0 // g)):

            def acc(s, carry, sg=sg, t0=t0):
                ts = [tt_ref[s, j * g:(j + 1) * g, :] for j in range(sg + 1)]
                nrow = [nt_ref[t0 + i, pl.ds(s, 1), :] for i in range(nb)]
                return tuple(tuple(carry[i][j] + nrow[i] * ts[j] for j in range(sg + 1))
                             for i in range(nb))

            res = lax.fori_loop(sg * g, min(sg * g + g, t0), acc,
                                tuple(tuple(rs[i][:sg + 1]) for i in range(nb)))
            for i in range(nb):
                rs[i][:sg + 1] = list(res[i])
        for i in range(nb):
            for j in range(i):
                nij = nt_ref[t0 + i, t0 + j:t0 + j + 1, :]
                rs[i] = [x + nij * y for x, y in zip(rs[i], rs[j])]
            tt_ref[t0 + i, 0:ngrp * g, :] = jnp.concatenate(rs[i], axis=0)


def _tri_solve(nt):
    nh, c, _, ninst = nt.shape
    spec = pl.BlockSpec((None, c, c, ninst), lambda h: (h, 0, 0, 0))
    return pl.pallas_call(
        _tri_solve_body, grid=(nh,), in_specs=[spec], out_specs=spec,
        out_shape=jax.ShapeDtypeStruct(nt.shape, F32),
        compiler_params=_cp(1), name="tri_solve")(nt)


def _rwkv_chunk_body(r_ref, w_ref, k_ref, v_ref, a_ref, b_ref, t_ref, l_ref, y_ref, sf_ref, s_ref):
    c = RWKV_CHUNK
    pw = 2 * RWKV_N
    ci = pl.program_id(1)

    @pl.when(ci == 0)
    def _():
        s_ref[...] = jnp.zeros_like(s_ref)

    lmat = l_ref[...]
    ti = lax.broadcasted_iota(jnp.int32, (c, c), 0)
    si = lax.broadcasted_iota(jnp.int32, (c, c), 1)
    strict, incl = si < ti, si <= ti
    lane = lax.broadcasted_iota(jnp.int32, (c, pw), 1)
    lane2 = lax.broadcasted_iota(jnp.int32, (2 * c, pw), 1)
    r2 = lax.broadcasted_iota(jnp.int32, (pw, pw), 0)
    c2 = lax.broadcasted_iota(jnp.int32, (pw, pw), 1)
    same_head = (r2 < RWKV_N) == (c2 < RWKV_N)
    eye = r2 == c2
    bz = lambda x: jnp.zeros_like(x)
    pairs = range(RWKV_HEADS // 2)
    units = [(q, p) for q in range(RWKV_C_CHUNKS) for p in pairs]
    at = lambda ref, u: ref[u[0] * c:(u[0] + 1) * c, u[1] * pw:(u[1] + 1) * pw]
    halves = [(u, hh) for u in units for hh in range(2)]
    lws = {u: jnp.log(at(w_ref, u)) for u in units}
    cws = {u: _split_dot_left(lmat, lws[u]) for u in units}
    vs = {u: at(v_ref, u) for u in units}
    wts = {u: jnp.exp(cws[u]) for u in units}
    rts = {u: at(r_ref, u) * wts[u] for u in units}
    ats = {u: at(a_ref, u) * jnp.exp(cws[u] - lws[u]) for u in units}
    lhs = {u: _split(jnp.concatenate([ats[u], rts[u]], axis=0)) for u in units}
    bts = {u: _split(at(b_ref, u) * jnp.exp(-cws[u])) for u in units}
    kts = {u: _split(at(k_ref, u) * jnp.exp(-cws[u])) for u in units}
    a_ak, a_rk, a_rb = {}, {}, {}
    for u, hh in halves:
        m2 = (lane2 < RWKV_N) if hh == 0 else (lane2 >= RWKV_N)
        lh = jnp.where(m2, lhs[u][0], bz(lhs[u][0]))
        ll = jnp.where(m2, lhs[u][1], bz(lhs[u][1]))
        pak = (_dot_nt(lh[:c], kts[u][0])
               + (_dot_nt(lh[:c], kts[u][1]) + _dot_nt(ll[:c], kts[u][0])))
        a_ak[u, hh] = jnp.where(strict, pak, 0.0)
        a_rk[u, hh] = jnp.where(incl, _dot_nt(lh[c:], kts[u][0]), 0.0).astype(BF16)
        a_rb[u, hh] = jnp.where(incl, _dot_nt(lh[c:], bts[u][0]), 0.0).astype(BF16)
    vps = {uh: _dot3(a_ak[uh], vs[uh[0]]) for uh in halves}
    gs = {(u, hh): _dot3(t_ref[2 * u[1] + hh, u[0]], jnp.concatenate([ats[u], vps[u, hh]], axis=1))
          for u, hh in halves}
    pick = lambda x0, x1: jnp.where(lane < RWKV_N, x0, x1)
    ahats = {u: pick(gs[u, 0][:, :pw], gs[u, 1][:, :pw]) for u in units}
    vhats = {u: pick(gs[u, 0][:, pw:], gs[u, 1][:, pw:]) for u in units}
    ahb = {u: ahats[u].astype(BF16) for u in units}
    vhb = {u: vhats[u].astype(BF16) for u in units}
    vbs = {u: vs[u].astype(BF16) for u in units}
    rhats = {u: (rts[u] + pick(_dot(a_rb[u, 0], ahb[u]), _dot(a_rb[u, 1], ahb[u]))).astype(BF16)
             for u in units}
    yhats = {u: pick(_dot(a_rb[u, 0], vhb[u]) + _dot(a_rk[u, 0], vbs[u]),
                     _dot(a_rb[u, 1], vhb[u]) + _dot(a_rk[u, 1], vbs[u])) for u in units}
    wends = {u: jnp.exp(cws[u][c - 1:c] - cws[u]) for u in units}
    bkts = {u: jnp.concatenate([at(b_ref, u) * wends[u], at(k_ref, u) * wends[u]], axis=0).T
            for u in units}
    mpts = {u: jnp.where(same_head, _dot3(bkts[u], jnp.concatenate([ahats[u], bz(ahats[u])], axis=0)), 0.0)
            + jnp.where(eye, wts[u][c - 1:c], 0.0) for u in units}
    zpts = {u: jnp.where(same_head, _dot3(bkts[u], jnp.concatenate([vhats[u], vs[u]], axis=0)), 0.0)
            for u in units}
    spts = [s_ref[p] for p in pairs]
    for q in range(RWKV_C_CHUNKS):
        for p in pairs:
            y_ref[q * c:(q + 1) * c, p * pw:(p + 1) * pw] = (
                _dot(rhats[q, p], spts[p].astype(BF16)) + yhats[q, p])
        spts = [_dot3(mpts[q, p], spts[p]) + zpts[q, p] for p in pairs]
    for p in pairs:
        s_ref[p] = spts[p]

    @pl.when(ci == pl.num_programs(1) - 1)
    def _():
        sf_ref[...] = s_ref[...]


def _rwkv_chunk(xs, t4, lmat, nseq):
    n = xs[0].shape[0]
    c = RWKV_CHUNK
    q = RWKV_C_CHUNKS
    nch = n // (q * c) // nseq
    pw = 2 * RWKV_N
    tile = pl.BlockSpec((q * c, RWKV_W), lambda b, i: (b * nch + i, 0))
    return pl.pallas_call(
        _rwkv_chunk_body, grid=(nseq, nch),
        in_specs=[tile] * 6 + [pl.BlockSpec((RWKV_HEADS, q, c, c), lambda b, i: (0, b * nch + i, 0, 0)),
                               pl.BlockSpec((c, c), lambda b, i: (0, 0))],
        out_specs=[tile, pl.BlockSpec((None, RWKV_HEADS // 2, pw, pw), lambda b, i: (b, 0, 0, 0))],
        out_shape=[jax.ShapeDtypeStruct((n, RWKV_W), F32),
                   jax.ShapeDtypeStruct((nseq, RWKV_HEADS // 2, pw, pw), F32)],
        scratch_shapes=[pltpu.VMEM((RWKV_HEADS // 2, pw, pw), F32)],
        compiler_params=_cp(2), name="rwkv_chunk")(*xs, t4, lmat)


def _hgrn_lower_bound(logits, layer):
    m = jnp.max(logits, axis=0, keepdims=True)
    e = jnp.exp(logits - m)
    gam = e / jnp.sum(e, axis=0, keepdims=True)
    cs = gam[0:1]
    for i in range(1, layer + 1):
        cs = cs + gam[i:i + 1]
    return cs - gam[0:1]


def _hgrn_level_matrix():
    t = np.arange(HGRN_CHUNK)[:, None]
    s = np.arange(HGRN_CHUNK)[None, :]
    mats = []
    for upper in (False, True):
        for e in (1, 2):
            same = (t >> e) == (s >> e)
            mats.append(same & ((s > t) if upper else (s <= t)))
    mats.append(s <= t)
    return jnp.asarray(np.concatenate(mats, axis=0).astype(np.float32), dtype=BF16)


def _hgrn_prompt_body(*refs, layer):
    npair = HGRN_HEADS // 2
    q_refs, f_refs, i_refs, g_refs = (refs[i * npair:(i + 1) * npair] for i in range(4))
    lg_ref, nw_ref, cm_ref, o_ref, sf_ref, s_ref = refs[4 * npair:]
    c = pl.program_id(1)

    @pl.when(c == 0)
    def _():
        s_ref[...] = jnp.zeros_like(s_ref)

    n = HGRN_CHUNK
    heads = range(HGRN_HEADS)
    col = lambda rs, h: rs[h // 2][:, (h % 2) * HGRN_D:(h % 2 + 1) * HGRN_D]
    hsl = lambda h: slice(h * HGRN_D, (h + 1) * HGRN_D)
    t = lax.broadcasted_iota(jnp.int32, (n, n), 0)
    s_ = lax.broadcasted_iota(jnp.int32, (n, n), 1)
    lb_all = _hgrn_lower_bound(lg_ref[...], layer)
    cm = cm_ref[...]
    ones = jnp.ones((n, n), BF16)
    fgs = [lb_all[:, hsl(h)] + (1.0 - lb_all[:, hsl(h)]) * jax.nn.sigmoid(col(f_refs, h))
           for h in heads]
    lfs = [jnp.log(fg) for fg in fgs]
    css = [_split_dot_left(cm, lf) for lf in lfs]
    bs = [cs[4 * n:5 * n] for cs in css]

    def cl(h, e):
        if e <= 2:
            return css[h][(e - 1) * n:e * n]
        if e == HGRN_LEVELS:
            return bs[h]
        m = 1 << e
        b3 = bs[h].reshape(n // m, m, HGRN_D)
        ends = jnp.broadcast_to(b3[:, m - 1:m, :], b3.shape)
        before = jnp.concatenate([jnp.zeros_like(ends[:1]), ends[:-1]], axis=0)
        return (b3 - before).reshape(n, HGRN_D)

    def cu(h, e):
        if e <= 2:
            return css[h][(2 + e - 1) * n:(2 + e) * n]
        if e == HGRN_LEVELS:
            return bs[h][n - 1:n] - bs[h]
        m = 1 << e
        b3 = bs[h].reshape(n // m, m, HGRN_D)
        return (jnp.broadcast_to(b3[:, m - 1:m, :], b3.shape) - b3).reshape(n, HGRN_D)
    qs = [col(q_refs, h) * jax.nn.sigmoid(col(q_refs, h)) for h in heads]
    ks = [1.0 - fg for fg in fgs]
    vbs = [col(i_refs, h).astype(BF16) for h in heads]
    acc = [jnp.where(t == s_, _dot_nt(qs[h].astype(BF16), ks[h].astype(BF16)), 0.0) for h in heads]
    for e in range(HGRN_LEVELS):
        mask = (((t >> e) & 1) == 1) & ((s_ >> e) == ((t >> e) - 1))
        for h in heads:
            qe = qs[h] * (fgs[h] if e == 0 else jnp.exp(cl(h, e)))
            ke = ks[h] if e == 0 else ks[h] * jnp.exp(cu(h, e))
            acc[h] = acc[h] + jnp.where(mask, _dot_nt(qe.astype(BF16), ke.astype(BF16)), 0.0)
    sts = [s_ref[h] for h in heads]
    os_ = [_dot(acc[h].astype(BF16), vbs[h])
           + _dot((qs[h] * jnp.exp(cl(h, HGRN_LEVELS))).astype(BF16), sts[h].astype(BF16))
           for h in heads]
    dks = [jnp.exp(_split_dot(lfs[h].T, ones)) for h in heads]
    for h in heads:
        kf = ks[h] * jnp.exp(cu(h, HGRN_LEVELS))
        s_ref[h] = dks[h] * sts[h] + _dot(kf.T.astype(BF16), vbs[h])
    for h in heads:
        o = os_[h]
        ms = jnp.mean(o * o, axis=-1, keepdims=True)
        gr = col(g_refs, h)
        o_ref[:, hsl(h)] = (o * lax.rsqrt(ms + NORM_EPS) * nw_ref[:, hsl(h)]) * (gr * jax.nn.sigmoid(gr))

    @pl.when(c == pl.num_programs(1) - 1)
    def _():
        sf_ref[...] = s_ref[...]


def _split_dot_left(m, x):
    hi, lo = _split(x)
    return _dot(m, hi) + _dot(m, lo)


def _hgrn_prompt(proj, logits, nw, cmat, nseq, seq_len, layer):
    n = proj.shape[0]
    nch = seq_len // HGRN_CHUNK
    wb = 2 * HGRN_D
    npair = HGRN_HEADS // 2
    cols = [pl.BlockSpec((HGRN_CHUNK, wb), lambda b, c, j=(HGRN_COL0 + g * HGRN_W) // wb + p:
                         (b * nch + c, j)) for g in range(4) for p in range(npair)]
    return pl.pallas_call(
        functools.partial(_hgrn_prompt_body, layer=layer),
        grid=(nseq, nch),
        in_specs=cols + [pl.BlockSpec(logits.shape, lambda b, c: (0, 0)),
                         pl.BlockSpec((1, HGRN_W), lambda b, c: (0, 0)),
                         pl.BlockSpec(cmat.shape, lambda b, c: (0, 0))],
        out_specs=[pl.BlockSpec((HGRN_CHUNK, HGRN_W), lambda b, c: (b * nch + c, 0)),
                   pl.BlockSpec((None, HGRN_HEADS, HGRN_D, HGRN_D), lambda b, c: (b, 0, 0, 0))],
        out_shape=[jax.ShapeDtypeStruct((n, HGRN_W), F32),
                   jax.ShapeDtypeStruct((nseq, HGRN_HEADS, HGRN_D, HGRN_D), F32)],
        scratch_shapes=[pltpu.VMEM((HGRN_HEADS, HGRN_D, HGRN_D), F32)],
        compiler_params=_cp(2), name="hgrn_prompt")(*([proj] * (4 * npair)), logits, nw, cmat)


HGRN_KT = 32


def _hgrn_lanes_body(q_ref, f_ref, i_ref, g_ref, lg_ref, nw_ref, s0_ref, o_ref, sf_ref,
                     s_ref, qt_ref, ft_ref, kt_ref, vt_ref, oacc_ref, *, layer):
    kt = pl.program_id(1)
    ninst = q_ref.shape[1]
    lb = _hgrn_lower_bound(lg_ref[...], layer)
    s_ref[...] = s0_ref[0]

    @pl.when(kt == 0)
    def _():
        oacc_ref[...] = jnp.zeros_like(oacc_ref)
        for tl in range(SUBLANES):
            qr = q_ref[tl]
            fg = lb + (1.0 - lb) * jax.nn.sigmoid(f_ref[tl])
            qt_ref[tl] = (qr * jax.nn.sigmoid(qr)).T
            ft_ref[tl] = fg.T
            kt_ref[tl] = (1.0 - fg).T
            vt_ref[tl] = i_ref[tl].T

    def step(tl, carry):
        vt = vt_ref[tl]

        def krow(kk, o):
            row = kt * HGRN_KT + kk
            s = ft_ref[tl, pl.ds(row, 1), :] * s_ref[kk] + kt_ref[tl, pl.ds(row, 1), :] * vt
            s_ref[kk] = s
            return o + qt_ref[tl, pl.ds(row, 1), :] * s

        oacc_ref[tl] = lax.fori_loop(0, HGRN_KT, krow, oacc_ref[tl], unroll=2)
        return carry

    lax.fori_loop(0, SUBLANES, step, 0)
    sf_ref[0] = s_ref[...]

    @pl.when(kt == pl.num_programs(1) - 1)
    def _():
        for tl in range(SUBLANES):
            o = oacc_ref[tl].T
            ms = jnp.mean(o * o, axis=-1, keepdims=True)
            gr = g_ref[tl]
            o_ref[tl] = (o * lax.rsqrt(ms + NORM_EPS) * nw_ref[...]) * (gr * jax.nn.sigmoid(gr))


def _hgrn_lanes(proj_t, logits, nw, s0, layer):
    steps, ninst, _ = proj_t.shape
    col = lambda off: pl.BlockSpec((steps, ninst, HGRN_D),
                                   lambda h, k, off=off: (0, 0, off // HGRN_D + h))
    sspec = pl.BlockSpec((1, HGRN_KT, HGRN_D, ninst), lambda h, k: (h, k, 0, 0))
    tbuf = pltpu.VMEM((steps, HGRN_D, ninst), F32)
    return pl.pallas_call(
        functools.partial(_hgrn_lanes_body, layer=layer),
        grid=(HGRN_HEADS, HGRN_D // HGRN_KT),
        in_specs=[col(0), col(HGRN_W), col(2 * HGRN_W), col(3 * HGRN_W),
                  pl.BlockSpec((logits.shape[0], HGRN_D), lambda h, k: (0, h)),
                  pl.BlockSpec((1, HGRN_D), lambda h, k: (0, h)), sspec],
        out_specs=[pl.BlockSpec((steps, ninst, HGRN_D), lambda h, k: (0, 0, h)), sspec],
        out_shape=[jax.ShapeDtypeStruct((steps, ninst, HGRN_W), F32),
                   jax.ShapeDtypeStruct(s0.shape, F32)],
        scratch_shapes=[pltpu.VMEM((HGRN_KT, HGRN_D, ninst), F32), tbuf, tbuf, tbuf, tbuf, tbuf],
        compiler_params=_cp(2), name="hgrn_lanes")(
            proj_t, proj_t, proj_t, proj_t, logits, nw, s0)


def _rglru_body(xb_ref, gate_ref, chist_ref, hinit_ref, cw_ref, cb_ref, wa_ref, ba_ref, wx_ref,
                bx_ref, lam_ref, y_ref, cst_ref, hst_ref, ccarry_ref, hcarry_ref,
                *, long_seq, tps):
    x = xb_ref[...]
    tm, c = x.shape
    if long_seq:
        @pl.when(pl.program_id(0) % tps == 0)
        def _():
            ccarry_ref[...] = chist_ref[...]
            hcarry_ref[...] = hinit_ref[...]
        hist = ccarry_ref[...]
    else:
        hist = chist_ref[...]
    p1, p2, p3 = _prev_rows(x, hist, (1, 2, 3), long_seq)
    cw = cw_ref[...]
    xc = cb_ref[...] + cw[0:1] * p3 + cw[1:2] * p2 + cw[2:3] * p1 + cw[3:4] * x
    xcb = xc.astype(BF16)
    r = jax.nn.sigmoid(_dot(xcb, wa_ref[...]) + ba_ref[...])
    ig = jax.nn.sigmoid(_dot(xcb, wx_ref[...]) + bx_ref[...])
    log_a = (-LRU_C) * r * _softplus(-lam_ref[...])
    a = jnp.exp(log_a)
    h = jnp.sqrt(-_expm1(2.0 * log_a)) * (ig * xc)
    if long_seq:
        pos = lax.broadcasted_iota(jnp.int32, (tm, c), 0)
        k = 1
        while k < tm:
            keep = pos >= k
            h = h + a * jnp.where(keep, pltpu.roll(h, k, 0), 0.0)
            a = a * jnp.where(keep, pltpu.roll(a, k, 0), 1.0)
            k *= 2
        h = h + a * hcarry_ref[SUBLANES - 1:SUBLANES, :]
        hcarry_ref[...] = h[tm - SUBLANES:]
        ccarry_ref[...] = x[tm - SUBLANES:]
        cst_ref[...] = x[tm - SUBLANES:]
        hst_ref[...] = h[tm - SUBLANES:]
    else:
        shp = (tm // SUBLANES, SUBLANES, c)
        h3, a3 = h.reshape(shp), a.reshape(shp)
        pos = lax.broadcasted_iota(jnp.int32, shp, 1)
        k = 1
        while k < SUBLANES:
            keep = pos >= k
            h3 = h3 + a3 * jnp.where(keep, pltpu.roll(h3, k, 1), 0.0)
            a3 = a3 * jnp.where(keep, pltpu.roll(a3, k, 1), 1.0)
            k *= 2
        h = (h3 + a3 * hinit_ref[...].reshape(shp)).reshape(tm, c)
        cst_ref[...] = x
        hst_ref[...] = h
    y_ref[...] = h * jax.nn.gelu(gate_ref[...])


def _rglru(proj, chist, hinit, lp, seq_len, tm):
    n = proj.shape[0]
    long_seq = seq_len > SUBLANES
    tps = max(seq_len // tm, 1)
    c0 = LRU_COL0 // LRU_W
    if long_seq:
        hs = pl.BlockSpec((SUBLANES, LRU_W), lambda i: (i // tps, 0))
    else:
        hs = pl.BlockSpec((tm, LRU_W), lambda i: (i, 0))
    row = pl.BlockSpec((1, LRU_W), lambda i: (0, 0))
    sq = pl.BlockSpec((LRU_W, LRU_W), lambda i: (0, 0))
    return pl.pallas_call(
        functools.partial(_rglru_body, long_seq=long_seq, tps=tps), grid=(n // tm,),
        in_specs=[pl.BlockSpec((tm, LRU_W), lambda i: (i, c0)),
                  pl.BlockSpec((tm, LRU_W), lambda i: (i, c0 + 1)), hs, hs,
                  pl.BlockSpec((4, LRU_W), lambda i: (0, 0)), row, sq, row, sq, row, row],
        out_specs=[pl.BlockSpec((tm, LRU_W), lambda i: (i, 0)), hs, hs],
        out_shape=[jax.ShapeDtypeStruct((n, LRU_W), F32),
                   jax.ShapeDtypeStruct(chist.shape, F32),
                   jax.ShapeDtypeStruct(chist.shape, F32)],
        scratch_shapes=[pltpu.VMEM((SUBLANES, LRU_W), F32), pltpu.VMEM((SUBLANES, LRU_W), F32)],
        compiler_params=_cp(1), name="rglru")(
            proj, proj, chist, hinit, lp["lru_cw"], lp["lru_cb"], lp["lru_wa"], lp["lru_ba"],
            lp["lru_wx"], lp["lru_bx"], lp["lru_lam"])


def _hist(state):
    nseq, k, c = state.shape
    return jnp.pad(state, ((0, 0), (SUBLANES - k, 0), (0, 0))).reshape(nseq * SUBLANES, c)


def _block_diag(w):
    h, a, b = w.shape
    eye = jnp.eye(h, dtype=w.dtype)
    return (eye[:, None, :, None] * w[:, :, None, :]).reshape(h * a, h * b)


def _layer(x3, st, lp, consts, layer, final_nw):
    nseq, seq_len, d = x3.shape
    n = nseq * seq_len
    long_seq = seq_len > SUBLANES
    s_rw, shift_rw, s_hg, h_lru, buf_lru, buf_ffn = st
    x = x3.reshape(n, d)
    ones = consts["ones"]

    proj = _in_proj(x, lp["norm_mix"], consts["w_in"], layer, tm=min(n, 1024), tn=IN_COLS // 4)

    tm_pre = min(n, seq_len if long_seq else n, 256)
    lmat_pre = None
    if long_seq:
        tril = np.tril(np.ones((RWKV_CHUNK, RWKV_CHUNK)))
        lmat_pre = jnp.asarray(np.kron(np.eye(tm_pre // RWKV_CHUNK), tril), dtype=BF16)
    outs = _rwkv_pre(proj, _hist(shift_rw[:, None, :]), lp, ones, seq_len, tm_pre, lmat_pre)
    r, w, k, v, na, kb, g, bonus, shift_out = outs[:9]
    new_shift = shift_out.reshape(nseq, SUBLANES, RWKV_COLS)[:, SUBLANES - 1]
    if long_seq:
        c = RWKV_CHUNK
        t4 = _tri_solve(outs[9].transpose(0, 2, 3, 1)).transpose(0, 3, 1, 2)
        y, spt = _rwkv_chunk((r, w, k, v, na, kb), t4, consts["lmat"], nseq)
        sp = spt.reshape(nseq, RWKV_HEADS // 2, 2, RWKV_N, 2, RWKV_N)
        new_s_rw = jnp.stack([sp[:, :, 0, :, 0, :], sp[:, :, 1, :, 1, :]], axis=2)
        new_s_rw = new_s_rw.reshape(nseq, RWKV_HEADS, RWKV_N, RWKV_N).transpose(0, 1, 3, 2)
    else:
        tmaj = lambda t: t.reshape(nseq, seq_len, RWKV_W).transpose(1, 0, 2)
        s0 = s_rw.transpose(1, 2, 3, 0)
        yb, sf = _rwkv_lanes([tmaj(t) for t in (r, w, k, v, na, kb)], s0)
        y = yb.transpose(1, 0, 2).reshape(n, RWKV_W)
        new_s_rw = sf.transpose(3, 0, 1, 2)
    y_rw = _rwkv_post(y, bonus, g, lp["ln_w"], lp["ln_b"], ones, tm=min(n, 512))

    if long_seq:
        y_hg, new_s_hg = _hgrn_prompt(proj, consts["lb_logits"], lp["hgrn_nw"], consts["cmat"],
                                      nseq, seq_len, layer)
    else:
        proj_t = proj[:, HGRN_COL0:LRU_COL0].reshape(nseq, seq_len, 4 * HGRN_W).transpose(1, 0, 2)
        s0 = s_hg.transpose(1, 2, 3, 0)
        o_t, sf = _hgrn_lanes(proj_t, consts["lb_logits"], lp["hgrn_nw"], s0, layer)
        y_hg = o_t.transpose(1, 0, 2).reshape(n, HGRN_W)
        new_s_hg = sf.transpose(3, 0, 1, 2)

    if long_seq:
        hinit = _hist(h_lru[:, None, :])
        tm_lru = min(seq_len, 256)
    else:
        hinit = jnp.broadcast_to(h_lru[:, None, :], (nseq, seq_len, LRU_W)).reshape(n, LRU_W)
        tm_lru = min(n, 512)
    y_lru, cst, hst = _rglru(proj, _hist(buf_lru), hinit, lp, seq_len, tm_lru)
    new_buf_lru = cst.reshape(nseq, SUBLANES, LRU_W)[:, SUBLANES - 3:]
    new_h_lru = hst.reshape(nseq, SUBLANES, LRU_W)[:, SUBLANES - 1]

    x = _out_proj(x, y_rw, y_hg, y_lru, consts["w_out"], layer, tm=min(n, 1024), tn=512)
    tm_ffn = min(n, seq_len if long_seq else n, 1024)
    outs = _ffn(x, lp["norm_ffn"], consts["w_up"], lp["ffn_cw"], lp["ffn_cb"], consts["w_down"],
                _hist(buf_ffn), layer, seq_len, tm=tm_ffn, tf=512, final_nw=final_nw)
    x, fst = outs
    new_buf_ffn = fst.reshape(nseq, -1, SUBLANES, D_FF)[:, -1, SUBLANES - 2:]
    return x.reshape(nseq, seq_len, d), (new_s_rw, new_shift, new_s_hg, new_h_lru, new_buf_lru,
                                         new_buf_ffn)


def kernel(x_prompt, x_sample, state_rwkv, state_rwkv_shift, state_hgrn, state_rglru, cache_rglru_conv, cache_ffn_conv, norm_mix, w_in, rwkv_mu, rwkv_w0, rwkv_w2, rwkv_a0, rwkv_a2, rwkv_g2, rwkv_k_k, rwkv_k_a, rwkv_r_k, rwkv_ln_w, rwkv_ln_b, hgrn_lb_logits, hgrn_norm_w, rglru_conv_w, rglru_conv_b, rglru_wa, rglru_ba, rglru_wx, rglru_bx, rglru_lambda, w_out, norm_ffn, ffn_w_up, ffn_conv_w, ffn_conv_b, ffn_w_down, norm_final):
    depth = w_in.shape[0]
    nb, nt, _ = x_prompt.shape
    db = x_sample.shape[0]
    consts = {
        "ones": jnp.asarray(np.kron(np.eye(RWKV_HEADS), np.ones((RWKV_N, RWKV_N))), dtype=BF16),
        "cmat": _hgrn_level_matrix(),
        "lmat": jnp.asarray(np.tril(np.ones((RWKV_CHUNK, RWKV_CHUNK), np.float32)), dtype=BF16),
        "lb_logits": hgrn_lb_logits,
        "w_in": w_in.astype(BF16), "w_out": w_out.astype(BF16),
        "w_up": ffn_w_up.astype(BF16), "w_down": ffn_w_down.astype(BF16),
    }
    rowv = lambda a: a.reshape(1, -1)
    x_p, x_s = x_prompt, x_sample
    new_p, new_s = [], []
    for l in range(depth):
        lp = {
            "norm_mix": rowv(norm_mix[l]), "mu": rowv(rwkv_mu[l]),
            "w0": rowv(rwkv_w0[l]),
            "w2p": jnp.pad(rwkv_w2[l], ((0, 64), (0, 0))).astype(BF16),
            "a0": rowv(rwkv_a0[l]),
            "a2p": jnp.pad(rwkv_a2[l], ((64, 0), (0, 0))).astype(BF16),
            "g2": rwkv_g2[l].astype(BF16), "k_k": rowv(rwkv_k_k[l]), "k_a": rowv(rwkv_k_a[l]),
            "r_k": rowv(rwkv_r_k[l]), "ln_w": rowv(rwkv_ln_w[l]), "ln_b": rowv(rwkv_ln_b[l]),
            "hgrn_nw": rowv(hgrn_norm_w[l]),
            "lru_cw": rglru_conv_w[l], "lru_cb": rowv(rglru_conv_b[l]),
            "lru_wa": _block_diag(rglru_wa[l]).astype(BF16), "lru_ba": rowv(rglru_ba[l]),
            "lru_wx": _block_diag(rglru_wx[l]).astype(BF16), "lru_bx": rowv(rglru_bx[l]),
            "lru_lam": rowv(rglru_lambda[l]),
            "norm_ffn": rowv(norm_ffn[l]), "ffn_cw": ffn_conv_w[l],
            "ffn_cb": rowv(ffn_conv_b[l]),
        }
        zero = lambda *s: jnp.zeros(s, F32)
        st_p = (None, zero(nb, RWKV_COLS), None, zero(nb, LRU_W), zero(nb, 3, LRU_W),
                zero(nb, 2, D_FF))
        final_nw = rowv(norm_final) if l == depth - 1 else None
        x_p, sp = _layer(x_p, st_p, lp, consts, l, final_nw)
        st_s = (state_rwkv[l], state_rwkv_shift[l], state_hgrn[l], state_rglru[l],
                cache_rglru_conv[l], cache_ffn_conv[l])
        x_s, ss = _layer(x_s, st_s, lp, consts, l, final_nw)
        new_p.append(sp)
        new_s.append(ss)
    stack = lambda sts: [jnp.stack(s, axis=0) for s in zip(*sts)]
    return (x_p, x_s, *stack(new_p), *stack(new_s))
```

```python
import functools

import numpy as np
import jax
import jax.numpy as jnp
from jax import lax
from jax.experimental import pallas as pl
from jax.experimental.pallas import tpu as pltpu

F32 = jnp.float32
BF16 = jnp.bfloat16

D_MODEL = 2048
RWKV_HEADS = 12
RWKV_N = 64
RWKV_W = RWKV_HEADS * RWKV_N
RWKV_COLS = 2560
RWKV_LN_EPS = 64e-5
HGRN_HEADS = 6
HGRN_D = 128
HGRN_W = HGRN_HEADS * HGRN_D
HGRN_COL0 = RWKV_COLS
LRU_W = 512
LRU_COL0 = RWKV_COLS + 4 * HGRN_W
LRU_C = 8.0
IN_COLS = 6656
D_FF = 5632
NORM_EPS = 1e-6

SUBLANES = 8
LANES = 128
RWKV_CHUNK = 64
HGRN_CHUNK = 128
HGRN_LEVELS = 7
VMEM_LIMIT = 48 * 1024 * 1024
BIG_VMEM_LIMIT = 56 * 1024 * 1024


def _cp(n, limit=VMEM_LIMIT):
    return pltpu.CompilerParams(dimension_semantics=("arbitrary",) * n, vmem_limit_bytes=limit)


def _dot(a, b):
    return jnp.dot(a, b, preferred_element_type=F32)


def _dot_nt(a, b):
    return lax.dot_general(a, b, (((1,), (1,)), ((), ())), preferred_element_type=F32)


def _split(x):
    hi = x.astype(BF16)
    lo = (x - hi.astype(F32)).astype(BF16)
    return hi, lo


def _split_dot(x, m):
    hi, lo = _split(x)
    return _dot(hi, m) + _dot(lo, m)


def _dot3(a, b, nt=False):
    f = _dot_nt if nt else _dot
    ah, al = _split(a)
    bh, bl = _split(b)
    return f(ah, bh) + (f(ah, bl) + f(al, bh))


def _softplus(z):
    return jnp.maximum(z, 0.0) + jnp.log1p(jnp.exp(-jnp.abs(z)))


def _expm1(z):
    return jnp.tanh(0.5 * z) * (jnp.exp(z) + 1.0)


def _prev_rows(x, hist, ks, long_seq):
    tm, c = x.shape
    if long_seq:
        ext = jnp.concatenate([hist, x], axis=0)
        return [pltpu.roll(ext, k, 0)[SUBLANES:] for k in ks]
    x3 = x.reshape(tm // SUBLANES, SUBLANES, c)
    h3 = hist.reshape(tm // SUBLANES, SUBLANES, c)
    pos = lax.broadcasted_iota(jnp.int32, x3.shape, 1)
    return [jnp.where(pos >= k, pltpu.roll(x3, k, 1), pltpu.roll(h3, k, 1)).reshape(tm, c)
            for k in ks]


def _in_proj_body(x_ref, nw_ref, w_ref, o_ref, *rest):
    xn_ref = rest[-1]

    @pl.when(pl.program_id(1) == 0)
    def _():
        x = x_ref[...]
        ms = jnp.mean(x * x, axis=-1, keepdims=True)
        xn_ref[...] = (x * lax.rsqrt(ms + NORM_EPS) * nw_ref[...]).astype(BF16)
    w = w_ref[...].astype(BF16)
    if len(rest) == 2:
        rest[0][...] = w
    o_ref[...] = _dot(xn_ref[...], w)


def _in_proj(x, nw, w, layer, tm, tn):
    n, d = x.shape
    c = w.shape[2]
    emit = w.dtype != BF16
    assert not emit or n == tm
    out_specs = [pl.BlockSpec((tm, tn), lambda i, j: (i, j))]
    out_shape = [jax.ShapeDtypeStruct((n, c), F32)]
    if emit:
        out_specs.append(pl.BlockSpec((None, d, tn), lambda i, j: (0, 0, j)))
        out_shape.append(jax.ShapeDtypeStruct((1, d, c), BF16))
    outs = pl.pallas_call(
        _in_proj_body, grid=(n // tm, c // tn),
        in_specs=[pl.BlockSpec((tm, d), lambda i, j: (i, 0)),
                  pl.BlockSpec((1, d), lambda i, j: (0, 0)),
                  pl.BlockSpec((None, d, tn), lambda i, j: (layer, 0, j))],
        out_specs=out_specs, out_shape=out_shape,
        scratch_shapes=[pltpu.VMEM((tm, d), BF16)],
        compiler_params=_cp(2, BIG_VMEM_LIMIT), name="in_proj")(x, nw, w)
    return outs if emit else (outs[0], None)


def _out_proj_body(x_ref, ya_ref, yb_ref, yc_ref, w_ref, o_ref, y_ref):
    @pl.when(pl.program_id(1) == 0)
    def _():
        y_ref[:, 0:RWKV_W] = ya_ref[...].astype(BF16)
        y_ref[:, RWKV_W:RWKV_W + HGRN_W] = yb_ref[...].astype(BF16)
        y_ref[:, RWKV_W + HGRN_W:] = yc_ref[...].astype(BF16)
    o_ref[...] = x_ref[...] + _dot(y_ref[...], w_ref[...])


def _out_proj(x, ya, yb, yc, w, layer, tm, tn):
    n, d = x.shape
    return pl.pallas_call(
        _out_proj_body, grid=(n // tm, d // tn),
        in_specs=[pl.BlockSpec((tm, tn), lambda i, j: (i, j)),
                  pl.BlockSpec((tm, RWKV_W), lambda i, j: (i, 0)),
                  pl.BlockSpec((tm, HGRN_W), lambda i, j: (i, 0)),
                  pl.BlockSpec((tm, LRU_W), lambda i, j: (i, 0)),
                  pl.BlockSpec((None, d, tn), lambda i, j: (layer, 0, j))],
        out_specs=pl.BlockSpec((tm, tn), lambda i, j: (i, j)),
        out_shape=jax.ShapeDtypeStruct((n, d), F32),
        scratch_shapes=[pltpu.VMEM((tm, d), BF16)],
        compiler_params=_cp(2), name="out_proj")(x, ya, yb, yc, w)


def _ffn_body(*refs, long_seq, tps, final):
    x_ref, nw_ref, wg_ref, wv_ref, cw_ref, cb_ref, wd_ref, hist_ref = refs[:8]
    if final:
        fnw_ref, o_ref, st_ref, hn_ref, carry_ref = refs[8:]
    else:
        o_ref, st_ref, hn_ref, carry_ref = refs[8:]
    i = pl.program_id(0)
    j = pl.program_id(1)

    @pl.when(j == 0)
    def _():
        x = x_ref[...]
        ms = jnp.mean(x * x, axis=-1, keepdims=True)
        hn_ref[...] = (x * lax.rsqrt(ms + NORM_EPS) * nw_ref[...]).astype(BF16)
        o_ref[...] = x

    hn = hn_ref[...]
    if long_seq:
        @pl.when(i % tps == 0)
        def _():
            carry_ref[j] = hist_ref[...]
        hist = carry_ref[j]
    else:
        hist = hist_ref[...]
    tf = wg_ref.shape[1]
    halves = [slice(q * (tf // 2), (q + 1) * (tf // 2)) for q in range(2)]
    gs = [_dot(hn, wg_ref[:, sl]) for sl in halves]
    vs = [_dot(hn, wv_ref[:, sl]) for sl in halves]
    cw = cw_ref[...]
    acc = None
    for sl, g, v in zip(halves, gs, vs):
        p1, p2 = _prev_rows(g, hist[:, sl], (1, 2), long_seq)
        gc = cb_ref[:, sl] + cw[0:1, sl] * p2 + cw[1:2, sl] * p1 + cw[2:3, sl] * g
        h = (gc * jax.nn.sigmoid(gc)) * v
        d = _dot(h.astype(BF16), wd_ref[sl, :])
        acc = d if acc is None else acc + d
        tail = g if not long_seq else g[g.shape[0] - SUBLANES:]
        st_ref[:, sl] = tail
        if long_seq:
            carry_ref[j, :, sl] = tail
    o_ref[...] += acc

    if final:
        @pl.when(j == pl.num_programs(1) - 1)
        def _():
            y = o_ref[...]
            ms = jnp.mean(y * y, axis=-1, keepdims=True)
            o_ref[...] = y * lax.rsqrt(ms + NORM_EPS) * fnw_ref[...]


def _ffn(x, nw, w_up, cw, cb, w_down, hist, layer, seq_len, tm, tf, final_nw=None):
    n, d = x.shape
    final = final_nw is not None
    nf = D_FF // tf
    long_seq = seq_len > SUBLANES
    tps = max(seq_len // tm, 1)
    if long_seq:
        hist_spec = pl.BlockSpec((SUBLANES, tf), lambda i, j: (i // tps, j))
        st_spec = pl.BlockSpec((SUBLANES, tf), lambda i, j: (i, j))
        st_rows = n // tm * SUBLANES
    else:
        hist_spec = pl.BlockSpec((tm, tf), lambda i, j: (i, j))
        st_spec = pl.BlockSpec((tm, tf), lambda i, j: (i, j))
        st_rows = n
    body = functools.partial(_ffn_body, long_seq=long_seq, tps=tps, final=final)
    row_tile = pl.BlockSpec((tm, d), lambda i, j: (i, 0))
    wide = pl.BlockSpec((1, d), lambda i, j: (0, 0))
    in_specs = [row_tile, wide,
                pl.BlockSpec((None, d, tf), lambda i, j: (layer, 0, j)),
                pl.BlockSpec((None, d, tf), lambda i, j: (layer, 0, j + nf)),
                pl.BlockSpec((3, tf), lambda i, j: (0, j)),
                pl.BlockSpec((1, tf), lambda i, j: (0, j)),
                pl.BlockSpec((None, tf, d), lambda i, j: (layer, j, 0)),
                hist_spec]
    args = [x, nw, w_up, w_up, cw, cb, w_down, hist]
    out_specs = [row_tile, st_spec]
    out_shape = [jax.ShapeDtypeStruct((n, d), F32), jax.ShapeDtypeStruct((st_rows, D_FF), F32)]
    scratch = [pltpu.VMEM((tm, d), BF16), pltpu.VMEM((nf, SUBLANES, tf), F32)]
    if final:
        in_specs.append(wide)
        args.append(final_nw)
    return pl.pallas_call(
        body, grid=(n // tm, nf), in_specs=in_specs, out_specs=out_specs, out_shape=out_shape,
        scratch_shapes=scratch,
        compiler_params=pltpu.CompilerParams(dimension_semantics=("arbitrary",) * 2,
                                             vmem_limit_bytes=BIG_VMEM_LIMIT),
        name="ffn")(*args)


def _rwkv_pre_body(*refs, long_seq, tps, with_n):
    (p_ref, hist_ref, mu_ref, w0_ref, w2_ref, a0_ref, a2_ref, g2_ref, kk_ref, ka_ref, rk_ref,
     ones_ref) = refs[:12]
    refs = refs[12:]
    if with_n:
        l_ref, refs = refs[0], refs[1:]
    r_ref, w_ref, k_ref, v_ref, na_ref, kb_ref, g_ref, bonus_ref, st_ref = refs[:9]
    refs = refs[9:]
    if with_n:
        n_ref, refs = refs[0], refs[1:]
    (carry_ref,) = refs
    p = p_ref[...]
    if long_seq:
        @pl.when(pl.program_id(0) % tps == 0)
        def _():
            carry_ref[...] = hist_ref[...]
        hist = carry_ref[...]
    else:
        hist = hist_ref[...]
    (prev,) = _prev_rows(p, hist, (1,), long_seq)
    if long_seq:
        carry_ref[...] = p[p.shape[0] - SUBLANES:]
        st_ref[...] = p[p.shape[0] - SUBLANES:]
    else:
        st_ref[...] = p
    xs = p + (prev - p) * mu_ref[...]
    r = xs[:, 0:RWKV_W]
    k = xs[:, RWKV_W:2 * RWKV_W]
    v = xs[:, 2 * RWKV_W:3 * RWKV_W]
    xwa = xs[:, 3 * RWKV_W:3 * RWKV_W + 128]
    xg = xs[:, 3 * RWKV_W + 128:RWKV_COLS]
    zw = w0_ref[...] + _dot(jnp.tanh(xwa).astype(BF16), w2_ref[...])
    w_log = -_softplus(-zw) - 0.5
    decay = jnp.exp(-jnp.exp(w_log))
    a = jax.nn.sigmoid(a0_ref[...] + _dot(xwa.astype(BF16), a2_ref[...]))
    g = _dot(jax.nn.sigmoid(xg).astype(BF16), g2_ref[...])
    ones = ones_ref[...]
    kk = k * kk_ref[...]
    kk = kk / jnp.maximum(jnp.sqrt(_split_dot(kk * kk, ones)), 1e-12)
    kf = k * (1.0 + (a - 1.0) * ka_ref[...])
    r_ref[...] = r
    w_ref[...] = decay
    k_ref[...] = kf
    v_ref[...] = v
    na_ref[...] = -kk
    kb_ref[...] = kk * a
    g_ref[...] = g
    bonus_ref[...] = _split_dot(r * kf * rk_ref[...], ones) * v
    if with_n:
        c = RWKV_CHUNK
        pw = 2 * RWKV_N
        lw = jnp.log(decay)
        cw = _split_dot_left(l_ref[...], lw)
        at = _split(-kk * jnp.exp(cw - lw))
        bt = _split(kk * a * jnp.exp(-cw))
        strict = (lax.broadcasted_iota(jnp.int32, (c, c), 1)
                  < lax.broadcasted_iota(jnp.int32, (c, c), 0))
        lane = lax.broadcasted_iota(jnp.int32, (p.shape[0], pw), 1)
        for pr in range(RWKV_HEADS // 2):
            sl = slice(pr * pw, (pr + 1) * pw)
            for hh in range(2):
                m = (lane < RWKV_N) if hh == 0 else (lane >= RWKV_N)
                ah = jnp.where(m, at[0][:, sl], jnp.zeros_like(at[0][:, sl]))
                al = jnp.where(m, at[1][:, sl], jnp.zeros_like(at[1][:, sl]))
                for ci in range(p.shape[0] // c):
                    rs = slice(ci * c, (ci + 1) * c)
                    bh, bl = bt[0][rs, sl], bt[1][rs, sl]
                    nm = _dot_nt(ah[rs], bh) + (_dot_nt(ah[rs], bl) + _dot_nt(al[rs], bh))
                    n_ref[2 * pr + hh, ci] = jnp.where(strict, nm, 0.0)


def _rwkv_pre(proj, hist, lp, ones, seq_len, tm, lmat=None):
    n = proj.shape[0]
    long_seq = seq_len > SUBLANES
    tps = max(seq_len // tm, 1)
    with_n = lmat is not None
    row = lambda c: pl.BlockSpec((1, c), lambda i: (0, 0))
    full = lambda a, b: pl.BlockSpec((a, b), lambda i: (0, 0))
    if long_seq:
        hist_spec = pl.BlockSpec((SUBLANES, RWKV_COLS), lambda i: (i // tps, 0))
    else:
        hist_spec = pl.BlockSpec((tm, RWKV_COLS), lambda i: (i, 0))
    tile = pl.BlockSpec((tm, RWKV_W), lambda i: (i, 0))
    body = functools.partial(_rwkv_pre_body, long_seq=long_seq, tps=tps, with_n=with_n)
    in_specs = [pl.BlockSpec((tm, RWKV_COLS), lambda i: (i, 0)), hist_spec,
                row(RWKV_COLS), row(RWKV_W), full(128, RWKV_W), row(RWKV_W),
                full(128, RWKV_W), full(128, RWKV_W), row(RWKV_W), row(RWKV_W), row(RWKV_W),
                full(RWKV_W, RWKV_W)]
    args = [proj, hist, lp["mu"], lp["w0"], lp["w2p"], lp["a0"], lp["a2p"], lp["g2"],
            lp["k_k"], lp["k_a"], lp["r_k"], ones]
    out_specs = [tile] * 8 + [hist_spec]
    out_shape = [jax.ShapeDtypeStruct((n, RWKV_W), F32)] * 8 + [jax.ShapeDtypeStruct(hist.shape, F32)]
    if with_n:
        c = RWKV_CHUNK
        in_specs.append(full(tm, tm))
        args.append(lmat)
        out_specs.append(pl.BlockSpec((RWKV_HEADS, tm // c, c, c), lambda i: (0, i, 0, 0)))
        out_shape.append(jax.ShapeDtypeStruct((RWKV_HEADS, n // c, c, c), F32))
    return pl.pallas_call(
        body, grid=(n // tm,), in_specs=in_specs, out_specs=out_specs, out_shape=out_shape,
        scratch_shapes=[pltpu.VMEM((SUBLANES, RWKV_COLS), F32)],
        compiler_params=_cp(1), name="rwkv_pre")(*args)


def _rwkv_post_body(y_ref, bonus_ref, g_ref, lw_ref, lb_ref, ones_ref, o_ref):
    y = y_ref[...]
    ones = ones_ref[...]
    mean = _split_dot(y, ones) * (1.0 / RWKV_N)
    d = y - mean
    var = _split_dot(d * d, ones) * (1.0 / RWKV_N)
    yn = d * lax.rsqrt(var + RWKV_LN_EPS)
    o_ref[...] = (yn * lw_ref[...] + lb_ref[...] + bonus_ref[...]) * g_ref[...]


def _rwkv_post(y, bonus, g, lw, lb, ones, tm):
    n = y.shape[0]
    tile = pl.BlockSpec((tm, RWKV_W), lambda i: (i, 0))
    row = pl.BlockSpec((1, RWKV_W), lambda i: (0, 0))
    return pl.pallas_call(
        _rwkv_post_body, grid=(n // tm,),
        in_specs=[tile, tile, tile, row, row, pl.BlockSpec((RWKV_W, RWKV_W), lambda i: (0, 0))],
        out_specs=tile, out_shape=jax.ShapeDtypeStruct((n, RWKV_W), F32),
        compiler_params=_cp(1), name="rwkv_post")(y, bonus, g, lw, lb, ones)


def _rwkv_lanes_body(r_ref, w_ref, k_ref, v_ref, a_ref, b_ref, s0_ref, y_ref, sf_ref,
                     s_ref, xt_ref, y_buf):
    tb = pl.program_id(1)
    ninst = r_ref.shape[1]

    @pl.when(tb == 0)
    def _():
        for hh in range(2):
            s_ref[hh] = s0_ref[hh]

    def step(tl, carry):
        for idx, ref in enumerate((r_ref, w_ref, k_ref, v_ref, a_ref, b_ref)):
            xt_ref[idx] = ref[tl].T
        for hh in range(2):
            base = hh * RWKV_N
            heads = lambda i: xt_ref[i, base:base + RWKV_N, :]

            def row(rho, c):
                s = s_ref[hh, rho]
                sa = jnp.sum(s * heads(4), axis=0, keepdims=True)
                vrow = xt_ref[3, pl.ds(base + rho, 1), :]
                s = s * heads(1) + sa * heads(5) + vrow * heads(2)
                s_ref[hh, rho] = s
                y_buf[pl.ds(base + rho, 1), :] = jnp.sum(s * heads(0), axis=0, keepdims=True)
                return c

            lax.fori_loop(0, RWKV_N, row, 0, unroll=8)
        y_ref[tl] = y_buf[...].T
        return carry

    lax.fori_loop(0, SUBLANES, step, 0)

    @pl.when(tb == pl.num_programs(1) - 1)
    def _():
        for hh in range(2):
            sf_ref[hh] = s_ref[hh]


def _rwkv_lanes(xs, s0):
    steps, ninst, _ = xs[0].shape
    xspec = pl.BlockSpec((SUBLANES, ninst, 2 * RWKV_N), lambda h, t: (t, 0, h))
    sspec = pl.BlockSpec((2, RWKV_N, RWKV_N, ninst), lambda h, t: (h, 0, 0, 0))
    return pl.pallas_call(
        _rwkv_lanes_body, grid=(RWKV_HEADS // 2, steps // SUBLANES),
        in_specs=[xspec] * 6 + [sspec], out_specs=[xspec, sspec],
        out_shape=[jax.ShapeDtypeStruct((steps, ninst, RWKV_W), F32),
                   jax.ShapeDtypeStruct(s0.shape, F32)],
        scratch_shapes=[pltpu.VMEM((2, RWKV_N, RWKV_N, ninst), F32),
                        pltpu.VMEM((6, 2 * RWKV_N, ninst), F32),
                        pltpu.VMEM((2 * RWKV_N, ninst), F32)],
        compiler_params=_cp(2), name="rwkv_lanes")(*xs, s0)


RWKV_C_CHUNKS = 2


def _tri_solve_body(nt_ref, tt_ref):
    c = RWKV_CHUNK
    ninst = nt_ref.shape[2]
    tt_ref[...] = jnp.zeros_like(tt_ref)
    g = SUBLANES
    nb = 4
    sub = lax.broadcasted_iota(jnp.int32, (g, ninst), 0)
    for t0 in range(0, c, nb):
        ngrp = (t0 + nb - 1) // g + 1
        rs = [[jnp.where(sub + g * j == t0 + i, 1.0, 0.0).astype(F32) for j in range(ngrp)]
              for i in range(nb)]
        for sg in range(-(-t0 // g)):

            def acc(s, carry, sg=sg, t0=t0):
                ts = [tt_ref[s, j * g:(j + 1) * g, :] for j in range(sg + 1)]
                nrow = [nt_ref[t0 + i, pl.ds(s, 1), :] for i in range(nb)]
                return tuple(tuple(carry[i][j] + nrow[i] * ts[j] for j in range(sg + 1))
                             for i in range(nb))

            res = lax.fori_loop(sg * g, min(sg * g + g, t0), acc,
                                tuple(tuple(rs[i][:sg + 1]) for i in range(nb)))
            for i in range(nb):
                rs[i][:sg + 1] = list(res[i])
        for i in range(nb):
            for j in range(i):
                nij = nt_ref[t0 + i, t0 + j:t0 + j + 1, :]
                rs[i] = [x + nij * y for x, y in zip(rs[i], rs[j])]
            tt_ref[t0 + i, 0:ngrp * g, :] = jnp.concatenate(rs[i], axis=0)


def _tri_solve(nt):
    nh, c, _, ninst = nt.shape
    spec = pl.BlockSpec((None, c, c, ninst), lambda h: (h, 0, 0, 0))
    return pl.pallas_call(
        _tri_solve_body, grid=(nh,), in_specs=[spec], out_specs=spec,
        out_shape=jax.ShapeDtypeStruct(nt.shape, F32),
        compiler_params=_cp(1), name="tri_solve")(nt)


def _rwkv_chunk_body(r_ref, w_ref, k_ref, v_ref, a_ref, b_ref, t_ref, l_ref, y_ref, sf_ref, s_ref):
    c = RWKV_CHUNK
    pw = 2 * RWKV_N
    ci = pl.program_id(1)

    @pl.when(ci == 0)
    def _():
        s_ref[...] = jnp.zeros_like(s_ref)

    lmat = l_ref[...]
    ti = lax.broadcasted_iota(jnp.int32, (c, c), 0)
    si = lax.broadcasted_iota(jnp.int32, (c, c), 1)
    strict, incl = si < ti, si <= ti
    lane = lax.broadcasted_iota(jnp.int32, (c, pw), 1)
    lane2 = lax.broadcasted_iota(jnp.int32, (2 * c, pw), 1)
    r2 = lax.broadcasted_iota(jnp.int32, (pw, pw), 0)
    c2 = lax.broadcasted_iota(jnp.int32, (pw, pw), 1)
    same_head = (r2 < RWKV_N) == (c2 < RWKV_N)
    eye = r2 == c2
    bz = lambda x: jnp.zeros_like(x)
    pairs = range(RWKV_HEADS // 2)
    units = [(q, p) for q in range(RWKV_C_CHUNKS) for p in pairs]
    at = lambda ref, u: ref[u[0] * c:(u[0] + 1) * c, u[1] * pw:(u[1] + 1) * pw]
    halves = [(u, hh) for u in units for hh in range(2)]
    lws = {u: jnp.log(at(w_ref, u)) for u in units}
    cws = {u: _split_dot_left(lmat, lws[u]) for u in units}
    vs = {u: at(v_ref, u) for u in units}
    wts = {u: jnp.exp(cws[u]) for u in units}
    rts = {u: at(r_ref, u) * wts[u] for u in units}
    ats = {u: at(a_ref, u) * jnp.exp(cws[u] - lws[u]) for u in units}
    lhs = {u: _split(jnp.concatenate([ats[u], rts[u]], axis=0)) for u in units}
    bts = {u: _split(at(b_ref, u) * jnp.exp(-cws[u])) for u in units}
    kts = {u: _split(at(k_ref, u) * jnp.exp(-cws[u])) for u in units}
    a_ak, a_rk, a_rb = {}, {}, {}
    for u, hh in halves:
        m2 = (lane2 < RWKV_N) if hh == 0 else (lane2 >= RWKV_N)
        lh = jnp.where(m2, lhs[u][0], bz(lhs[u][0]))
        ll = jnp.where(m2, lhs[u][1], bz(lhs[u][1]))
        pak = (_dot_nt(lh[:c], kts[u][0])
               + (_dot_nt(lh[:c], kts[u][1]) + _dot_nt(ll[:c], kts[u][0])))
        a_ak[u, hh] = jnp.where(strict, pak, 0.0)
        a_rk[u, hh] = jnp.where(incl, _dot_nt(lh[c:], kts[u][0]), 0.0).astype(BF16)
        a_rb[u, hh] = jnp.where(incl, _dot_nt(lh[c:], bts[u][0]), 0.0).astype(BF16)
    vps = {uh: _dot3(a_ak[uh], vs[uh[0]]) for uh in halves}
    gs = {(u, hh): _dot3(t_ref[2 * u[1] + hh, u[0]], jnp.concatenate([ats[u], vps[u, hh]], axis=1))
          for u, hh in halves}
    pick = lambda x0, x1: jnp.where(lane < RWKV_N, x0, x1)
    ahats = {u: pick(gs[u, 0][:, :pw], gs[u, 1][:, :pw]) for u in units}
    vhats = {u: pick(gs[u, 0][:, pw:], gs[u, 1][:, pw:]) for u in units}
    ahb = {u: ahats[u].astype(BF16) for u in units}
    vhb = {u: vhats[u].astype(BF16) for u in units}
    vbs = {u: vs[u].astype(BF16) for u in units}
    rhats = {u: (rts[u] + pick(_dot(a_rb[u, 0], ahb[u]), _dot(a_rb[u, 1], ahb[u]))).astype(BF16)
             for u in units}
    yhats = {u: pick(_dot(a_rb[u, 0], vhb[u]) + _dot(a_rk[u, 0], vbs[u]),
                     _dot(a_rb[u, 1], vhb[u]) + _dot(a_rk[u, 1], vbs[u])) for u in units}
    wends = {u: jnp.exp(cws[u][c - 1:c] - cws[u]) for u in units}
    bkts = {u: jnp.concatenate([at(b_ref, u) * wends[u], at(k_ref, u) * wends[u]], axis=0).T
            for u in units}
    mpts = {u: jnp.where(same_head, _dot3(bkts[u], jnp.concatenate([ahats[u], bz(ahats[u])], axis=0)), 0.0)
            + jnp.where(eye, wts[u][c - 1:c], 0.0) for u in units}
    zpts = {u: jnp.where(same_head, _dot3(bkts[u], jnp.concatenate([vhats[u], vs[u]], axis=0)), 0.0)
            for u in units}
    spts = [s_ref[p] for p in pairs]
    for q in range(RWKV_C_CHUNKS):
        for p in pairs:
            y_ref[q * c:(q + 1) * c, p * pw:(p + 1) * pw] = (
                _dot(rhats[q, p], spts[p].astype(BF16)) + yhats[q, p])
        spts = [_dot3(mpts[q, p], spts[p]) + zpts[q, p] for p in pairs]
    for p in pairs:
        s_ref[p] = spts[p]

    @pl.when(ci == pl.num_programs(1) - 1)
    def _():
        sf_ref[...] = s_ref[...]


def _rwkv_chunk(xs, t4, lmat, nseq):
    n = xs[0].shape[0]
    c = RWKV_CHUNK
    q = RWKV_C_CHUNKS
    nch = n // (q * c) // nseq
    pw = 2 * RWKV_N
    tile = pl.BlockSpec((q * c, RWKV_W), lambda b, i: (b * nch + i, 0))
    return pl.pallas_call(
        _rwkv_chunk_body, grid=(nseq, nch),
        in_specs=[tile] * 6 + [pl.BlockSpec((RWKV_HEADS, q, c, c), lambda b, i: (0, b * nch + i, 0, 0)),
                               pl.BlockSpec((c, c), lambda b, i: (0, 0))],
        out_specs=[tile, pl.BlockSpec((None, RWKV_HEADS // 2, pw, pw), lambda b, i: (b, 0, 0, 0))],
        out_shape=[jax.ShapeDtypeStruct((n, RWKV_W), F32),
                   jax.ShapeDtypeStruct((nseq, RWKV_HEADS // 2, pw, pw), F32)],
        scratch_shapes=[pltpu.VMEM((RWKV_HEADS // 2, pw, pw), F32)],
        compiler_params=_cp(2), name="rwkv_chunk")(*xs, t4, lmat)


def _hgrn_lower_bound(logits, layer):
    m = jnp.max(logits, axis=0, keepdims=True)
    e = jnp.exp(logits - m)
    gam = e / jnp.sum(e, axis=0, keepdims=True)
    cs = gam[0:1]
    for i in range(1, layer + 1):
        cs = cs + gam[i:i + 1]
    return cs - gam[0:1]


def _hgrn_level_matrix():
    t = np.arange(HGRN_CHUNK)[:, None]
    s = np.arange(HGRN_CHUNK)[None, :]
    mats = []
    for upper in (False, True):
        for e in (1, 2):
            same = (t >> e) == (s >> e)
            mats.append(same & ((s > t) if upper else (s <= t)))
    mats.append(s <= t)
    return jnp.asarray(np.concatenate(mats, axis=0).astype(np.float32), dtype=BF16)


def _hgrn_prompt_body(*refs, layer):
    npair = HGRN_HEADS // 2
    q_refs, f_refs, i_refs, g_refs = (refs[i * npair:(i + 1) * npair] for i in range(4))
    lg_ref, nw_ref, cm_ref, o_ref, sf_ref, s_ref = refs[4 * npair:]
    c = pl.program_id(1)

    @pl.when(c == 0)
    def _():
        s_ref[...] = jnp.zeros_like(s_ref)

    n = HGRN_CHUNK
    heads = range(HGRN_HEADS)
    col = lambda rs, h: rs[h // 2][:, (h % 2) * HGRN_D:(h % 2 + 1) * HGRN_D]
    hsl = lambda h: slice(h * HGRN_D, (h + 1) * HGRN_D)
    t = lax.broadcasted_iota(jnp.int32, (n, n), 0)
    s_ = lax.broadcasted_iota(jnp.int32, (n, n), 1)
    lb_all = _hgrn_lower_bound(lg_ref[...], layer)
    cm = cm_ref[...]
    ones = jnp.ones((n, n), BF16)
    fgs = [lb_all[:, hsl(h)] + (1.0 - lb_all[:, hsl(h)]) * jax.nn.sigmoid(col(f_refs, h))
           for h in heads]
    lfs = [jnp.log(fg) for fg in fgs]
    css = [_split_dot_left(cm, lf) for lf in lfs]
    bs = [cs[4 * n:5 * n] for cs in css]

    def cl(h, e):
        if e <= 2:
            return css[h][(e - 1) * n:e * n]
        if e == HGRN_LEVELS:
            return bs[h]
        m = 1 << e
        b3 = bs[h].reshape(n // m, m, HGRN_D)
        ends = jnp.broadcast_to(b3[:, m - 1:m, :], b3.shape)
        before = jnp.concatenate([jnp.zeros_like(ends[:1]), ends[:-1]], axis=0)
        return (b3 - before).reshape(n, HGRN_D)

    def cu(h, e):
        if e <= 2:
            return css[h][(2 + e - 1) * n:(2 + e) * n]
        if e == HGRN_LEVELS:
            return bs[h][n - 1:n] - bs[h]
        m = 1 << e
        b3 = bs[h].reshape(n // m, m, HGRN_D)
        return (jnp.broadcast_to(b3[:, m - 1:m, :], b3.shape) - b3).reshape(n, HGRN_D)
    qs = [col(q_refs, h) * jax.nn.sigmoid(col(q_refs, h)) for h in heads]
    ks = [1.0 - fg for fg in fgs]
    vbs = [col(i_refs, h).astype(BF16) for h in heads]
    acc = [jnp.where(t == s_, _dot_nt(qs[h].astype(BF16), ks[h].astype(BF16)), 0.0) for h in heads]
    for e in range(HGRN_LEVELS):
        mask = (((t >> e) & 1) == 1) & ((s_ >> e) == ((t >> e) - 1))
        for h in heads:
            qe = qs[h] * (fgs[h] if e == 0 else jnp.exp(cl(h, e)))
            ke = ks[h] if e == 0 else ks[h] * jnp.exp(cu(h, e))
            acc[h] = acc[h] + jnp.where(mask, _dot_nt(qe.astype(BF16), ke.astype(BF16)), 0.0)
    sts = [s_ref[h] for h in heads]
    os_ = [_dot(acc[h].astype(BF16), vbs[h])
           + _dot((qs[h] * jnp.exp(cl(h, HGRN_LEVELS))).astype(BF16), sts[h].astype(BF16))
           for h in heads]
    dks = [jnp.exp(_split_dot(lfs[h].T, ones)) for h in heads]
    for h in heads:
        kf = ks[h] * jnp.exp(cu(h, HGRN_LEVELS))
        s_ref[h] = dks[h] * sts[h] + _dot(kf.T.astype(BF16), vbs[h])
    for h in heads:
        o = os_[h]
        ms = jnp.mean(o * o, axis=-1, keepdims=True)
        gr = col(g_refs, h)
        o_ref[:, hsl(h)] = (o * lax.rsqrt(ms + NORM_EPS) * nw_ref[:, hsl(h)]) * (gr * jax.nn.sigmoid(gr))

    @pl.when(c == pl.num_programs(1) - 1)
    def _():
        sf_ref[...] = s_ref[...]


def _split_dot_left(m, x):
    hi, lo = _split(x)
    return _dot(m, hi) + _dot(m, lo)


def _hgrn_prompt(proj, logits, nw, cmat, nseq, seq_len, layer):
    n = proj.shape[0]
    nch = seq_len // HGRN_CHUNK
    wb = 2 * HGRN_D
    npair = HGRN_HEADS // 2
    cols = [pl.BlockSpec((HGRN_CHUNK, wb), lambda b, c, j=(HGRN_COL0 + g * HGRN_W) // wb + p:
                         (b * nch + c, j)) for g in range(4) for p in range(npair)]
    return pl.pallas_call(
        functools.partial(_hgrn_prompt_body, layer=layer),
        grid=(nseq, nch),
        in_specs=cols + [pl.BlockSpec(logits.shape, lambda b, c: (0, 0)),
                         pl.BlockSpec((1, HGRN_W), lambda b, c: (0, 0)),
                         pl.BlockSpec(cmat.shape, lambda b, c: (0, 0))],
        out_specs=[pl.BlockSpec((HGRN_CHUNK, HGRN_W), lambda b, c: (b * nch + c, 0)),
                   pl.BlockSpec((None, HGRN_HEADS, HGRN_D, HGRN_D), lambda b, c: (b, 0, 0, 0))],
        out_shape=[jax.ShapeDtypeStruct((n, HGRN_W), F32),
                   jax.ShapeDtypeStruct((nseq, HGRN_HEADS, HGRN_D, HGRN_D), F32)],
        scratch_shapes=[pltpu.VMEM((HGRN_HEADS, HGRN_D, HGRN_D), F32)],
        compiler_params=_cp(2), name="hgrn_prompt")(*([proj] * (4 * npair)), logits, nw, cmat)


HGRN_KT = 32


def _hgrn_lanes_body(q_ref, f_ref, i_ref, g_ref, lg_ref, nw_ref, s0_ref, o_ref, sf_ref,
                     qt_ref, ft_ref, kt_ref, vt_ref, oacc_ref, *, layer):
    kt = pl.program_id(1)
    lb = _hgrn_lower_bound(lg_ref[...], layer)

    @pl.when(kt == 0)
    def _():
        oacc_ref[...] = jnp.zeros_like(oacc_ref)
        for tl in range(SUBLANES):
            qr = q_ref[tl]
            fg = lb + (1.0 - lb) * jax.nn.sigmoid(f_ref[tl])
            qt_ref[tl] = (qr * jax.nn.sigmoid(qr)).T
            ft_ref[tl] = fg.T
            kt_ref[tl] = (1.0 - fg).T
            vt_ref[tl] = i_ref[tl].T

    def step(tl, carry, src_ref=sf_ref):
        vt = vt_ref[tl]

        def krow(kk, o):
            row = kt * HGRN_KT + kk
            s = ft_ref[tl, pl.ds(row, 1), :] * src_ref[0, kk] + kt_ref[tl, pl.ds(row, 1), :] * vt
            sf_ref[0, kk] = s
            return o + qt_ref[tl, pl.ds(row, 1), :] * s

        oacc_ref[tl] = lax.fori_loop(0, HGRN_KT, krow, oacc_ref[tl], unroll=2)
        return carry

    step(0, 0, s0_ref)
    lax.fori_loop(1, SUBLANES, step, 0)

    @pl.when(kt == pl.num_programs(1) - 1)
    def _():
        for tl in range(SUBLANES):
            o = oacc_ref[tl].T
            ms = jnp.mean(o * o, axis=-1, keepdims=True)
            gr = g_ref[tl]
            o_ref[tl] = (o * lax.rsqrt(ms + NORM_EPS) * nw_ref[...]) * (gr * jax.nn.sigmoid(gr))


def _hgrn_lanes(proj_t, logits, nw, s0, layer):
    steps, ninst, _ = proj_t.shape
    col = lambda off: pl.BlockSpec((steps, ninst, HGRN_D),
                                   lambda h, k, off=off: (0, 0, off // HGRN_D + h))
    sspec = pl.BlockSpec((1, HGRN_KT, HGRN_D, ninst), lambda h, k: (h, k, 0, 0))
    tbuf = pltpu.VMEM((steps, HGRN_D, ninst), F32)
    return pl.pallas_call(
        functools.partial(_hgrn_lanes_body, layer=layer),
        grid=(HGRN_HEADS, HGRN_D // HGRN_KT),
        in_specs=[col(0), col(HGRN_W), col(2 * HGRN_W), col(3 * HGRN_W),
                  pl.BlockSpec((logits.shape[0], HGRN_D), lambda h, k: (0, h)),
                  pl.BlockSpec((1, HGRN_D), lambda h, k: (0, h)), sspec],
        out_specs=[pl.BlockSpec((steps, ninst, HGRN_D), lambda h, k: (0, 0, h)), sspec],
        out_shape=[jax.ShapeDtypeStruct((steps, ninst, HGRN_W), F32),
                   jax.ShapeDtypeStruct(s0.shape, F32)],
        scratch_shapes=[tbuf, tbuf, tbuf, tbuf, tbuf],
        compiler_params=_cp(2), name="hgrn_lanes")(
            proj_t, proj_t, proj_t, proj_t, logits, nw, s0)


def _rglru_body(xb_ref, gate_ref, chist_ref, hinit_ref, cw_ref, cb_ref, wa_ref, ba_ref, wx_ref,
                bx_ref, lam_ref, y_ref, cst_ref, hst_ref, ccarry_ref, hcarry_ref,
                *, long_seq, tps):
    x = xb_ref[...]
    tm, c = x.shape
    if long_seq:
        @pl.when(pl.program_id(0) % tps == 0)
        def _():
            ccarry_ref[...] = chist_ref[...]
            hcarry_ref[...] = hinit_ref[...]
        hist = ccarry_ref[...]
    else:
        hist = chist_ref[...]
    p1, p2, p3 = _prev_rows(x, hist, (1, 2, 3), long_seq)
    cw = cw_ref[...]
    xc = cb_ref[...] + cw[0:1] * p3 + cw[1:2] * p2 + cw[2:3] * p1 + cw[3:4] * x
    xcb = xc.astype(BF16)
    r = jax.nn.sigmoid(_dot(xcb, wa_ref[...]) + ba_ref[...])
    ig = jax.nn.sigmoid(_dot(xcb, wx_ref[...]) + bx_ref[...])
    log_a = (-LRU_C) * r * _softplus(-lam_ref[...])
    a = jnp.exp(log_a)
    h = jnp.sqrt(-_expm1(2.0 * log_a)) * (ig * xc)
    if long_seq:
        pos = lax.broadcasted_iota(jnp.int32, (tm, c), 0)
        k = 1
        while k < tm:
            keep = pos >= k
            h = h + a * jnp.where(keep, pltpu.roll(h, k, 0), 0.0)
            a = a * jnp.where(keep, pltpu.roll(a, k, 0), 1.0)
            k *= 2
        h = h + a * hcarry_ref[SUBLANES - 1:SUBLANES, :]
        hcarry_ref[...] = h[tm - SUBLANES:]
        ccarry_ref[...] = x[tm - SUBLANES:]
        cst_ref[...] = x[tm - SUBLANES:]
        hst_ref[...] = h[tm - SUBLANES:]
    else:
        shp = (tm // SUBLANES, SUBLANES, c)
        h3, a3 = h.reshape(shp), a.reshape(shp)
        pos = lax.broadcasted_iota(jnp.int32, shp, 1)
        k = 1
        while k < SUBLANES:
            keep = pos >= k
            h3 = h3 + a3 * jnp.where(keep, pltpu.roll(h3, k, 1), 0.0)
            a3 = a3 * jnp.where(keep, pltpu.roll(a3, k, 1), 1.0)
            k *= 2
        h = (h3 + a3 * hinit_ref[...].reshape(shp)).reshape(tm, c)
        cst_ref[...] = x
        hst_ref[...] = h
    y_ref[...] = h * jax.nn.gelu(gate_ref[...])


def _rglru(proj, chist, hinit, lp, seq_len, tm):
    n = proj.shape[0]
    long_seq = seq_len > SUBLANES
    tps = max(seq_len // tm, 1)
    c0 = LRU_COL0 // LRU_W
    if long_seq:
        hs = pl.BlockSpec((SUBLANES, LRU_W), lambda i: (i // tps, 0))
    else:
        hs = pl.BlockSpec((tm, LRU_W), lambda i: (i, 0))
    row = pl.BlockSpec((1, LRU_W), lambda i: (0, 0))
    sq = pl.BlockSpec((LRU_W, LRU_W), lambda i: (0, 0))
    return pl.pallas_call(
        functools.partial(_rglru_body, long_seq=long_seq, tps=tps), grid=(n // tm,),
        in_specs=[pl.BlockSpec((tm, LRU_W), lambda i: (i, c0)),
                  pl.BlockSpec((tm, LRU_W), lambda i: (i, c0 + 1)), hs, hs,
                  pl.BlockSpec((4, LRU_W), lambda i: (0, 0)), row, sq, row, sq, row, row],
        out_specs=[pl.BlockSpec((tm, LRU_W), lambda i: (i, 0)), hs, hs],
        out_shape=[jax.ShapeDtypeStruct((n, LRU_W), F32),
                   jax.ShapeDtypeStruct(chist.shape, F32),
                   jax.ShapeDtypeStruct(chist.shape, F32)],
        scratch_shapes=[pltpu.VMEM((SUBLANES, LRU_W), F32), pltpu.VMEM((SUBLANES, LRU_W), F32)],
        compiler_params=_cp(1), name="rglru")(
            proj, proj, chist, hinit, lp["lru_cw"], lp["lru_cb"], lp["lru_wa"], lp["lru_ba"],
            lp["lru_wx"], lp["lru_bx"], lp["lru_lam"])


def _hist(state):
    nseq, k, c = state.shape
    return jnp.pad(state, ((0, 0), (SUBLANES - k, 0), (0, 0))).reshape(nseq * SUBLANES, c)


def _block_diag(w):
    h, a, b = w.shape
    eye = jnp.eye(h, dtype=w.dtype)
    return (eye[:, None, :, None] * w[:, :, None, :]).reshape(h * a, h * b)


def _layer(x3, st, lp, consts, layer, final_nw, w_in_at):
    nseq, seq_len, d = x3.shape
    n = nseq * seq_len
    long_seq = seq_len > SUBLANES
    s_rw, shift_rw, s_hg, h_lru, buf_lru, buf_ffn = st
    x = x3.reshape(n, d)
    ones = consts["ones"]

    w_in, w_in_layer = w_in_at
    proj, w_in_bf16 = _in_proj(x, lp["norm_mix"], w_in, w_in_layer, tm=min(n, 1024),
                               tn=IN_COLS // 4 if w_in.dtype == BF16 else 512)

    tm_pre = min(n, seq_len if long_seq else n, 256)
    lmat_pre = None
    if long_seq:
        tril = np.tril(np.ones((RWKV_CHUNK, RWKV_CHUNK)))
        lmat_pre = jnp.asarray(np.kron(np.eye(tm_pre // RWKV_CHUNK), tril), dtype=BF16)
    outs = _rwkv_pre(proj, _hist(shift_rw[:, None, :]), lp, ones, seq_len, tm_pre, lmat_pre)
    r, w, k, v, na, kb, g, bonus, shift_out = outs[:9]
    new_shift = shift_out.reshape(nseq, SUBLANES, RWKV_COLS)[:, SUBLANES - 1]
    if long_seq:
        c = RWKV_CHUNK
        t4 = _tri_solve(outs[9].transpose(0, 2, 3, 1)).transpose(0, 3, 1, 2)
        y, spt = _rwkv_chunk((r, w, k, v, na, kb), t4, consts["lmat"], nseq)
        sp = spt.reshape(nseq, RWKV_HEADS // 2, 2, RWKV_N, 2, RWKV_N)
        new_s_rw = jnp.stack([sp[:, :, 0, :, 0, :], sp[:, :, 1, :, 1, :]], axis=2)
        new_s_rw = new_s_rw.reshape(nseq, RWKV_HEADS, RWKV_N, RWKV_N).transpose(0, 1, 3, 2)
    else:
        tmaj = lambda t: t.reshape(nseq, seq_len, RWKV_W).transpose(1, 0, 2)
        s0 = s_rw.transpose(1, 2, 3, 0)
        yb, sf = _rwkv_lanes([tmaj(t) for t in (r, w, k, v, na, kb)], s0)
        y = yb.transpose(1, 0, 2).reshape(n, RWKV_W)
        new_s_rw = sf.transpose(3, 0, 1, 2)
    y_rw = _rwkv_post(y, bonus, g, lp["ln_w"], lp["ln_b"], ones, tm=min(n, 512))

    if long_seq:
        y_hg, new_s_hg = _hgrn_prompt(proj, consts["lb_logits"], lp["hgrn_nw"], consts["cmat"],
                                      nseq, seq_len, layer)
    else:
        proj_t = proj[:, HGRN_COL0:LRU_COL0].reshape(nseq, seq_len, 4 * HGRN_W).transpose(1, 0, 2)
        s0 = s_hg.transpose(1, 2, 3, 0)
        o_t, sf = _hgrn_lanes(proj_t, consts["lb_logits"], lp["hgrn_nw"], s0, layer)
        y_hg = o_t.transpose(1, 0, 2).reshape(n, HGRN_W)
        new_s_hg = sf.transpose(3, 0, 1, 2)

    if long_seq:
        hinit = _hist(h_lru[:, None, :])
        tm_lru = min(seq_len, 256)
    else:
        hinit = jnp.broadcast_to(h_lru[:, None, :], (nseq, seq_len, LRU_W)).reshape(n, LRU_W)
        tm_lru = min(n, 512)
    y_lru, cst, hst = _rglru(proj, _hist(buf_lru), hinit, lp, seq_len, tm_lru)
    new_buf_lru = cst.reshape(nseq, SUBLANES, LRU_W)[:, SUBLANES - 3:]
    new_h_lru = hst.reshape(nseq, SUBLANES, LRU_W)[:, SUBLANES - 1]

    x = _out_proj(x, y_rw, y_hg, y_lru, consts["w_out"], layer, tm=min(n, 1024), tn=512)
    tm_ffn = min(n, seq_len if long_seq else n, 1024)
    outs = _ffn(x, lp["norm_ffn"], consts["w_up"], lp["ffn_cw"], lp["ffn_cb"], consts["w_down"],
                _hist(buf_ffn), layer, seq_len, tm=tm_ffn, tf=512, final_nw=final_nw)
    x, fst = outs
    new_buf_ffn = fst.reshape(nseq, -1, SUBLANES, D_FF)[:, -1, SUBLANES - 2:]
    states = (new_s_rw, new_shift, new_s_hg, new_h_lru, new_buf_lru, new_buf_ffn)
    return x.reshape(nseq, seq_len, d), states, w_in_bf16


def kernel(x_prompt, x_sample, state_rwkv, state_rwkv_shift, state_hgrn, state_rglru, cache_rglru_conv, cache_ffn_conv, norm_mix, w_in, rwkv_mu, rwkv_w0, rwkv_w2, rwkv_a0, rwkv_a2, rwkv_g2, rwkv_k_k, rwkv_k_a, rwkv_r_k, rwkv_ln_w, rwkv_ln_b, hgrn_lb_logits, hgrn_norm_w, rglru_conv_w, rglru_conv_b, rglru_wa, rglru_ba, rglru_wx, rglru_bx, rglru_lambda, w_out, norm_ffn, ffn_w_up, ffn_conv_w, ffn_conv_b, ffn_w_down, norm_final):
    depth = w_in.shape[0]
    nb, nt, _ = x_prompt.shape
    db = x_sample.shape[0]
    consts = {
        "ones": jnp.asarray(np.kron(np.eye(RWKV_HEADS), np.ones((RWKV_N, RWKV_N))), dtype=BF16),
        "cmat": _hgrn_level_matrix(),
        "lmat": jnp.asarray(np.tril(np.ones((RWKV_CHUNK, RWKV_CHUNK), np.float32)), dtype=BF16),
        "lb_logits": hgrn_lb_logits,
        "w_out": w_out.astype(BF16),
        "w_up": ffn_w_up.astype(BF16), "w_down": ffn_w_down.astype(BF16),
    }
    rowv = lambda a: a.reshape(1, -1)
    x_p, x_s = x_prompt, x_sample
    new_p, new_s = [], []
    for l in range(depth):
        lp = {
            "norm_mix": rowv(norm_mix[l]), "mu": rowv(rwkv_mu[l]),
            "w0": rowv(rwkv_w0[l]),
            "w2p": jnp.pad(rwkv_w2[l], ((0, 64), (0, 0))).astype(BF16),
            "a0": rowv(rwkv_a0[l]),
            "a2p": jnp.pad(rwkv_a2[l], ((64, 0), (0, 0))).astype(BF16),
            "g2": rwkv_g2[l].astype(BF16), "k_k": rowv(rwkv_k_k[l]), "k_a": rowv(rwkv_k_a[l]),
            "r_k": rowv(rwkv_r_k[l]), "ln_w": rowv(rwkv_ln_w[l]), "ln_b": rowv(rwkv_ln_b[l]),
            "hgrn_nw": rowv(hgrn_norm_w[l]),
            "lru_cw": rglru_conv_w[l], "lru_cb": rowv(rglru_conv_b[l]),
            "lru_wa": _block_diag(rglru_wa[l]).astype(BF16), "lru_ba": rowv(rglru_ba[l]),
            "lru_wx": _block_diag(rglru_wx[l]).astype(BF16), "lru_bx": rowv(rglru_bx[l]),
            "lru_lam": rowv(rglru_lambda[l]),
            "norm_ffn": rowv(norm_ffn[l]), "ffn_cw": ffn_conv_w[l],
            "ffn_cb": rowv(ffn_conv_b[l]),
        }
        zero = lambda *s: jnp.zeros(s, F32)
        st_p = (None, zero(nb, RWKV_COLS), None, zero(nb, LRU_W), zero(nb, 3, LRU_W),
                zero(nb, 2, D_FF))
        final_nw = rowv(norm_final) if l == depth - 1 else None
        st_s = (state_rwkv[l], state_rwkv_shift[l], state_hgrn[l], state_rglru[l],
                cache_rglru_conv[l], cache_ffn_conv[l])
        x_s, ss, w_in_l = _layer(x_s, st_s, lp, consts, l, final_nw, (w_in, l))
        x_p, sp, _ = _layer(x_p, st_p, lp, consts, l, final_nw, (w_in_l, 0))
        new_p.append(sp)
        new_s.append(ss)
    stack = lambda sts: [jnp.stack(s, axis=0) for s in zip(*sts)]
    return (x_p, x_s, *stack(new_p), *stack(new_s))
```

```python
import functools

import numpy as np
import jax
import jax.numpy as jnp
from jax import lax
from jax.experimental import pallas as pl
from jax.experimental.pallas import tpu as pltpu

F32 = jnp.float32
BF16 = jnp.bfloat16

D_MODEL = 2048
RWKV_HEADS = 12
RWKV_N = 64
RWKV_W = RWKV_HEADS * RWKV_N
RWKV_COLS = 2560
RWKV_LN_EPS = 64e-5
HGRN_HEADS = 6
HGRN_D = 128
HGRN_W = HGRN_HEADS * HGRN_D
HGRN_COL0 = RWKV_COLS
LRU_W = 512
LRU_COL0 = RWKV_COLS + 4 * HGRN_W
LRU_C = 8.0
IN_COLS = 6656
D_FF = 5632
NORM_EPS = 1e-6

SUBLANES = 8
LANES = 128
MXU_WIDTH = 256
RWKV_CHUNK = 64
HGRN_CHUNK = 128
HGRN_LEVELS = 7
VMEM_LIMIT = 48 * 1024 * 1024
BIG_VMEM_LIMIT = 56 * 1024 * 1024


def _cp(n, limit=VMEM_LIMIT):
    return pltpu.CompilerParams(dimension_semantics=("arbitrary",) * n, vmem_limit_bytes=limit)


def _dot(a, b):
    return jnp.dot(a, b, preferred_element_type=F32)


def _dot_nt(a, b):
    return lax.dot_general(a, b, (((1,), (1,)), ((), ())), preferred_element_type=F32)


def _split(x):
    hi = x.astype(BF16)
    lo = (x - hi.astype(F32)).astype(BF16)
    return hi, lo


def _split_dot(x, m):
    hi, lo = _split(x)
    return _dot(hi, m) + _dot(lo, m)


def _dot3(a, b, nt=False):
    f = _dot_nt if nt else _dot
    ah, al = _split(a)
    bh, bl = _split(b)
    return f(ah, bh) + (f(ah, bl) + f(al, bh))


def _softplus(z):
    return jnp.maximum(z, 0.0) + jnp.log1p(jnp.exp(-jnp.abs(z)))


def _expm1(z):
    return jnp.tanh(0.5 * z) * (jnp.exp(z) + 1.0)


def _prev_rows(x, hist, ks, long_seq):
    tm, c = x.shape
    if long_seq:
        ext = jnp.concatenate([hist, x], axis=0)
        return [pltpu.roll(ext, k, 0)[SUBLANES:] for k in ks]
    x3 = x.reshape(tm // SUBLANES, SUBLANES, c)
    h3 = hist.reshape(tm // SUBLANES, SUBLANES, c)
    pos = lax.broadcasted_iota(jnp.int32, x3.shape, 1)
    return [jnp.where(pos >= k, pltpu.roll(x3, k, 1), pltpu.roll(h3, k, 1)).reshape(tm, c)
            for k in ks]


def _in_proj_body(x_ref, nw_ref, w_ref, o_ref, *rest):
    xn_ref = rest[-1]

    @pl.when(pl.program_id(1) == 0)
    def _():
        x = x_ref[...]
        ms = jnp.mean(x * x, axis=-1, keepdims=True)
        xn_ref[...] = (x * lax.rsqrt(ms + NORM_EPS) * nw_ref[...]).astype(BF16)
    w = w_ref[...].astype(BF16)
    if len(rest) == 2:
        rest[0][...] = w
    o_ref[...] = _dot(xn_ref[...], w)


def _in_proj(x, nw, w, layer, tm, tn):
    n, d = x.shape
    c = w.shape[2]
    emit = w.dtype != BF16
    assert not emit or n == tm
    out_specs = [pl.BlockSpec((tm, tn), lambda i, j: (i, j))]
    out_shape = [jax.ShapeDtypeStruct((n, c), F32)]
    if emit:
        out_specs.append(pl.BlockSpec((None, d, tn), lambda i, j: (0, 0, j)))
        out_shape.append(jax.ShapeDtypeStruct((1, d, c), BF16))
    outs = pl.pallas_call(
        _in_proj_body, grid=(n // tm, c // tn),
        in_specs=[pl.BlockSpec((tm, d), lambda i, j: (i, 0)),
                  pl.BlockSpec((1, d), lambda i, j: (0, 0)),
                  pl.BlockSpec((None, d, tn), lambda i, j: (layer, 0, j))],
        out_specs=out_specs, out_shape=out_shape,
        scratch_shapes=[pltpu.VMEM((tm, d), BF16)],
        compiler_params=_cp(2, BIG_VMEM_LIMIT), name="in_proj")(x, nw, w)
    return outs if emit else (outs[0], None)


def _out_proj_body(x_ref, ya_ref, yb_ref, yc_ref, w_ref, o_ref, y_ref):
    @pl.when(pl.program_id(1) == 0)
    def _():
        y_ref[:, 0:RWKV_W] = ya_ref[...].astype(BF16)
        y_ref[:, RWKV_W:RWKV_W + HGRN_W] = yb_ref[...].astype(BF16)
        y_ref[:, RWKV_W + HGRN_W:] = yc_ref[...].astype(BF16)
    o_ref[...] = x_ref[...] + _dot(y_ref[...], w_ref[...])


def _out_proj(x, ya, yb, yc, w, layer, tm, tn):
    n, d = x.shape
    return pl.pallas_call(
        _out_proj_body, grid=(n // tm, d // tn),
        in_specs=[pl.BlockSpec((tm, tn), lambda i, j: (i, j)),
                  pl.BlockSpec((tm, RWKV_W), lambda i, j: (i, 0)),
                  pl.BlockSpec((tm, HGRN_W), lambda i, j: (i, 0)),
                  pl.BlockSpec((tm, LRU_W), lambda i, j: (i, 0)),
                  pl.BlockSpec((None, d, tn), lambda i, j: (layer, 0, j))],
        out_specs=pl.BlockSpec((tm, tn), lambda i, j: (i, j)),
        out_shape=jax.ShapeDtypeStruct((n, d), F32),
        scratch_shapes=[pltpu.VMEM((tm, d), BF16)],
        compiler_params=_cp(2), name="out_proj")(x, ya, yb, yc, w)


def _ffn_body(*refs, long_seq, tps, final, emit):
    x_ref, nw_ref, wg_ref, wv_ref, cw_ref, cb_ref, wd_ref, hist_ref = refs[:8]
    refs = refs[8:]
    if final:
        fnw_ref, refs = refs[0], refs[1:]
    o_ref, st_ref = refs[:2]
    refs = refs[2:]
    if emit:
        wgb_ref, wvb_ref, wdb_ref = refs[:3]
        refs = refs[3:]
    hn_ref, carry_ref = refs
    i = pl.program_id(0)
    j = pl.program_id(1)

    @pl.when(j == 0)
    def _():
        x = x_ref[...]
        ms = jnp.mean(x * x, axis=-1, keepdims=True)
        hn_ref[...] = (x * lax.rsqrt(ms + NORM_EPS) * nw_ref[...]).astype(BF16)
        o_ref[...] = x

    hn = hn_ref[...]
    if long_seq:
        @pl.when(i % tps == 0)
        def _():
            carry_ref[j] = hist_ref[...]
        hist = carry_ref[j]
    else:
        hist = hist_ref[...]
    tf = wg_ref.shape[1]
    nsplit = min(2, max(1, tf // MXU_WIDTH))
    halves = [slice(q * (tf // nsplit), (q + 1) * (tf // nsplit)) for q in range(nsplit)]
    wgs = [wg_ref[:, sl].astype(BF16) for sl in halves]
    wvs = [wv_ref[:, sl].astype(BF16) for sl in halves]
    wds = [wd_ref[sl, :].astype(BF16) for sl in halves]
    if emit:
        for sl, wg, wv, wd in zip(halves, wgs, wvs, wds):
            wgb_ref[:, sl] = wg
            wvb_ref[:, sl] = wv
            wdb_ref[sl, :] = wd
    gs = [_dot(hn, wg) for wg in wgs]
    vs = [_dot(hn, wv) for wv in wvs]
    cw = cw_ref[...]
    acc = None
    for sl, g, v, wd in zip(halves, gs, vs, wds):
        p1, p2 = _prev_rows(g, hist[:, sl], (1, 2), long_seq)
        gc = cb_ref[:, sl] + cw[0:1, sl] * p2 + cw[1:2, sl] * p1 + cw[2:3, sl] * g
        h = (gc * jax.nn.sigmoid(gc)) * v
        d = _dot(h.astype(BF16), wd)
        acc = d if acc is None else acc + d
        tail = g if not long_seq else g[g.shape[0] - SUBLANES:]
        st_ref[:, sl] = tail
        if long_seq:
            carry_ref[j, :, sl] = tail
    o_ref[...] += acc

    if final:
        @pl.when(j == pl.num_programs(1) - 1)
        def _():
            y = o_ref[...]
            ms = jnp.mean(y * y, axis=-1, keepdims=True)
            o_ref[...] = y * lax.rsqrt(ms + NORM_EPS) * fnw_ref[...]


def _ffn(x, nw, weights, cw, cb, hist, seq_len, tm, tf, final_nw=None):
    n, d = x.shape
    final = final_nw is not None
    (wg, wg_l, wg_off), (wv, wv_l, wv_off), (wd, wd_l) = weights
    emit = wg.dtype != BF16
    assert not emit or n == tm
    nf = D_FF // tf
    long_seq = seq_len > SUBLANES
    tps = max(seq_len // tm, 1)
    if long_seq:
        hist_spec = pl.BlockSpec((SUBLANES, tf), lambda i, j: (i // tps, j))
        st_spec = pl.BlockSpec((SUBLANES, tf), lambda i, j: (i, j))
        st_rows = n // tm * SUBLANES
    else:
        hist_spec = pl.BlockSpec((tm, tf), lambda i, j: (i, j))
        st_spec = pl.BlockSpec((tm, tf), lambda i, j: (i, j))
        st_rows = n
    body = functools.partial(_ffn_body, long_seq=long_seq, tps=tps, final=final, emit=emit)
    row_tile = pl.BlockSpec((tm, d), lambda i, j: (i, 0))
    wide = pl.BlockSpec((1, d), lambda i, j: (0, 0))
    x_tile = (pl.BlockSpec((tm, d), lambda i, j: (i, 0), pipeline_mode=pl.Buffered(1))
              if emit else row_tile)
    in_specs = [x_tile, wide,
                pl.BlockSpec((None, d, tf), lambda i, j: (wg_l, 0, wg_off // tf + j)),
                pl.BlockSpec((None, d, tf), lambda i, j: (wv_l, 0, wv_off // tf + j)),
                pl.BlockSpec((3, tf), lambda i, j: (0, j)),
                pl.BlockSpec((1, tf), lambda i, j: (0, j)),
                pl.BlockSpec((None, tf, d), lambda i, j: (wd_l, j, 0)),
                hist_spec]
    args = [x, nw, wg, wv, cw, cb, wd, hist]
    out_specs = [row_tile, st_spec]
    out_shape = [jax.ShapeDtypeStruct((n, d), F32), jax.ShapeDtypeStruct((st_rows, D_FF), F32)]
    scratch = [pltpu.VMEM((tm, d), BF16), pltpu.VMEM((nf, SUBLANES, tf), F32)]
    if emit:
        up_tile = pl.BlockSpec((None, d, tf), lambda i, j: (0, 0, j))
        out_specs += [up_tile, up_tile, pl.BlockSpec((None, tf, d), lambda i, j: (0, j, 0))]
        out_shape += [jax.ShapeDtypeStruct((1, d, D_FF), BF16)] * 2 + [
            jax.ShapeDtypeStruct((1, D_FF, d), BF16)]
    if final:
        in_specs.append(wide)
        args.append(final_nw)
    return pl.pallas_call(
        body, grid=(n // tm, nf), in_specs=in_specs, out_specs=out_specs, out_shape=out_shape,
        scratch_shapes=scratch,
        compiler_params=pltpu.CompilerParams(dimension_semantics=("arbitrary",) * 2,
                                             vmem_limit_bytes=BIG_VMEM_LIMIT),
        name="ffn")(*args)


def _rwkv_pre_body(*refs, long_seq, tps, with_n):
    (p_ref, hist_ref, mu_ref, w0_ref, w2_ref, a0_ref, a2_ref, g2_ref, kk_ref, ka_ref, rk_ref,
     ones_ref) = refs[:12]
    refs = refs[12:]
    if with_n:
        l_ref, refs = refs[0], refs[1:]
    r_ref, w_ref, k_ref, v_ref, na_ref, kb_ref, g_ref, bonus_ref, st_ref = refs[:9]
    refs = refs[9:]
    if with_n:
        n_ref, refs = refs[0], refs[1:]
    (carry_ref,) = refs
    p = p_ref[...]
    if long_seq:
        @pl.when(pl.program_id(0) % tps == 0)
        def _():
            carry_ref[...] = hist_ref[...]
        hist = carry_ref[...]
    else:
        hist = hist_ref[...]
    (prev,) = _prev_rows(p, hist, (1,), long_seq)
    if long_seq:
        carry_ref[...] = p[p.shape[0] - SUBLANES:]
        st_ref[...] = p[p.shape[0] - SUBLANES:]
    else:
        st_ref[...] = p
    xs = p + (prev - p) * mu_ref[...]
    r = xs[:, 0:RWKV_W]
    k = xs[:, RWKV_W:2 * RWKV_W]
    v = xs[:, 2 * RWKV_W:3 * RWKV_W]
    xwa = xs[:, 3 * RWKV_W:3 * RWKV_W + 128]
    xg = xs[:, 3 * RWKV_W + 128:RWKV_COLS]
    zw = w0_ref[...] + _dot(jnp.tanh(xwa).astype(BF16), w2_ref[...])
    w_log = -_softplus(-zw) - 0.5
    decay = jnp.exp(-jnp.exp(w_log))
    a = jax.nn.sigmoid(a0_ref[...] + _dot(xwa.astype(BF16), a2_ref[...]))
    g = _dot(jax.nn.sigmoid(xg).astype(BF16), g2_ref[...])
    ones = ones_ref[...]
    kk = k * kk_ref[...]
    kk = kk / jnp.maximum(jnp.sqrt(_split_dot(kk * kk, ones)), 1e-12)
    kf = k * (1.0 + (a - 1.0) * ka_ref[...])
    r_ref[...] = r
    w_ref[...] = decay
    k_ref[...] = kf
    v_ref[...] = v
    na_ref[...] = -kk
    kb_ref[...] = kk * a
    g_ref[...] = g
    bonus_ref[...] = _split_dot(r * kf * rk_ref[...], ones) * v
    if with_n:
        c = RWKV_CHUNK
        pw = 2 * RWKV_N
        lw = jnp.log(decay)
        cw = _split_dot_left(l_ref[...], lw)
        at = _split(-kk * jnp.exp(cw - lw))
        bt = _split(kk * a * jnp.exp(-cw))
        strict = (lax.broadcasted_iota(jnp.int32, (c, c), 1)
                  < lax.broadcasted_iota(jnp.int32, (c, c), 0))
        lane = lax.broadcasted_iota(jnp.int32, (p.shape[0], pw), 1)
        for pr in range(RWKV_HEADS // 2):
            sl = slice(pr * pw, (pr + 1) * pw)
            for hh in range(2):
                m = (lane < RWKV_N) if hh == 0 else (lane >= RWKV_N)
                ah = jnp.where(m, at[0][:, sl], jnp.zeros_like(at[0][:, sl]))
                al = jnp.where(m, at[1][:, sl], jnp.zeros_like(at[1][:, sl]))
                for ci in range(p.shape[0] // c):
                    rs = slice(ci * c, (ci + 1) * c)
                    bh, bl = bt[0][rs, sl], bt[1][rs, sl]
                    nm = _dot_nt(ah[rs], bh) + (_dot_nt(ah[rs], bl) + _dot_nt(al[rs], bh))
                    n_ref[2 * pr + hh, ci] = jnp.where(strict, nm, 0.0)


def _rwkv_pre(proj, hist, lp, ones, seq_len, tm, lmat=None):
    n = proj.shape[0]
    long_seq = seq_len > SUBLANES
    tps = max(seq_len // tm, 1)
    with_n = lmat is not None
    row = lambda c: pl.BlockSpec((1, c), lambda i: (0, 0))
    full = lambda a, b: pl.BlockSpec((a, b), lambda i: (0, 0))
    if long_seq:
        hist_spec = pl.BlockSpec((SUBLANES, RWKV_COLS), lambda i: (i // tps, 0))
    else:
        hist_spec = pl.BlockSpec((tm, RWKV_COLS), lambda i: (i, 0))
    tile = pl.BlockSpec((tm, RWKV_W), lambda i: (i, 0))
    body = functools.partial(_rwkv_pre_body, long_seq=long_seq, tps=tps, with_n=with_n)
    in_specs = [pl.BlockSpec((tm, RWKV_COLS), lambda i: (i, 0)), hist_spec,
                row(RWKV_COLS), row(RWKV_W), full(128, RWKV_W), row(RWKV_W),
                full(128, RWKV_W), full(128, RWKV_W), row(RWKV_W), row(RWKV_W), row(RWKV_W),
                full(RWKV_W, RWKV_W)]
    args = [proj, hist, lp["mu"], lp["w0"], lp["w2p"], lp["a0"], lp["a2p"], lp["g2"],
            lp["k_k"], lp["k_a"], lp["r_k"], ones]
    out_specs = [tile] * 8 + [hist_spec]
    out_shape = [jax.ShapeDtypeStruct((n, RWKV_W), F32)] * 8 + [jax.ShapeDtypeStruct(hist.shape, F32)]
    if with_n:
        c = RWKV_CHUNK
        in_specs.append(full(tm, tm))
        args.append(lmat)
        out_specs.append(pl.BlockSpec((RWKV_HEADS, tm // c, c, c), lambda i: (0, i, 0, 0)))
        out_shape.append(jax.ShapeDtypeStruct((RWKV_HEADS, n // c, c, c), F32))
    return pl.pallas_call(
        body, grid=(n // tm,), in_specs=in_specs, out_specs=out_specs, out_shape=out_shape,
        scratch_shapes=[pltpu.VMEM((SUBLANES, RWKV_COLS), F32)],
        compiler_params=_cp(1), name="rwkv_pre")(*args)


def _rwkv_post_body(y_ref, bonus_ref, g_ref, lw_ref, lb_ref, ones_ref, o_ref):
    y = y_ref[...]
    ones = ones_ref[...]
    mean = _split_dot(y, ones) * (1.0 / RWKV_N)
    d = y - mean
    var = _split_dot(d * d, ones) * (1.0 / RWKV_N)
    yn = d * lax.rsqrt(var + RWKV_LN_EPS)
    o_ref[...] = (yn * lw_ref[...] + lb_ref[...] + bonus_ref[...]) * g_ref[...]


def _rwkv_post(y, bonus, g, lw, lb, ones, tm):
    n = y.shape[0]
    tile = pl.BlockSpec((tm, RWKV_W), lambda i: (i, 0))
    row = pl.BlockSpec((1, RWKV_W), lambda i: (0, 0))
    return pl.pallas_call(
        _rwkv_post_body, grid=(n // tm,),
        in_specs=[tile, tile, tile, row, row, pl.BlockSpec((RWKV_W, RWKV_W), lambda i: (0, 0))],
        out_specs=tile, out_shape=jax.ShapeDtypeStruct((n, RWKV_W), F32),
        compiler_params=_cp(1), name="rwkv_post")(y, bonus, g, lw, lb, ones)


def _rwkv_lanes_body(r_ref, w_ref, k_ref, v_ref, a_ref, b_ref, s0_ref, y_ref, sf_ref,
                     s_ref, xt_ref, y_buf):
    tb = pl.program_id(1)
    ninst = r_ref.shape[1]

    @pl.when(tb == 0)
    def _():
        for hh in range(2):
            s_ref[hh] = s0_ref[hh]

    def step(tl, carry):
        for idx, ref in enumerate((r_ref, w_ref, k_ref, v_ref, a_ref, b_ref)):
            xt_ref[idx] = ref[tl].T
        for hh in range(2):
            base = hh * RWKV_N
            heads = lambda i: xt_ref[i, base:base + RWKV_N, :]

            def row(rho, c):
                s = s_ref[hh, rho]
                sa = jnp.sum(s * heads(4), axis=0, keepdims=True)
                vrow = xt_ref[3, pl.ds(base + rho, 1), :]
                s = s * heads(1) + sa * heads(5) + vrow * heads(2)
                s_ref[hh, rho] = s
                y_buf[pl.ds(base + rho, 1), :] = jnp.sum(s * heads(0), axis=0, keepdims=True)
                return c

            lax.fori_loop(0, RWKV_N, row, 0, unroll=8)
        y_ref[tl] = y_buf[...].T
        return carry

    lax.fori_loop(0, SUBLANES, step, 0)

    @pl.when(tb == pl.num_programs(1) - 1)
    def _():
        for hh in range(2):
            sf_ref[hh] = s_ref[hh]


def _rwkv_lanes(xs, s0):
    steps, ninst, _ = xs[0].shape
    xspec = pl.BlockSpec((SUBLANES, ninst, 2 * RWKV_N), lambda h, t: (t, 0, h))
    sspec = pl.BlockSpec((2, RWKV_N, RWKV_N, ninst), lambda h, t: (h, 0, 0, 0))
    return pl.pallas_call(
        _rwkv_lanes_body, grid=(RWKV_HEADS // 2, steps // SUBLANES),
        in_specs=[xspec] * 6 + [sspec], out_specs=[xspec, sspec],
        out_shape=[jax.ShapeDtypeStruct((steps, ninst, RWKV_W), F32),
                   jax.ShapeDtypeStruct(s0.shape, F32)],
        scratch_shapes=[pltpu.VMEM((2, RWKV_N, RWKV_N, ninst), F32),
                        pltpu.VMEM((6, 2 * RWKV_N, ninst), F32),
                        pltpu.VMEM((2 * RWKV_N, ninst), F32)],
        compiler_params=_cp(2), name="rwkv_lanes")(*xs, s0)


RWKV_C_CHUNKS = 2


def _tri_solve_body(nt_ref, tt_ref):
    c = RWKV_CHUNK
    ninst = nt_ref.shape[2]
    tt_ref[...] = jnp.zeros_like(tt_ref)
    g = SUBLANES
    nb = 4
    sub = lax.broadcasted_iota(jnp.int32, (g, ninst), 0)
    for t0 in range(0, c, nb):
        ngrp = (t0 + nb - 1) // g + 1
        rs = [[jnp.where(sub + g * j == t0 + i, 1.0, 0.0).astype(F32) for j in range(ngrp)]
              for i in range(nb)]
        for sg in range(-(-t0 // g)):

            def acc(s, carry, sg=sg, t0=t0):
                ts = [tt_ref[s, j * g:(j + 1) * g, :] for j in range(sg + 1)]
                nrow = [nt_ref[t0 + i, pl.ds(s, 1), :] for i in range(nb)]
                return tuple(tuple(carry[i][j] + nrow[i] * ts[j] for j in range(sg + 1))
                             for i in range(nb))

            res = lax.fori_loop(sg * g, min(sg * g + g, t0), acc,
                                tuple(tuple(rs[i][:sg + 1]) for i in range(nb)))
            for i in range(nb):
                rs[i][:sg + 1] = list(res[i])
        for i in range(nb):
            for j in range(i):
                nij = nt_ref[t0 + i, t0 + j:t0 + j + 1, :]
                rs[i] = [x + nij * y for x, y in zip(rs[i], rs[j])]
            tt_ref[t0 + i, 0:ngrp * g, :] = jnp.concatenate(rs[i], axis=0)


def _tri_solve(nt):
    nh, c, _, ninst = nt.shape
    spec = pl.BlockSpec((None, c, c, ninst), lambda h: (h, 0, 0, 0))
    return pl.pallas_call(
        _tri_solve_body, grid=(nh,), in_specs=[spec], out_specs=spec,
        out_shape=jax.ShapeDtypeStruct(nt.shape, F32),
        compiler_params=_cp(1), name="tri_solve")(nt)


def _rwkv_chunk_body(r_ref, w_ref, k_ref, v_ref, a_ref, b_ref, t_ref, l_ref, y_ref, sf_ref, s_ref):
    c = RWKV_CHUNK
    pw = 2 * RWKV_N
    ci = pl.program_id(1)

    @pl.when(ci == 0)
    def _():
        s_ref[...] = jnp.zeros_like(s_ref)

    lmat = l_ref[...]
    ti = lax.broadcasted_iota(jnp.int32, (c, c), 0)
    si = lax.broadcasted_iota(jnp.int32, (c, c), 1)
    strict, incl = si < ti, si <= ti
    lane = lax.broadcasted_iota(jnp.int32, (c, pw), 1)
    lane2 = lax.broadcasted_iota(jnp.int32, (2 * c, pw), 1)
    r2 = lax.broadcasted_iota(jnp.int32, (pw, pw), 0)
    c2 = lax.broadcasted_iota(jnp.int32, (pw, pw), 1)
    same_head = (r2 < RWKV_N) == (c2 < RWKV_N)
    eye = r2 == c2
    bz = lambda x: jnp.zeros_like(x)
    pairs = range(RWKV_HEADS // 2)
    units = [(q, p) for q in range(RWKV_C_CHUNKS) for p in pairs]
    at = lambda ref, u: ref[u[0] * c:(u[0] + 1) * c, u[1] * pw:(u[1] + 1) * pw]
    halves = [(u, hh) for u in units for hh in range(2)]
    lws = {u: jnp.log(at(w_ref, u)) for u in units}
    cws = {u: _split_dot_left(lmat, lws[u]) for u in units}
    vs = {u: at(v_ref, u) for u in units}
    wts = {u: jnp.exp(cws[u]) for u in units}
    rts = {u: at(r_ref, u) * wts[u] for u in units}
    ats = {u: at(a_ref, u) * jnp.exp(cws[u] - lws[u]) for u in units}
    lhs = {u: _split(jnp.concatenate([ats[u], rts[u]], axis=0)) for u in units}
    bts = {u: _split(at(b_ref, u) * jnp.exp(-cws[u])) for u in units}
    kts = {u: _split(at(k_ref, u) * jnp.exp(-cws[u])) for u in units}
    a_ak, a_rk, a_rb = {}, {}, {}
    for u, hh in halves:
        m2 = (lane2 < RWKV_N) if hh == 0 else (lane2 >= RWKV_N)
        lh = jnp.where(m2, lhs[u][0], bz(lhs[u][0]))
        ll = jnp.where(m2, lhs[u][1], bz(lhs[u][1]))
        pak = (_dot_nt(lh[:c], kts[u][0])
               + (_dot_nt(lh[:c], kts[u][1]) + _dot_nt(ll[:c], kts[u][0])))
        a_ak[u, hh] = jnp.where(strict, pak, 0.0)
        a_rk[u, hh] = jnp.where(incl, _dot_nt(lh[c:], kts[u][0]), 0.0).astype(BF16)
        a_rb[u, hh] = jnp.where(incl, _dot_nt(lh[c:], bts[u][0]), 0.0).astype(BF16)
    vps = {uh: _dot3(a_ak[uh], vs[uh[0]]) for uh in halves}
    gs = {(u, hh): _dot3(t_ref[2 * u[1] + hh, u[0]], jnp.concatenate([ats[u], vps[u, hh]], axis=1))
          for u, hh in halves}
    pick = lambda x0, x1: jnp.where(lane < RWKV_N, x0, x1)
    ahats = {u: pick(gs[u, 0][:, :pw], gs[u, 1][:, :pw]) for u in units}
    vhats = {u: pick(gs[u, 0][:, pw:], gs[u, 1][:, pw:]) for u in units}
    ahb = {u: ahats[u].astype(BF16) for u in units}
    vhb = {u: vhats[u].astype(BF16) for u in units}
    vbs = {u: vs[u].astype(BF16) for u in units}
    rhats = {u: (rts[u] + pick(_dot(a_rb[u, 0], ahb[u]), _dot(a_rb[u, 1], ahb[u]))).astype(BF16)
             for u in units}
    yhats = {u: pick(_dot(a_rb[u, 0], vhb[u]) + _dot(a_rk[u, 0], vbs[u]),
                     _dot(a_rb[u, 1], vhb[u]) + _dot(a_rk[u, 1], vbs[u])) for u in units}
    wends = {u: jnp.exp(cws[u][c - 1:c] - cws[u]) for u in units}
    bkts = {u: jnp.concatenate([at(b_ref, u) * wends[u], at(k_ref, u) * wends[u]], axis=0).T
            for u in units}
    mpts = {u: jnp.where(same_head, _dot3(bkts[u], jnp.concatenate([ahats[u], bz(ahats[u])], axis=0)), 0.0)
            + jnp.where(eye, wts[u][c - 1:c], 0.0) for u in units}
    zpts = {u: jnp.where(same_head, _dot3(bkts[u], jnp.concatenate([vhats[u], vs[u]], axis=0)), 0.0)
            for u in units}
    spts = [s_ref[p] for p in pairs]
    for q in range(RWKV_C_CHUNKS):
        for p in pairs:
            y_ref[q * c:(q + 1) * c, p * pw:(p + 1) * pw] = (
                _dot(rhats[q, p], spts[p].astype(BF16)) + yhats[q, p])
        spts = [_dot3(mpts[q, p], spts[p]) + zpts[q, p] for p in pairs]
    for p in pairs:
        s_ref[p] = spts[p]

    @pl.when(ci == pl.num_programs(1) - 1)
    def _():
        sf_ref[...] = s_ref[...]


def _rwkv_chunk(xs, t4, lmat, nseq):
    n = xs[0].shape[0]
    c = RWKV_CHUNK
    q = RWKV_C_CHUNKS
    nch = n // (q * c) // nseq
    pw = 2 * RWKV_N
    tile = pl.BlockSpec((q * c, RWKV_W), lambda b, i: (b * nch + i, 0))
    return pl.pallas_call(
        _rwkv_chunk_body, grid=(nseq, nch),
        in_specs=[tile] * 6 + [pl.BlockSpec((RWKV_HEADS, q, c, c), lambda b, i: (0, b * nch + i, 0, 0)),
                               pl.BlockSpec((c, c), lambda b, i: (0, 0))],
        out_specs=[tile, pl.BlockSpec((None, RWKV_HEADS // 2, pw, pw), lambda b, i: (b, 0, 0, 0))],
        out_shape=[jax.ShapeDtypeStruct((n, RWKV_W), F32),
                   jax.ShapeDtypeStruct((nseq, RWKV_HEADS // 2, pw, pw), F32)],
        scratch_shapes=[pltpu.VMEM((RWKV_HEADS // 2, pw, pw), F32)],
        compiler_params=_cp(2), name="rwkv_chunk")(*xs, t4, lmat)


def _hgrn_lower_bound(logits, layer):
    m = jnp.max(logits, axis=0, keepdims=True)
    e = jnp.exp(logits - m)
    gam = e / jnp.sum(e, axis=0, keepdims=True)
    cs = gam[0:1]
    for i in range(1, layer + 1):
        cs = cs + gam[i:i + 1]
    return cs - gam[0:1]


def _hgrn_level_matrix():
    t = np.arange(HGRN_CHUNK)[:, None]
    s = np.arange(HGRN_CHUNK)[None, :]
    mats = []
    for upper in (False, True):
        for e in (1, 2):
            same = (t >> e) == (s >> e)
            mats.append(same & ((s > t) if upper else (s <= t)))
    mats.append(s <= t)
    return jnp.asarray(np.concatenate(mats, axis=0).astype(np.float32), dtype=BF16)


def _hgrn_prompt_body(*refs, layer):
    npair = HGRN_HEADS // 2
    q_refs, f_refs, i_refs, g_refs = (refs[i * npair:(i + 1) * npair] for i in range(4))
    lg_ref, nw_ref, cm_ref, o_ref, sf_ref, s_ref = refs[4 * npair:]
    c = pl.program_id(1)

    @pl.when(c == 0)
    def _():
        s_ref[...] = jnp.zeros_like(s_ref)

    n = HGRN_CHUNK
    heads = range(HGRN_HEADS)
    col = lambda rs, h: rs[h // 2][:, (h % 2) * HGRN_D:(h % 2 + 1) * HGRN_D]
    hsl = lambda h: slice(h * HGRN_D, (h + 1) * HGRN_D)
    t = lax.broadcasted_iota(jnp.int32, (n, n), 0)
    s_ = lax.broadcasted_iota(jnp.int32, (n, n), 1)
    lb_all = _hgrn_lower_bound(lg_ref[...], layer)
    cm = cm_ref[...]
    ones = jnp.ones((n, n), BF16)
    fgs = [lb_all[:, hsl(h)] + (1.0 - lb_all[:, hsl(h)]) * jax.nn.sigmoid(col(f_refs, h))
           for h in heads]
    lfs = [jnp.log(fg) for fg in fgs]
    css = [_split_dot_left(cm, lf) for lf in lfs]
    bs = [cs[4 * n:5 * n] for cs in css]

    def cl(h, e):
        if e <= 2:
            return css[h][(e - 1) * n:e * n]
        if e == HGRN_LEVELS:
            return bs[h]
        m = 1 << e
        b3 = bs[h].reshape(n // m, m, HGRN_D)
        ends = jnp.broadcast_to(b3[:, m - 1:m, :], b3.shape)
        before = jnp.concatenate([jnp.zeros_like(ends[:1]), ends[:-1]], axis=0)
        return (b3 - before).reshape(n, HGRN_D)

    def cu(h, e):
        if e <= 2:
            return css[h][(2 + e - 1) * n:(2 + e) * n]
        if e == HGRN_LEVELS:
            return bs[h][n - 1:n] - bs[h]
        m = 1 << e
        b3 = bs[h].reshape(n // m, m, HGRN_D)
        return (jnp.broadcast_to(b3[:, m - 1:m, :], b3.shape) - b3).reshape(n, HGRN_D)
    qs = [col(q_refs, h) * jax.nn.sigmoid(col(q_refs, h)) for h in heads]
    ks = [1.0 - fg for fg in fgs]
    vbs = [col(i_refs, h).astype(BF16) for h in heads]
    acc = [jnp.where(t == s_, _dot_nt(qs[h].astype(BF16), ks[h].astype(BF16)), 0.0) for h in heads]
    for e in range(HGRN_LEVELS):
        mask = (((t >> e) & 1) == 1) & ((s_ >> e) == ((t >> e) - 1))
        for h in heads:
            qe = qs[h] * (fgs[h] if e == 0 else jnp.exp(cl(h, e)))
            ke = ks[h] if e == 0 else ks[h] * jnp.exp(cu(h, e))
            acc[h] = acc[h] + jnp.where(mask, _dot_nt(qe.astype(BF16), ke.astype(BF16)), 0.0)
    sts = [s_ref[h] for h in heads]
    os_ = [_dot(acc[h].astype(BF16), vbs[h])
           + _dot((qs[h] * jnp.exp(cl(h, HGRN_LEVELS))).astype(BF16), sts[h].astype(BF16))
           for h in heads]
    dks = [jnp.exp(_split_dot(lfs[h].T, ones)) for h in heads]
    for h in heads:
        kf = ks[h] * jnp.exp(cu(h, HGRN_LEVELS))
        s_ref[h] = dks[h] * sts[h] + _dot(kf.T.astype(BF16), vbs[h])
    for h in heads:
        o = os_[h]
        ms = jnp.mean(o * o, axis=-1, keepdims=True)
        gr = col(g_refs, h)
        o_ref[:, hsl(h)] = (o * lax.rsqrt(ms + NORM_EPS) * nw_ref[:, hsl(h)]) * (gr * jax.nn.sigmoid(gr))

    @pl.when(c == pl.num_programs(1) - 1)
    def _():
        sf_ref[...] = s_ref[...]


def _split_dot_left(m, x):
    hi, lo = _split(x)
    return _dot(m, hi) + _dot(m, lo)


def _hgrn_prompt(proj, logits, nw, cmat, nseq, seq_len, layer):
    n = proj.shape[0]
    nch = seq_len // HGRN_CHUNK
    wb = 2 * HGRN_D
    npair = HGRN_HEADS // 2
    cols = [pl.BlockSpec((HGRN_CHUNK, wb), lambda b, c, j=(HGRN_COL0 + g * HGRN_W) // wb + p:
                         (b * nch + c, j)) for g in range(4) for p in range(npair)]
    return pl.pallas_call(
        functools.partial(_hgrn_prompt_body, layer=layer),
        grid=(nseq, nch),
        in_specs=cols + [pl.BlockSpec(logits.shape, lambda b, c: (0, 0)),
                         pl.BlockSpec((1, HGRN_W), lambda b, c: (0, 0)),
                         pl.BlockSpec(cmat.shape, lambda b, c: (0, 0))],
        out_specs=[pl.BlockSpec((HGRN_CHUNK, HGRN_W), lambda b, c: (b * nch + c, 0)),
                   pl.BlockSpec((None, HGRN_HEADS, HGRN_D, HGRN_D), lambda b, c: (b, 0, 0, 0))],
        out_shape=[jax.ShapeDtypeStruct((n, HGRN_W), F32),
                   jax.ShapeDtypeStruct((nseq, HGRN_HEADS, HGRN_D, HGRN_D), F32)],
        scratch_shapes=[pltpu.VMEM((HGRN_HEADS, HGRN_D, HGRN_D), F32)],
        compiler_params=_cp(2), name="hgrn_prompt")(*([proj] * (4 * npair)), logits, nw, cmat)


HGRN_KT = 32


def _hgrn_lanes_body(q_ref, f_ref, i_ref, g_ref, lg_ref, nw_ref, s0_ref, o_ref, sf_ref,
                     qt_ref, ft_ref, kt_ref, vt_ref, oacc_ref, *, layer):
    kt = pl.program_id(1)
    lb = _hgrn_lower_bound(lg_ref[...], layer)

    @pl.when(kt == 0)
    def _():
        oacc_ref[...] = jnp.zeros_like(oacc_ref)
        for tl in range(SUBLANES):
            qr = q_ref[tl]
            fg = lb + (1.0 - lb) * jax.nn.sigmoid(f_ref[tl])
            qt_ref[tl] = (qr * jax.nn.sigmoid(qr)).T
            ft_ref[tl] = fg.T
            kt_ref[tl] = (1.0 - fg).T
            vt_ref[tl] = i_ref[tl].T

    def step(tl, carry, src_ref=sf_ref):
        vt = vt_ref[tl]

        def krow(kk, o):
            row = kt * HGRN_KT + kk
            s = ft_ref[tl, pl.ds(row, 1), :] * src_ref[0, kk] + kt_ref[tl, pl.ds(row, 1), :] * vt
            sf_ref[0, kk] = s
            return o + qt_ref[tl, pl.ds(row, 1), :] * s

        oacc_ref[tl] = lax.fori_loop(0, HGRN_KT, krow, oacc_ref[tl], unroll=2)
        return carry

    step(0, 0, s0_ref)
    lax.fori_loop(1, SUBLANES, step, 0)

    @pl.when(kt == pl.num_programs(1) - 1)
    def _():
        for tl in range(SUBLANES):
            o = oacc_ref[tl].T
            ms = jnp.mean(o * o, axis=-1, keepdims=True)
            gr = g_ref[tl]
            o_ref[tl] = (o * lax.rsqrt(ms + NORM_EPS) * nw_ref[...]) * (gr * jax.nn.sigmoid(gr))


def _hgrn_lanes(proj_t, logits, nw, s0, layer):
    steps, ninst, _ = proj_t.shape
    col = lambda off: pl.BlockSpec((steps, ninst, HGRN_D),
                                   lambda h, k, off=off: (0, 0, off // HGRN_D + h))
    sspec = pl.BlockSpec((1, HGRN_KT, HGRN_D, ninst), lambda h, k: (h, k, 0, 0))
    tbuf = pltpu.VMEM((steps, HGRN_D, ninst), F32)
    return pl.pallas_call(
        functools.partial(_hgrn_lanes_body, layer=layer),
        grid=(HGRN_HEADS, HGRN_D // HGRN_KT),
        in_specs=[col(0), col(HGRN_W), col(2 * HGRN_W), col(3 * HGRN_W),
                  pl.BlockSpec((logits.shape[0], HGRN_D), lambda h, k: (0, h)),
                  pl.BlockSpec((1, HGRN_D), lambda h, k: (0, h)), sspec],
        out_specs=[pl.BlockSpec((steps, ninst, HGRN_D), lambda h, k: (0, 0, h)), sspec],
        out_shape=[jax.ShapeDtypeStruct((steps, ninst, HGRN_W), F32),
                   jax.ShapeDtypeStruct(s0.shape, F32)],
        scratch_shapes=[tbuf, tbuf, tbuf, tbuf, tbuf],
        compiler_params=_cp(2), name="hgrn_lanes")(
            proj_t, proj_t, proj_t, proj_t, logits, nw, s0)


def _rglru_body(xb_ref, gate_ref, chist_ref, hinit_ref, cw_ref, cb_ref, wa_ref, ba_ref, wx_ref,
                bx_ref, lam_ref, y_ref, cst_ref, hst_ref, ccarry_ref, hcarry_ref,
                *, long_seq, tps):
    x = xb_ref[...]
    tm, c = x.shape
    if long_seq:
        @pl.when(pl.program_id(0) % tps == 0)
        def _():
            ccarry_ref[...] = chist_ref[...]
            hcarry_ref[...] = hinit_ref[...]
        hist = ccarry_ref[...]
    else:
        hist = chist_ref[...]
    p1, p2, p3 = _prev_rows(x, hist, (1, 2, 3), long_seq)
    cw = cw_ref[...]
    xc = cb_ref[...] + cw[0:1] * p3 + cw[1:2] * p2 + cw[2:3] * p1 + cw[3:4] * x
    xcb = xc.astype(BF16)
    r = jax.nn.sigmoid(_dot(xcb, wa_ref[...]) + ba_ref[...])
    ig = jax.nn.sigmoid(_dot(xcb, wx_ref[...]) + bx_ref[...])
    log_a = (-LRU_C) * r * _softplus(-lam_ref[...])
    a = jnp.exp(log_a)
    h = jnp.sqrt(-_expm1(2.0 * log_a)) * (ig * xc)
    if long_seq:
        pos = lax.broadcasted_iota(jnp.int32, (tm, c), 0)
        k = 1
        while k < tm:
            keep = pos >= k
            h = h + a * jnp.where(keep, pltpu.roll(h, k, 0), 0.0)
            a = a * jnp.where(keep, pltpu.roll(a, k, 0), 1.0)
            k *= 2
        h = h + a * hcarry_ref[SUBLANES - 1:SUBLANES, :]
        hcarry_ref[...] = h[tm - SUBLANES:]
        ccarry_ref[...] = x[tm - SUBLANES:]
        cst_ref[...] = x[tm - SUBLANES:]
        hst_ref[...] = h[tm - SUBLANES:]
    else:
        shp = (tm // SUBLANES, SUBLANES, c)
        h3, a3 = h.reshape(shp), a.reshape(shp)
        pos = lax.broadcasted_iota(jnp.int32, shp, 1)
        k = 1
        while k < SUBLANES:
            keep = pos >= k
            h3 = h3 + a3 * jnp.where(keep, pltpu.roll(h3, k, 1), 0.0)
            a3 = a3 * jnp.where(keep, pltpu.roll(a3, k, 1), 1.0)
            k *= 2
        h = (h3 + a3 * hinit_ref[...].reshape(shp)).reshape(tm, c)
        cst_ref[...] = x
        hst_ref[...] = h
    y_ref[...] = h * jax.nn.gelu(gate_ref[...])


def _rglru(proj, chist, hinit, lp, seq_len, tm):
    n = proj.shape[0]
    long_seq = seq_len > SUBLANES
    tps = max(seq_len // tm, 1)
    c0 = LRU_COL0 // LRU_W
    if long_seq:
        hs = pl.BlockSpec((SUBLANES, LRU_W), lambda i: (i // tps, 0))
    else:
        hs = pl.BlockSpec((tm, LRU_W), lambda i: (i, 0))
    row = pl.BlockSpec((1, LRU_W), lambda i: (0, 0))
    sq = pl.BlockSpec((LRU_W, LRU_W), lambda i: (0, 0))
    return pl.pallas_call(
        functools.partial(_rglru_body, long_seq=long_seq, tps=tps), grid=(n // tm,),
        in_specs=[pl.BlockSpec((tm, LRU_W), lambda i: (i, c0)),
                  pl.BlockSpec((tm, LRU_W), lambda i: (i, c0 + 1)), hs, hs,
                  pl.BlockSpec((4, LRU_W), lambda i: (0, 0)), row, sq, row, sq, row, row],
        out_specs=[pl.BlockSpec((tm, LRU_W), lambda i: (i, 0)), hs, hs],
        out_shape=[jax.ShapeDtypeStruct((n, LRU_W), F32),
                   jax.ShapeDtypeStruct(chist.shape, F32),
                   jax.ShapeDtypeStruct(chist.shape, F32)],
        scratch_shapes=[pltpu.VMEM((SUBLANES, LRU_W), F32), pltpu.VMEM((SUBLANES, LRU_W), F32)],
        compiler_params=_cp(1), name="rglru")(
            proj, proj, chist, hinit, lp["lru_cw"], lp["lru_cb"], lp["lru_wa"], lp["lru_ba"],
            lp["lru_wx"], lp["lru_bx"], lp["lru_lam"])


def _hist(state):
    nseq, k, c = state.shape
    return jnp.pad(state, ((0, 0), (SUBLANES - k, 0), (0, 0))).reshape(nseq * SUBLANES, c)


def _block_diag(w):
    h, a, b = w.shape
    eye = jnp.eye(h, dtype=w.dtype)
    return (eye[:, None, :, None] * w[:, :, None, :]).reshape(h * a, h * b)


def _layer(x3, st, lp, consts, layer, final_nw, w_in_at, ffn_w):
    nseq, seq_len, d = x3.shape
    n = nseq * seq_len
    long_seq = seq_len > SUBLANES
    s_rw, shift_rw, s_hg, h_lru, buf_lru, buf_ffn = st
    x = x3.reshape(n, d)
    ones = consts["ones"]

    w_in, w_in_layer = w_in_at
    proj, w_in_bf16 = _in_proj(x, lp["norm_mix"], w_in, w_in_layer, tm=min(n, 1024),
                               tn=IN_COLS // 4 if w_in.dtype == BF16 else 512)

    tm_pre = min(n, seq_len if long_seq else n, 256)
    lmat_pre = None
    if long_seq:
        tril = np.tril(np.ones((RWKV_CHUNK, RWKV_CHUNK)))
        lmat_pre = jnp.asarray(np.kron(np.eye(tm_pre // RWKV_CHUNK), tril), dtype=BF16)
    outs = _rwkv_pre(proj, _hist(shift_rw[:, None, :]), lp, ones, seq_len, tm_pre, lmat_pre)
    r, w, k, v, na, kb, g, bonus, shift_out = outs[:9]
    new_shift = shift_out.reshape(nseq, SUBLANES, RWKV_COLS)[:, SUBLANES - 1]
    if long_seq:
        c = RWKV_CHUNK
        t4 = _tri_solve(outs[9].transpose(0, 2, 3, 1)).transpose(0, 3, 1, 2)
        y, spt = _rwkv_chunk((r, w, k, v, na, kb), t4, consts["lmat"], nseq)
        sp = spt.reshape(nseq, RWKV_HEADS // 2, 2, RWKV_N, 2, RWKV_N)
        new_s_rw = jnp.stack([sp[:, :, 0, :, 0, :], sp[:, :, 1, :, 1, :]], axis=2)
        new_s_rw = new_s_rw.reshape(nseq, RWKV_HEADS, RWKV_N, RWKV_N).transpose(0, 1, 3, 2)
    else:
        tmaj = lambda t: t.reshape(nseq, seq_len, RWKV_W).transpose(1, 0, 2)
        s0 = s_rw.transpose(1, 2, 3, 0)
        yb, sf = _rwkv_lanes([tmaj(t) for t in (r, w, k, v, na, kb)], s0)
        y = yb.transpose(1, 0, 2).reshape(n, RWKV_W)
        new_s_rw = sf.transpose(3, 0, 1, 2)
    y_rw = _rwkv_post(y, bonus, g, lp["ln_w"], lp["ln_b"], ones, tm=min(n, 512))

    if long_seq:
        y_hg, new_s_hg = _hgrn_prompt(proj, consts["lb_logits"], lp["hgrn_nw"], consts["cmat"],
                                      nseq, seq_len, layer)
    else:
        proj_t = proj[:, HGRN_COL0:LRU_COL0].reshape(nseq, seq_len, 4 * HGRN_W).transpose(1, 0, 2)
        s0 = s_hg.transpose(1, 2, 3, 0)
        o_t, sf = _hgrn_lanes(proj_t, consts["lb_logits"], lp["hgrn_nw"], s0, layer)
        y_hg = o_t.transpose(1, 0, 2).reshape(n, HGRN_W)
        new_s_hg = sf.transpose(3, 0, 1, 2)

    if long_seq:
        hinit = _hist(h_lru[:, None, :])
        tm_lru = min(seq_len, 256)
    else:
        hinit = jnp.broadcast_to(h_lru[:, None, :], (nseq, seq_len, LRU_W)).reshape(n, LRU_W)
        tm_lru = min(n, 512)
    y_lru, cst, hst = _rglru(proj, _hist(buf_lru), hinit, lp, seq_len, tm_lru)
    new_buf_lru = cst.reshape(nseq, SUBLANES, LRU_W)[:, SUBLANES - 3:]
    new_h_lru = hst.reshape(nseq, SUBLANES, LRU_W)[:, SUBLANES - 1]

    x = _out_proj(x, y_rw, y_hg, y_lru, consts["w_out"], layer, tm=min(n, 1024), tn=512)
    tm_ffn = min(n, seq_len if long_seq else n, 1024)
    tf_ffn = 512 if ffn_w[0][0].dtype == BF16 else 256
    outs = _ffn(x, lp["norm_ffn"], ffn_w, lp["ffn_cw"], lp["ffn_cb"], _hist(buf_ffn), seq_len,
                tm=tm_ffn, tf=tf_ffn, final_nw=final_nw)
    x, fst = outs[:2]
    new_buf_ffn = fst.reshape(nseq, -1, SUBLANES, D_FF)[:, -1, SUBLANES - 2:]
    states = (new_s_rw, new_shift, new_s_hg, new_h_lru, new_buf_lru, new_buf_ffn)
    ffn_w_bf16 = None
    if len(outs) > 2:
        ffn_w_bf16 = ((outs[2], 0, 0), (outs[3], 0, 0), (outs[4], 0))
    return x.reshape(nseq, seq_len, d), states, (w_in_bf16, ffn_w_bf16)


def kernel(x_prompt, x_sample, state_rwkv, state_rwkv_shift, state_hgrn, state_rglru, cache_rglru_conv, cache_ffn_conv, norm_mix, w_in, rwkv_mu, rwkv_w0, rwkv_w2, rwkv_a0, rwkv_a2, rwkv_g2, rwkv_k_k, rwkv_k_a, rwkv_r_k, rwkv_ln_w, rwkv_ln_b, hgrn_lb_logits, hgrn_norm_w, rglru_conv_w, rglru_conv_b, rglru_wa, rglru_ba, rglru_wx, rglru_bx, rglru_lambda, w_out, norm_ffn, ffn_w_up, ffn_conv_w, ffn_conv_b, ffn_w_down, norm_final):
    depth = w_in.shape[0]
    nb, nt, _ = x_prompt.shape
    db = x_sample.shape[0]
    consts = {
        "ones": jnp.asarray(np.kron(np.eye(RWKV_HEADS), np.ones((RWKV_N, RWKV_N))), dtype=BF16),
        "cmat": _hgrn_level_matrix(),
        "lmat": jnp.asarray(np.tril(np.ones((RWKV_CHUNK, RWKV_CHUNK), np.float32)), dtype=BF16),
        "lb_logits": hgrn_lb_logits,
        "w_out": w_out.astype(BF16),
    }
    rowv = lambda a: a.reshape(1, -1)
    x_p, x_s = x_prompt, x_sample
    new_p, new_s = [], []
    for l in range(depth):
        lp = {
            "norm_mix": rowv(norm_mix[l]), "mu": rowv(rwkv_mu[l]),
            "w0": rowv(rwkv_w0[l]),
            "w2p": jnp.pad(rwkv_w2[l], ((0, 64), (0, 0))).astype(BF16),
            "a0": rowv(rwkv_a0[l]),
            "a2p": jnp.pad(rwkv_a2[l], ((64, 0), (0, 0))).astype(BF16),
            "g2": rwkv_g2[l].astype(BF16), "k_k": rowv(rwkv_k_k[l]), "k_a": rowv(rwkv_k_a[l]),
            "r_k": rowv(rwkv_r_k[l]), "ln_w": rowv(rwkv_ln_w[l]), "ln_b": rowv(rwkv_ln_b[l]),
            "hgrn_nw": rowv(hgrn_norm_w[l]),
            "lru_cw": rglru_conv_w[l], "lru_cb": rowv(rglru_conv_b[l]),
            "lru_wa": _block_diag(rglru_wa[l]).astype(BF16), "lru_ba": rowv(rglru_ba[l]),
            "lru_wx": _block_diag(rglru_wx[l]).astype(BF16), "lru_bx": rowv(rglru_bx[l]),
            "lru_lam": rowv(rglru_lambda[l]),
            "norm_ffn": rowv(norm_ffn[l]), "ffn_cw": ffn_conv_w[l],
            "ffn_cb": rowv(ffn_conv_b[l]),
        }
        zero = lambda *s: jnp.zeros(s, F32)
        st_p = (None, zero(nb, RWKV_COLS), None, zero(nb, LRU_W), zero(nb, 3, LRU_W),
                zero(nb, 2, D_FF))
        final_nw = rowv(norm_final) if l == depth - 1 else None
        st_s = (state_rwkv[l], state_rwkv_shift[l], state_hgrn[l], state_rglru[l],
                cache_rglru_conv[l], cache_ffn_conv[l])
        ffn_f32 = ((ffn_w_up, l, 0), (ffn_w_up, l, D_FF), (ffn_w_down, l))
        x_s, ss, (w_in_l, ffn_l) = _layer(x_s, st_s, lp, consts, l, final_nw, (w_in, l), ffn_f32)
        x_p, sp, _ = _layer(x_p, st_p, lp, consts, l, final_nw, (w_in_l, 0), ffn_l)
        new_p.append(sp)
        new_s.append(ss)
    stack = lambda sts: [jnp.stack(s, axis=0) for s in zip(*sts)]
    return (x_p, x_s, *stack(new_p), *stack(new_s))
```

```python
import functools

import numpy as np
import jax
import jax.numpy as jnp
from jax import lax
from jax.experimental import pallas as pl
from jax.experimental.pallas import tpu as pltpu

F32 = jnp.float32
BF16 = jnp.bfloat16

D_MODEL = 2048
RWKV_HEADS = 12
RWKV_N = 64
RWKV_W = RWKV_HEADS * RWKV_N
RWKV_COLS = 2560
RWKV_LN_EPS = 64e-5
HGRN_HEADS = 6
HGRN_D = 128
HGRN_W = HGRN_HEADS * HGRN_D
HGRN_COL0 = RWKV_COLS
LRU_W = 512
LRU_COL0 = RWKV_COLS + 4 * HGRN_W
LRU_C = 8.0
IN_COLS = 6656
D_FF = 5632
NORM_EPS = 1e-6

SUBLANES = 8
LANES = 128
MXU_WIDTH = 256
RWKV_CHUNK = 64
HGRN_CHUNK = 128
HGRN_LEVELS = 7
VMEM_LIMIT = 48 * 1024 * 1024
BIG_VMEM_LIMIT = 56 * 1024 * 1024


def _cp(n, limit=VMEM_LIMIT):
    return pltpu.CompilerParams(dimension_semantics=("arbitrary",) * n, vmem_limit_bytes=limit)


def _dot(a, b):
    return jnp.dot(a, b, preferred_element_type=F32)


def _dot_nt(a, b):
    return lax.dot_general(a, b, (((1,), (1,)), ((), ())), preferred_element_type=F32)


def _split(x):
    hi = x.astype(BF16)
    lo = (x - hi.astype(F32)).astype(BF16)
    return hi, lo


def _split_dot(x, m):
    hi, lo = _split(x)
    return _dot(hi, m) + _dot(lo, m)


def _dot3(a, b, nt=False):
    f = _dot_nt if nt else _dot
    ah, al = _split(a)
    bh, bl = _split(b)
    return f(ah, bh) + (f(ah, bl) + f(al, bh))


def _softplus(z):
    return jnp.maximum(z, 0.0) + jnp.log1p(jnp.exp(-jnp.abs(z)))


def _expm1(z):
    return jnp.tanh(0.5 * z) * (jnp.exp(z) + 1.0)


def _prev_rows(x, hist, ks, long_seq):
    tm, c = x.shape
    if long_seq:
        ext = jnp.concatenate([hist, x], axis=0)
        return [pltpu.roll(ext, k, 0)[SUBLANES:] for k in ks]
    x3 = x.reshape(tm // SUBLANES, SUBLANES, c)
    h3 = hist.reshape(tm // SUBLANES, SUBLANES, c)
    pos = lax.broadcasted_iota(jnp.int32, x3.shape, 1)
    return [jnp.where(pos >= k, pltpu.roll(x3, k, 1), pltpu.roll(h3, k, 1)).reshape(tm, c)
            for k in ks]


def _in_proj_body(x_ref, nw_ref, w_ref, o_ref, *rest):
    xn_ref = rest[-1]

    @pl.when(pl.program_id(1) == 0)
    def _():
        x = x_ref[...]
        ms = jnp.mean(x * x, axis=-1, keepdims=True)
        xn_ref[...] = (x * lax.rsqrt(ms + NORM_EPS) * nw_ref[...]).astype(BF16)
    w = w_ref[...].astype(BF16)
    if len(rest) == 2:
        rest[0][...] = w
    o_ref[...] = _dot(xn_ref[...], w)


def _in_proj(x, nw, w, layer, tm, tn):
    n, d = x.shape
    c = w.shape[2]
    emit = w.dtype != BF16
    assert not emit or n == tm
    out_specs = [pl.BlockSpec((tm, tn), lambda i, j: (i, j))]
    out_shape = [jax.ShapeDtypeStruct((n, c), F32)]
    if emit:
        out_specs.append(pl.BlockSpec((None, d, tn), lambda i, j: (0, 0, j)))
        out_shape.append(jax.ShapeDtypeStruct((1, d, c), BF16))
    outs = pl.pallas_call(
        _in_proj_body, grid=(n // tm, c // tn),
        in_specs=[pl.BlockSpec((tm, d), lambda i, j: (i, 0)),
                  pl.BlockSpec((1, d), lambda i, j: (0, 0)),
                  pl.BlockSpec((None, d, tn), lambda i, j: (layer, 0, j))],
        out_specs=out_specs, out_shape=out_shape,
        scratch_shapes=[pltpu.VMEM((tm, d), BF16)],
        compiler_params=_cp(2, BIG_VMEM_LIMIT), name="in_proj")(x, nw, w)
    return outs if emit else (outs[0], None)


def _out_proj_body(x_ref, ya_ref, yb_ref, yc_ref, w_ref, o_ref, *rest):
    y_ref = rest[-1]

    @pl.when(pl.program_id(1) == 0)
    def _():
        y_ref[:, 0:RWKV_W] = ya_ref[...].astype(BF16)
        y_ref[:, RWKV_W:RWKV_W + HGRN_W] = yb_ref[...].astype(BF16)
        y_ref[:, RWKV_W + HGRN_W:] = yc_ref[...].astype(BF16)
    w = w_ref[...].astype(BF16)
    if len(rest) == 2:
        rest[0][...] = w
    o_ref[...] = x_ref[...] + _dot(y_ref[...], w)


def _out_proj(x, ya, yb, yc, w, layer, tm, tn):
    n, d = x.shape
    emit = w.dtype != BF16
    assert not emit or n == tm
    tile = pl.BlockSpec((tm, tn), lambda i, j: (i, j))
    out_specs, out_shape = [tile], [jax.ShapeDtypeStruct((n, d), F32)]
    if emit:
        out_specs.append(pl.BlockSpec((None, d, tn), lambda i, j: (0, 0, j)))
        out_shape.append(jax.ShapeDtypeStruct((1, d, d), BF16))
    outs = pl.pallas_call(
        _out_proj_body, grid=(n // tm, d // tn),
        in_specs=[tile,
                  pl.BlockSpec((tm, RWKV_W), lambda i, j: (i, 0)),
                  pl.BlockSpec((tm, HGRN_W), lambda i, j: (i, 0)),
                  pl.BlockSpec((tm, LRU_W), lambda i, j: (i, 0)),
                  pl.BlockSpec((None, d, tn), lambda i, j: (layer, 0, j))],
        out_specs=out_specs, out_shape=out_shape,
        scratch_shapes=[pltpu.VMEM((tm, d), BF16)],
        compiler_params=_cp(2, BIG_VMEM_LIMIT), name="out_proj")(x, ya, yb, yc, w)
    return outs if emit else (outs[0], None)


def _ffn_body(*refs, long_seq, tps, final, emit):
    x_ref, nw_ref, wg_ref, wv_ref, cw_ref, cb_ref, wd_ref, hist_ref = refs[:8]
    refs = refs[8:]
    if final:
        fnw_ref, refs = refs[0], refs[1:]
    o_ref, st_ref = refs[:2]
    refs = refs[2:]
    if emit:
        wgb_ref, wvb_ref, wdb_ref = refs[:3]
        refs = refs[3:]
    hn_ref, carry_ref = refs
    i = pl.program_id(0)
    j = pl.program_id(1)

    @pl.when(j == 0)
    def _():
        x = x_ref[...]
        ms = jnp.mean(x * x, axis=-1, keepdims=True)
        hn_ref[...] = (x * lax.rsqrt(ms + NORM_EPS) * nw_ref[...]).astype(BF16)
        o_ref[...] = x

    hn = hn_ref[...]
    if long_seq:
        @pl.when(i % tps == 0)
        def _():
            carry_ref[j] = hist_ref[...]
        hist = carry_ref[j]
    else:
        hist = hist_ref[...]
    tf = wg_ref.shape[1]
    nsplit = min(2, max(1, tf // MXU_WIDTH))
    halves = [slice(q * (tf // nsplit), (q + 1) * (tf // nsplit)) for q in range(nsplit)]
    wgs = [wg_ref[:, sl].astype(BF16) for sl in halves]
    wvs = [wv_ref[:, sl].astype(BF16) for sl in halves]
    wds = [wd_ref[sl, :].astype(BF16) for sl in halves]
    if emit:
        for sl, wg, wv, wd in zip(halves, wgs, wvs, wds):
            wgb_ref[:, sl] = wg
            wvb_ref[:, sl] = wv
            wdb_ref[sl, :] = wd
    gs = [_dot(hn, wg) for wg in wgs]
    vs = [_dot(hn, wv) for wv in wvs]
    cw = cw_ref[...]
    acc = None
    for sl, g, v, wd in zip(halves, gs, vs, wds):
        p1, p2 = _prev_rows(g, hist[:, sl], (1, 2), long_seq)
        gc = cb_ref[:, sl] + cw[0:1, sl] * p2 + cw[1:2, sl] * p1 + cw[2:3, sl] * g
        h = (gc * jax.nn.sigmoid(gc)) * v
        d = _dot(h.astype(BF16), wd)
        acc = d if acc is None else acc + d
        tail = g if not long_seq else g[g.shape[0] - SUBLANES:]
        st_ref[:, sl] = tail
        if long_seq:
            carry_ref[j, :, sl] = tail
    o_ref[...] += acc

    if final:
        @pl.when(j == pl.num_programs(1) - 1)
        def _():
            y = o_ref[...]
            ms = jnp.mean(y * y, axis=-1, keepdims=True)
            o_ref[...] = y * lax.rsqrt(ms + NORM_EPS) * fnw_ref[...]


def _ffn(x, nw, weights, cw, cb, hist, seq_len, tm, tf, final_nw=None):
    n, d = x.shape
    final = final_nw is not None
    (wg, wg_l, wg_off), (wv, wv_l, wv_off), (wd, wd_l) = weights
    emit = wg.dtype != BF16
    assert not emit or n == tm
    nf = D_FF // tf
    long_seq = seq_len > SUBLANES
    tps = max(seq_len // tm, 1)
    if long_seq:
        hist_spec = pl.BlockSpec((SUBLANES, tf), lambda i, j: (i // tps, j))
        st_spec = pl.BlockSpec((SUBLANES, tf), lambda i, j: (i, j))
        st_rows = n // tm * SUBLANES
    else:
        hist_spec = pl.BlockSpec((tm, tf), lambda i, j: (i, j))
        st_spec = pl.BlockSpec((tm, tf), lambda i, j: (i, j))
        st_rows = n
    body = functools.partial(_ffn_body, long_seq=long_seq, tps=tps, final=final, emit=emit)
    row_tile = pl.BlockSpec((tm, d), lambda i, j: (i, 0))
    wide = pl.BlockSpec((1, d), lambda i, j: (0, 0))
    x_tile = (pl.BlockSpec((tm, d), lambda i, j: (i, 0), pipeline_mode=pl.Buffered(1))
              if emit else row_tile)
    in_specs = [x_tile, wide,
                pl.BlockSpec((None, d, tf), lambda i, j: (wg_l, 0, wg_off // tf + j)),
                pl.BlockSpec((None, d, tf), lambda i, j: (wv_l, 0, wv_off // tf + j)),
                pl.BlockSpec((3, tf), lambda i, j: (0, j)),
                pl.BlockSpec((1, tf), lambda i, j: (0, j)),
                pl.BlockSpec((None, tf, d), lambda i, j: (wd_l, j, 0)),
                hist_spec]
    args = [x, nw, wg, wv, cw, cb, wd, hist]
    out_specs = [row_tile, st_spec]
    out_shape = [jax.ShapeDtypeStruct((n, d), F32), jax.ShapeDtypeStruct((st_rows, D_FF), F32)]
    scratch = [pltpu.VMEM((tm, d), BF16), pltpu.VMEM((nf, SUBLANES, tf), F32)]
    if emit:
        up_tile = pl.BlockSpec((None, d, tf), lambda i, j: (0, 0, j))
        out_specs += [up_tile, up_tile, pl.BlockSpec((None, tf, d), lambda i, j: (0, j, 0))]
        out_shape += [jax.ShapeDtypeStruct((1, d, D_FF), BF16)] * 2 + [
            jax.ShapeDtypeStruct((1, D_FF, d), BF16)]
    if final:
        in_specs.append(wide)
        args.append(final_nw)
    return pl.pallas_call(
        body, grid=(n // tm, nf), in_specs=in_specs, out_specs=out_specs, out_shape=out_shape,
        scratch_shapes=scratch,
        compiler_params=pltpu.CompilerParams(dimension_semantics=("arbitrary",) * 2,
                                             vmem_limit_bytes=BIG_VMEM_LIMIT),
        name="ffn")(*args)


def _rwkv_pre_body(*refs, long_seq, tps, with_n):
    (p_ref, hist_ref, mu_ref, w0_ref, w2_ref, a0_ref, a2_ref, g2_ref, kk_ref, ka_ref, rk_ref,
     ones_ref) = refs[:12]
    refs = refs[12:]
    if with_n:
        l_ref, refs = refs[0], refs[1:]
    r_ref, w_ref, k_ref, v_ref, na_ref, kb_ref, g_ref, bonus_ref, st_ref = refs[:9]
    refs = refs[9:]
    if with_n:
        n_ref, refs = refs[0], refs[1:]
    (carry_ref,) = refs
    p = p_ref[...]
    if long_seq:
        @pl.when(pl.program_id(0) % tps == 0)
        def _():
            carry_ref[...] = hist_ref[...]
        hist = carry_ref[...]
    else:
        hist = hist_ref[...]
    (prev,) = _prev_rows(p, hist, (1,), long_seq)
    if long_seq:
        carry_ref[...] = p[p.shape[0] - SUBLANES:]
        st_ref[...] = p[p.shape[0] - SUBLANES:]
    else:
        st_ref[...] = p
    xs = p + (prev - p) * mu_ref[...]
    r = xs[:, 0:RWKV_W]
    k = xs[:, RWKV_W:2 * RWKV_W]
    v = xs[:, 2 * RWKV_W:3 * RWKV_W]
    xwa = xs[:, 3 * RWKV_W:3 * RWKV_W + 128]
    xg = xs[:, 3 * RWKV_W + 128:RWKV_COLS]
    zw = w0_ref[...] + _dot(jnp.tanh(xwa).astype(BF16), w2_ref[...])
    w_log = -_softplus(-zw) - 0.5
    decay = jnp.exp(-jnp.exp(w_log))
    a = jax.nn.sigmoid(a0_ref[...] + _dot(xwa.astype(BF16), a2_ref[...]))
    g = _dot(jax.nn.sigmoid(xg).astype(BF16), g2_ref[...])
    ones = ones_ref[...]
    kk = k * kk_ref[...]
    kk = kk / jnp.maximum(jnp.sqrt(_split_dot(kk * kk, ones)), 1e-12)
    kf = k * (1.0 + (a - 1.0) * ka_ref[...])
    r_ref[...] = r
    w_ref[...] = decay
    k_ref[...] = kf
    v_ref[...] = v
    na_ref[...] = -kk
    kb_ref[...] = kk * a
    g_ref[...] = g
    bonus_ref[...] = _split_dot(r * kf * rk_ref[...], ones) * v
    if with_n:
        c = RWKV_CHUNK
        pw = 2 * RWKV_N
        lw = jnp.log(decay)
        cw = _split_dot_left(l_ref[...], lw)
        at = _split(-kk * jnp.exp(cw - lw))
        bt = _split(kk * a * jnp.exp(-cw))
        strict = (lax.broadcasted_iota(jnp.int32, (c, c), 1)
                  < lax.broadcasted_iota(jnp.int32, (c, c), 0))
        lane = lax.broadcasted_iota(jnp.int32, (p.shape[0], pw), 1)
        for pr in range(RWKV_HEADS // 2):
            sl = slice(pr * pw, (pr + 1) * pw)
            for hh in range(2):
                m = (lane < RWKV_N) if hh == 0 else (lane >= RWKV_N)
                ah = jnp.where(m, at[0][:, sl], jnp.zeros_like(at[0][:, sl]))
                al = jnp.where(m, at[1][:, sl], jnp.zeros_like(at[1][:, sl]))
                for ci in range(p.shape[0] // c):
                    rs = slice(ci * c, (ci + 1) * c)
                    bh, bl = bt[0][rs, sl], bt[1][rs, sl]
                    nm = _dot_nt(ah[rs], bh) + (_dot_nt(ah[rs], bl) + _dot_nt(al[rs], bh))
                    n_ref[2 * pr + hh, ci] = jnp.where(strict, nm, 0.0)


def _rwkv_pre(proj, hist, lp, ones, seq_len, tm, lmat=None):
    n = proj.shape[0]
    long_seq = seq_len > SUBLANES
    tps = max(seq_len // tm, 1)
    with_n = lmat is not None
    row = lambda c: pl.BlockSpec((1, c), lambda i: (0, 0))
    full = lambda a, b: pl.BlockSpec((a, b), lambda i: (0, 0))
    if long_seq:
        hist_spec = pl.BlockSpec((SUBLANES, RWKV_COLS), lambda i: (i // tps, 0))
    else:
        hist_spec = pl.BlockSpec((tm, RWKV_COLS), lambda i: (i, 0))
    tile = pl.BlockSpec((tm, RWKV_W), lambda i: (i, 0))
    body = functools.partial(_rwkv_pre_body, long_seq=long_seq, tps=tps, with_n=with_n)
    in_specs = [pl.BlockSpec((tm, RWKV_COLS), lambda i: (i, 0)), hist_spec,
                row(RWKV_COLS), row(RWKV_W), full(128, RWKV_W), row(RWKV_W),
                full(128, RWKV_W), full(128, RWKV_W), row(RWKV_W), row(RWKV_W), row(RWKV_W),
                full(RWKV_W, RWKV_W)]
    args = [proj, hist, lp["mu"], lp["w0"], lp["w2p"], lp["a0"], lp["a2p"], lp["g2"],
            lp["k_k"], lp["k_a"], lp["r_k"], ones]
    out_specs = [tile] * 8 + [hist_spec]
    out_shape = [jax.ShapeDtypeStruct((n, RWKV_W), F32)] * 8 + [jax.ShapeDtypeStruct(hist.shape, F32)]
    if with_n:
        c = RWKV_CHUNK
        in_specs.append(full(tm, tm))
        args.append(lmat)
        out_specs.append(pl.BlockSpec((RWKV_HEADS, tm // c, c, c), lambda i: (0, i, 0, 0)))
        out_shape.append(jax.ShapeDtypeStruct((RWKV_HEADS, n // c, c, c), F32))
    return pl.pallas_call(
        body, grid=(n // tm,), in_specs=in_specs, out_specs=out_specs, out_shape=out_shape,
        scratch_shapes=[pltpu.VMEM((SUBLANES, RWKV_COLS), F32)],
        compiler_params=_cp(1), name="rwkv_pre")(*args)


def _rwkv_post_body(y_ref, bonus_ref, g_ref, lw_ref, lb_ref, ones_ref, o_ref):
    y = y_ref[...]
    ones = ones_ref[...]
    mean = _split_dot(y, ones) * (1.0 / RWKV_N)
    d = y - mean
    var = _split_dot(d * d, ones) * (1.0 / RWKV_N)
    yn = d * lax.rsqrt(var + RWKV_LN_EPS)
    o_ref[...] = (yn * lw_ref[...] + lb_ref[...] + bonus_ref[...]) * g_ref[...]


def _rwkv_post(y, bonus, g, lw, lb, ones, tm):
    n = y.shape[0]
    tile = pl.BlockSpec((tm, RWKV_W), lambda i: (i, 0))
    row = pl.BlockSpec((1, RWKV_W), lambda i: (0, 0))
    return pl.pallas_call(
        _rwkv_post_body, grid=(n // tm,),
        in_specs=[tile, tile, tile, row, row, pl.BlockSpec((RWKV_W, RWKV_W), lambda i: (0, 0))],
        out_specs=tile, out_shape=jax.ShapeDtypeStruct((n, RWKV_W), F32),
        compiler_params=_cp(1), name="rwkv_post")(y, bonus, g, lw, lb, ones)


def _rwkv_lanes_body(r_ref, w_ref, k_ref, v_ref, a_ref, b_ref, s0_ref, y_ref, sf_ref,
                     s_ref, xt_ref, y_buf):
    tb = pl.program_id(1)
    ninst = r_ref.shape[1]

    @pl.when(tb == 0)
    def _():
        for hh in range(2):
            s_ref[hh] = s0_ref[hh]

    def step(tl, carry):
        for idx, ref in enumerate((r_ref, w_ref, k_ref, v_ref, a_ref, b_ref)):
            xt_ref[idx] = ref[tl].T
        for hh in range(2):
            base = hh * RWKV_N
            heads = lambda i: xt_ref[i, base:base + RWKV_N, :]

            def row(rho, c):
                s = s_ref[hh, rho]
                sa = jnp.sum(s * heads(4), axis=0, keepdims=True)
                vrow = xt_ref[3, pl.ds(base + rho, 1), :]
                s = s * heads(1) + sa * heads(5) + vrow * heads(2)
                s_ref[hh, rho] = s
                y_buf[pl.ds(base + rho, 1), :] = jnp.sum(s * heads(0), axis=0, keepdims=True)
                return c

            lax.fori_loop(0, RWKV_N, row, 0, unroll=8)
        y_ref[tl] = y_buf[...].T
        return carry

    lax.fori_loop(0, SUBLANES, step, 0)

    @pl.when(tb == pl.num_programs(1) - 1)
    def _():
        for hh in range(2):
            sf_ref[hh] = s_ref[hh]


def _rwkv_lanes(xs, s0):
    steps, ninst, _ = xs[0].shape
    xspec = pl.BlockSpec((SUBLANES, ninst, 2 * RWKV_N), lambda h, t: (t, 0, h))
    sspec = pl.BlockSpec((2, RWKV_N, RWKV_N, ninst), lambda h, t: (h, 0, 0, 0))
    return pl.pallas_call(
        _rwkv_lanes_body, grid=(RWKV_HEADS // 2, steps // SUBLANES),
        in_specs=[xspec] * 6 + [sspec], out_specs=[xspec, sspec],
        out_shape=[jax.ShapeDtypeStruct((steps, ninst, RWKV_W), F32),
                   jax.ShapeDtypeStruct(s0.shape, F32)],
        scratch_shapes=[pltpu.VMEM((2, RWKV_N, RWKV_N, ninst), F32),
                        pltpu.VMEM((6, 2 * RWKV_N, ninst), F32),
                        pltpu.VMEM((2 * RWKV_N, ninst), F32)],
        compiler_params=_cp(2), name="rwkv_lanes")(*xs, s0)


RWKV_C_CHUNKS = 2


def _tri_solve_body(nt_ref, tt_ref):
    c = RWKV_CHUNK
    ninst = nt_ref.shape[2]
    tt_ref[...] = jnp.zeros_like(tt_ref)
    g = SUBLANES
    nb = 4
    sub = lax.broadcasted_iota(jnp.int32, (g, ninst), 0)
    for t0 in range(0, c, nb):
        ngrp = (t0 + nb - 1) // g + 1
        rs = [[jnp.where(sub + g * j == t0 + i, 1.0, 0.0).astype(F32) for j in range(ngrp)]
              for i in range(nb)]
        for sg in range(-(-t0 // g)):

            def acc(s, carry, sg=sg, t0=t0):
                ts = [tt_ref[s, j * g:(j + 1) * g, :] for j in range(sg + 1)]
                nrow = [nt_ref[t0 + i, pl.ds(s, 1), :] for i in range(nb)]
                return tuple(tuple(carry[i][j] + nrow[i] * ts[j] for j in range(sg + 1))
                             for i in range(nb))

            res = lax.fori_loop(sg * g, min(sg * g + g, t0), acc,
                                tuple(tuple(rs[i][:sg + 1]) for i in range(nb)))
            for i in range(nb):
                rs[i][:sg + 1] = list(res[i])
        for i in range(nb):
            for j in range(i):
                nij = nt_ref[t0 + i, t0 + j:t0 + j + 1, :]
                rs[i] = [x + nij * y for x, y in zip(rs[i], rs[j])]
            tt_ref[t0 + i, 0:ngrp * g, :] = jnp.concatenate(rs[i], axis=0)


def _tri_solve(nt):
    nh, c, _, ninst = nt.shape
    spec = pl.BlockSpec((None, c, c, ninst), lambda h: (h, 0, 0, 0))
    return pl.pallas_call(
        _tri_solve_body, grid=(nh,), in_specs=[spec], out_specs=spec,
        out_shape=jax.ShapeDtypeStruct(nt.shape, F32),
        compiler_params=_cp(1), name="tri_solve")(nt)


def _rwkv_chunk_body(r_ref, w_ref, k_ref, v_ref, a_ref, b_ref, t_ref, l_ref, y_ref, sf_ref, s_ref):
    c = RWKV_CHUNK
    pw = 2 * RWKV_N
    ci = pl.program_id(1)

    @pl.when(ci == 0)
    def _():
        s_ref[...] = jnp.zeros_like(s_ref)

    lmat = l_ref[...]
    ti = lax.broadcasted_iota(jnp.int32, (c, c), 0)
    si = lax.broadcasted_iota(jnp.int32, (c, c), 1)
    strict, incl = si < ti, si <= ti
    lane = lax.broadcasted_iota(jnp.int32, (c, pw), 1)
    lane2 = lax.broadcasted_iota(jnp.int32, (2 * c, pw), 1)
    r2 = lax.broadcasted_iota(jnp.int32, (pw, pw), 0)
    c2 = lax.broadcasted_iota(jnp.int32, (pw, pw), 1)
    same_head = (r2 < RWKV_N) == (c2 < RWKV_N)
    eye = r2 == c2
    bz = lambda x: jnp.zeros_like(x)
    pairs = range(RWKV_HEADS // 2)
    units = [(q, p) for q in range(RWKV_C_CHUNKS) for p in pairs]
    at = lambda ref, u: ref[u[0] * c:(u[0] + 1) * c, u[1] * pw:(u[1] + 1) * pw]
    halves = [(u, hh) for u in units for hh in range(2)]
    lws = {u: jnp.log(at(w_ref, u)) for u in units}
    cws = {u: _split_dot_left(lmat, lws[u]) for u in units}
    vs = {u: at(v_ref, u) for u in units}
    wts = {u: jnp.exp(cws[u]) for u in units}
    rts = {u: at(r_ref, u) * wts[u] for u in units}
    ats = {u: at(a_ref, u) * jnp.exp(cws[u] - lws[u]) for u in units}
    lhs = {u: _split(jnp.concatenate([ats[u], rts[u]], axis=0)) for u in units}
    bts = {u: _split(at(b_ref, u) * jnp.exp(-cws[u])) for u in units}
    kts = {u: _split(at(k_ref, u) * jnp.exp(-cws[u])) for u in units}
    a_ak, a_rk, a_rb = {}, {}, {}
    for u, hh in halves:
        m2 = (lane2 < RWKV_N) if hh == 0 else (lane2 >= RWKV_N)
        lh = jnp.where(m2, lhs[u][0], bz(lhs[u][0]))
        ll = jnp.where(m2, lhs[u][1], bz(lhs[u][1]))
        pak = (_dot_nt(lh[:c], kts[u][0])
               + (_dot_nt(lh[:c], kts[u][1]) + _dot_nt(ll[:c], kts[u][0])))
        a_ak[u, hh] = jnp.where(strict, pak, 0.0)
        a_rk[u, hh] = jnp.where(incl, _dot_nt(lh[c:], kts[u][0]), 0.0).astype(BF16)
        a_rb[u, hh] = jnp.where(incl, _dot_nt(lh[c:], bts[u][0]), 0.0).astype(BF16)
    vps = {uh: _dot3(a_ak[uh], vs[uh[0]]) for uh in halves}
    gs = {(u, hh): _dot3(t_ref[2 * u[1] + hh, u[0]], jnp.concatenate([ats[u], vps[u, hh]], axis=1))
          for u, hh in halves}
    pick = lambda x0, x1: jnp.where(lane < RWKV_N, x0, x1)
    ahats = {u: pick(gs[u, 0][:, :pw], gs[u, 1][:, :pw]) for u in units}
    vhats = {u: pick(gs[u, 0][:, pw:], gs[u, 1][:, pw:]) for u in units}
    ahb = {u: ahats[u].astype(BF16) for u in units}
    vhb = {u: vhats[u].astype(BF16) for u in units}
    vbs = {u: vs[u].astype(BF16) for u in units}
    rhats = {u: (rts[u] + pick(_dot(a_rb[u, 0], ahb[u]), _dot(a_rb[u, 1], ahb[u]))).astype(BF16)
             for u in units}
    yhats = {u: pick(_dot(a_rb[u, 0], vhb[u]) + _dot(a_rk[u, 0], vbs[u]),
                     _dot(a_rb[u, 1], vhb[u]) + _dot(a_rk[u, 1], vbs[u])) for u in units}
    wends = {u: jnp.exp(cws[u][c - 1:c] - cws[u]) for u in units}
    bkts = {u: jnp.concatenate([at(b_ref, u) * wends[u], at(k_ref, u) * wends[u]], axis=0).T
            for u in units}
    mpts = {u: jnp.where(same_head, _dot3(bkts[u], jnp.concatenate([ahats[u], bz(ahats[u])], axis=0)), 0.0)
            + jnp.where(eye, wts[u][c - 1:c], 0.0) for u in units}
    zpts = {u: jnp.where(same_head, _dot3(bkts[u], jnp.concatenate([vhats[u], vs[u]], axis=0)), 0.0)
            for u in units}
    spts = [s_ref[p] for p in pairs]
    for q in range(RWKV_C_CHUNKS):
        for p in pairs:
            y_ref[q * c:(q + 1) * c, p * pw:(p + 1) * pw] = (
                _dot(rhats[q, p], spts[p].astype(BF16)) + yhats[q, p])
        spts = [_dot3(mpts[q, p], spts[p]) + zpts[q, p] for p in pairs]
    for p in pairs:
        s_ref[p] = spts[p]

    @pl.when(ci == pl.num_programs(1) - 1)
    def _():
        sf_ref[...] = s_ref[...]


def _rwkv_chunk(xs, t4, lmat, nseq):
    n = xs[0].shape[0]
    c = RWKV_CHUNK
    q = RWKV_C_CHUNKS
    nch = n // (q * c) // nseq
    pw = 2 * RWKV_N
    tile = pl.BlockSpec((q * c, RWKV_W), lambda b, i: (b * nch + i, 0))
    return pl.pallas_call(
        _rwkv_chunk_body, grid=(nseq, nch),
        in_specs=[tile] * 6 + [pl.BlockSpec((RWKV_HEADS, q, c, c), lambda b, i: (0, b * nch + i, 0, 0)),
                               pl.BlockSpec((c, c), lambda b, i: (0, 0))],
        out_specs=[tile, pl.BlockSpec((None, RWKV_HEADS // 2, pw, pw), lambda b, i: (b, 0, 0, 0))],
        out_shape=[jax.ShapeDtypeStruct((n, RWKV_W), F32),
                   jax.ShapeDtypeStruct((nseq, RWKV_HEADS // 2, pw, pw), F32)],
        scratch_shapes=[pltpu.VMEM((RWKV_HEADS // 2, pw, pw), F32)],
        compiler_params=_cp(2), name="rwkv_chunk")(*xs, t4, lmat)


def _hgrn_lower_bound(logits, layer):
    m = jnp.max(logits, axis=0, keepdims=True)
    e = jnp.exp(logits - m)
    gam = e / jnp.sum(e, axis=0, keepdims=True)
    cs = gam[0:1]
    for i in range(1, layer + 1):
        cs = cs + gam[i:i + 1]
    return cs - gam[0:1]


def _hgrn_level_matrix():
    t = np.arange(HGRN_CHUNK)[:, None]
    s = np.arange(HGRN_CHUNK)[None, :]
    mats = []
    for upper in (False, True):
        for e in (1, 2):
            same = (t >> e) == (s >> e)
            mats.append(same & ((s > t) if upper else (s <= t)))
    mats.append(s <= t)
    return jnp.asarray(np.concatenate(mats, axis=0).astype(np.float32), dtype=BF16)


def _hgrn_prompt_body(*refs, layer):
    npair = HGRN_HEADS // 2
    q_refs, f_refs, i_refs, g_refs = (refs[i * npair:(i + 1) * npair] for i in range(4))
    lg_ref, nw_ref, cm_ref, o_ref, sf_ref, s_ref = refs[4 * npair:]
    c = pl.program_id(1)

    @pl.when(c == 0)
    def _():
        s_ref[...] = jnp.zeros_like(s_ref)

    n = HGRN_CHUNK
    heads = range(HGRN_HEADS)
    col = lambda rs, h: rs[h // 2][:, (h % 2) * HGRN_D:(h % 2 + 1) * HGRN_D]
    hsl = lambda h: slice(h * HGRN_D, (h + 1) * HGRN_D)
    t = lax.broadcasted_iota(jnp.int32, (n, n), 0)
    s_ = lax.broadcasted_iota(jnp.int32, (n, n), 1)
    lb_all = _hgrn_lower_bound(lg_ref[...], layer)
    cm = cm_ref[...]
    ones = jnp.ones((n, n), BF16)
    fgs = [lb_all[:, hsl(h)] + (1.0 - lb_all[:, hsl(h)]) * jax.nn.sigmoid(col(f_refs, h))
           for h in heads]
    lfs = [jnp.log(fg) for fg in fgs]
    css = [_split_dot_left(cm, lf) for lf in lfs]
    bs = [cs[4 * n:5 * n] for cs in css]

    def cl(h, e):
        if e <= 2:
            return css[h][(e - 1) * n:e * n]
        if e == HGRN_LEVELS:
            return bs[h]
        m = 1 << e
        b3 = bs[h].reshape(n // m, m, HGRN_D)
        ends = jnp.broadcast_to(b3[:, m - 1:m, :], b3.shape)
        before = jnp.concatenate([jnp.zeros_like(ends[:1]), ends[:-1]], axis=0)
        return (b3 - before).reshape(n, HGRN_D)

    def cu(h, e):
        if e <= 2:
            return css[h][(2 + e - 1) * n:(2 + e) * n]
        if e == HGRN_LEVELS:
            return bs[h][n - 1:n] - bs[h]
        m = 1 << e
        b3 = bs[h].reshape(n // m, m, HGRN_D)
        return (jnp.broadcast_to(b3[:, m - 1:m, :], b3.shape) - b3).reshape(n, HGRN_D)
    qs = [col(q_refs, h) * jax.nn.sigmoid(col(q_refs, h)) for h in heads]
    ks = [1.0 - fg for fg in fgs]
    vbs = [col(i_refs, h).astype(BF16) for h in heads]
    acc = [jnp.where(t == s_, _dot_nt(qs[h].astype(BF16), ks[h].astype(BF16)), 0.0) for h in heads]
    for e in range(HGRN_LEVELS):
        mask = (((t >> e) & 1) == 1) & ((s_ >> e) == ((t >> e) - 1))
        for h in heads:
            qe = qs[h] * (fgs[h] if e == 0 else jnp.exp(cl(h, e)))
            ke = ks[h] if e == 0 else ks[h] * jnp.exp(cu(h, e))
            acc[h] = acc[h] + jnp.where(mask, _dot_nt(qe.astype(BF16), ke.astype(BF16)), 0.0)
    sts = [s_ref[h] for h in heads]
    os_ = [_dot(acc[h].astype(BF16), vbs[h])
           + _dot((qs[h] * jnp.exp(cl(h, HGRN_LEVELS))).astype(BF16), sts[h].astype(BF16))
           for h in heads]
    dks = [jnp.exp(_split_dot(lfs[h].T, ones)) for h in heads]
    for h in heads:
        kf = ks[h] * jnp.exp(cu(h, HGRN_LEVELS))
        s_ref[h] = dks[h] * sts[h] + _dot(kf.T.astype(BF16), vbs[h])
    for h in heads:
        o = os_[h]
        ms = jnp.mean(o * o, axis=-1, keepdims=True)
        gr = col(g_refs, h)
        o_ref[:, hsl(h)] = (o * lax.rsqrt(ms + NORM_EPS) * nw_ref[:, hsl(h)]) * (gr * jax.nn.sigmoid(gr))

    @pl.when(c == pl.num_programs(1) - 1)
    def _():
        sf_ref[...] = s_ref[...]


def _split_dot_left(m, x):
    hi, lo = _split(x)
    return _dot(m, hi) + _dot(m, lo)


def _hgrn_prompt(proj, logits, nw, cmat, nseq, seq_len, layer):
    n = proj.shape[0]
    nch = seq_len // HGRN_CHUNK
    wb = 2 * HGRN_D
    npair = HGRN_HEADS // 2
    cols = [pl.BlockSpec((HGRN_CHUNK, wb), lambda b, c, j=(HGRN_COL0 + g * HGRN_W) // wb + p:
                         (b * nch + c, j)) for g in range(4) for p in range(npair)]
    return pl.pallas_call(
        functools.partial(_hgrn_prompt_body, layer=layer),
        grid=(nseq, nch),
        in_specs=cols + [pl.BlockSpec(logits.shape, lambda b, c: (0, 0)),
                         pl.BlockSpec((1, HGRN_W), lambda b, c: (0, 0)),
                         pl.BlockSpec(cmat.shape, lambda b, c: (0, 0))],
        out_specs=[pl.BlockSpec((HGRN_CHUNK, HGRN_W), lambda b, c: (b * nch + c, 0)),
                   pl.BlockSpec((None, HGRN_HEADS, HGRN_D, HGRN_D), lambda b, c: (b, 0, 0, 0))],
        out_shape=[jax.ShapeDtypeStruct((n, HGRN_W), F32),
                   jax.ShapeDtypeStruct((nseq, HGRN_HEADS, HGRN_D, HGRN_D), F32)],
        scratch_shapes=[pltpu.VMEM((HGRN_HEADS, HGRN_D, HGRN_D), F32)],
        compiler_params=_cp(2), name="hgrn_prompt")(*([proj] * (4 * npair)), logits, nw, cmat)


HGRN_KT = 32


def _hgrn_lanes_body(q_ref, f_ref, i_ref, g_ref, lg_ref, nw_ref, s0_ref, o_ref, sf_ref,
                     qt_ref, ft_ref, kt_ref, vt_ref, oacc_ref, *, layer):
    kt = pl.program_id(1)
    lb = _hgrn_lower_bound(lg_ref[...], layer)

    @pl.when(kt == 0)
    def _():
        oacc_ref[...] = jnp.zeros_like(oacc_ref)
        for tl in range(SUBLANES):
            qr = q_ref[tl]
            fg = lb + (1.0 - lb) * jax.nn.sigmoid(f_ref[tl])
            qt_ref[tl] = (qr * jax.nn.sigmoid(qr)).T
            ft_ref[tl] = fg.T
            kt_ref[tl] = (1.0 - fg).T
            vt_ref[tl] = i_ref[tl].T

    def step(tl, carry, src_ref=sf_ref):
        vt = vt_ref[tl]

        def krow(kk, o):
            row = kt * HGRN_KT + kk
            s = ft_ref[tl, pl.ds(row, 1), :] * src_ref[0, kk] + kt_ref[tl, pl.ds(row, 1), :] * vt
            sf_ref[0, kk] = s
            return o + qt_ref[tl, pl.ds(row, 1), :] * s

        oacc_ref[tl] = lax.fori_loop(0, HGRN_KT, krow, oacc_ref[tl], unroll=2)
        return carry

    step(0, 0, s0_ref)
    lax.fori_loop(1, SUBLANES, step, 0)

    @pl.when(kt == pl.num_programs(1) - 1)
    def _():
        for tl in range(SUBLANES):
            o = oacc_ref[tl].T
            ms = jnp.mean(o * o, axis=-1, keepdims=True)
            gr = g_ref[tl]
            o_ref[tl] = (o * lax.rsqrt(ms + NORM_EPS) * nw_ref[...]) * (gr * jax.nn.sigmoid(gr))


def _hgrn_lanes(proj_t, logits, nw, s0, layer):
    steps, ninst, _ = proj_t.shape
    col = lambda off: pl.BlockSpec((steps, ninst, HGRN_D),
                                   lambda h, k, off=off: (0, 0, off // HGRN_D + h))
    sspec = pl.BlockSpec((1, HGRN_KT, HGRN_D, ninst), lambda h, k: (h, k, 0, 0))
    tbuf = pltpu.VMEM((steps, HGRN_D, ninst), F32)
    return pl.pallas_call(
        functools.partial(_hgrn_lanes_body, layer=layer),
        grid=(HGRN_HEADS, HGRN_D // HGRN_KT),
        in_specs=[col(0), col(HGRN_W), col(2 * HGRN_W), col(3 * HGRN_W),
                  pl.BlockSpec((logits.shape[0], HGRN_D), lambda h, k: (0, h)),
                  pl.BlockSpec((1, HGRN_D), lambda h, k: (0, h)), sspec],
        out_specs=[pl.BlockSpec((steps, ninst, HGRN_D), lambda h, k: (0, 0, h)), sspec],
        out_shape=[jax.ShapeDtypeStruct((steps, ninst, HGRN_W), F32),
                   jax.ShapeDtypeStruct(s0.shape, F32)],
        scratch_shapes=[tbuf, tbuf, tbuf, tbuf, tbuf],
        compiler_params=_cp(2), name="hgrn_lanes")(
            proj_t, proj_t, proj_t, proj_t, logits, nw, s0)


def _rglru_body(xb_ref, gate_ref, chist_ref, hinit_ref, cw_ref, cb_ref, wa_ref, ba_ref, wx_ref,
                bx_ref, lam_ref, y_ref, cst_ref, hst_ref, ccarry_ref, hcarry_ref,
                *, long_seq, tps):
    x = xb_ref[...]
    tm, c = x.shape
    if long_seq:
        @pl.when(pl.program_id(0) % tps == 0)
        def _():
            ccarry_ref[...] = chist_ref[...]
            hcarry_ref[...] = hinit_ref[...]
        hist = ccarry_ref[...]
    else:
        hist = chist_ref[...]
    p1, p2, p3 = _prev_rows(x, hist, (1, 2, 3), long_seq)
    cw = cw_ref[...]
    xc = cb_ref[...] + cw[0:1] * p3 + cw[1:2] * p2 + cw[2:3] * p1 + cw[3:4] * x
    xcb = xc.astype(BF16)
    r = jax.nn.sigmoid(_dot(xcb, wa_ref[...]) + ba_ref[...])
    ig = jax.nn.sigmoid(_dot(xcb, wx_ref[...]) + bx_ref[...])
    log_a = (-LRU_C) * r * _softplus(-lam_ref[...])
    a = jnp.exp(log_a)
    h = jnp.sqrt(-_expm1(2.0 * log_a)) * (ig * xc)
    if long_seq:
        ng = tm // SUBLANES
        shp = (ng, SUBLANES, c)
        h3, a3 = h.reshape(shp), a.reshape(shp)
        pos = lax.broadcasted_iota(jnp.int32, shp, 1)
        k = 1
        while k < SUBLANES:
            keep = pos >= k
            h3 = h3 + a3 * jnp.where(keep, pltpu.roll(h3, k, 1), 0.0)
            a3 = a3 * jnp.where(keep, pltpu.roll(a3, k, 1), 1.0)
            k *= 2
        h_in = hcarry_ref[SUBLANES - 1:SUBLANES, :]
        groups = []
        for gi in range(ng):
            hg = h3[gi] + a3[gi] * h_in
            groups.append(hg)
            h_in = hg[SUBLANES - 1:SUBLANES, :]
        h = jnp.concatenate(groups, axis=0)
        hcarry_ref[...] = h[tm - SUBLANES:]
        ccarry_ref[...] = x[tm - SUBLANES:]
        cst_ref[...] = x[tm - SUBLANES:]
        hst_ref[...] = h[tm - SUBLANES:]
    else:
        shp = (tm // SUBLANES, SUBLANES, c)
        h3, a3 = h.reshape(shp), a.reshape(shp)
        pos = lax.broadcasted_iota(jnp.int32, shp, 1)
        k = 1
        while k < SUBLANES:
            keep = pos >= k
            h3 = h3 + a3 * jnp.where(keep, pltpu.roll(h3, k, 1), 0.0)
            a3 = a3 * jnp.where(keep, pltpu.roll(a3, k, 1), 1.0)
            k *= 2
        h = (h3 + a3 * hinit_ref[...].reshape(shp)).reshape(tm, c)
        cst_ref[...] = x
        hst_ref[...] = h
    y_ref[...] = h * jax.nn.gelu(gate_ref[...])


def _rglru(proj, chist, hinit, lp, seq_len, tm):
    n = proj.shape[0]
    long_seq = seq_len > SUBLANES
    tps = max(seq_len // tm, 1)
    c0 = LRU_COL0 // LRU_W
    if long_seq:
        hs = pl.BlockSpec((SUBLANES, LRU_W), lambda i: (i // tps, 0))
    else:
        hs = pl.BlockSpec((tm, LRU_W), lambda i: (i, 0))
    row = pl.BlockSpec((1, LRU_W), lambda i: (0, 0))
    sq = pl.BlockSpec((LRU_W, LRU_W), lambda i: (0, 0))
    return pl.pallas_call(
        functools.partial(_rglru_body, long_seq=long_seq, tps=tps), grid=(n // tm,),
        in_specs=[pl.BlockSpec((tm, LRU_W), lambda i: (i, c0)),
                  pl.BlockSpec((tm, LRU_W), lambda i: (i, c0 + 1)), hs, hs,
                  pl.BlockSpec((4, LRU_W), lambda i: (0, 0)), row, sq, row, sq, row, row],
        out_specs=[pl.BlockSpec((tm, LRU_W), lambda i: (i, 0)), hs, hs],
        out_shape=[jax.ShapeDtypeStruct((n, LRU_W), F32),
                   jax.ShapeDtypeStruct(chist.shape, F32),
                   jax.ShapeDtypeStruct(chist.shape, F32)],
        scratch_shapes=[pltpu.VMEM((SUBLANES, LRU_W), F32), pltpu.VMEM((SUBLANES, LRU_W), F32)],
        compiler_params=_cp(1), name="rglru")(
            proj, proj, chist, hinit, lp["lru_cw"], lp["lru_cb"], lp["lru_wa"], lp["lru_ba"],
            lp["lru_wx"], lp["lru_bx"], lp["lru_lam"])


def _hist(state):
    nseq, k, c = state.shape
    return jnp.pad(state, ((0, 0), (SUBLANES - k, 0), (0, 0))).reshape(nseq * SUBLANES, c)


def _block_diag(w):
    h, a, b = w.shape
    eye = jnp.eye(h, dtype=w.dtype)
    return (eye[:, None, :, None] * w[:, :, None, :]).reshape(h * a, h * b)


def _layer(x3, st, lp, consts, layer, final_nw, w_in_at, w_out_at, ffn_w):
    nseq, seq_len, d = x3.shape
    n = nseq * seq_len
    long_seq = seq_len > SUBLANES
    s_rw, shift_rw, s_hg, h_lru, buf_lru, buf_ffn = st
    x = x3.reshape(n, d)
    ones = consts["ones"]

    w_in, w_in_layer = w_in_at
    proj, w_in_bf16 = _in_proj(x, lp["norm_mix"], w_in, w_in_layer, tm=min(n, 1024),
                               tn=IN_COLS // 4 if w_in.dtype == BF16 else 512)

    tm_pre = min(n, seq_len if long_seq else n, 256)
    lmat_pre = None
    if long_seq:
        tril = np.tril(np.ones((RWKV_CHUNK, RWKV_CHUNK)))
        lmat_pre = jnp.asarray(np.kron(np.eye(tm_pre // RWKV_CHUNK), tril), dtype=BF16)
    outs = _rwkv_pre(proj, _hist(shift_rw[:, None, :]), lp, ones, seq_len, tm_pre, lmat_pre)
    r, w, k, v, na, kb, g, bonus, shift_out = outs[:9]
    new_shift = shift_out.reshape(nseq, SUBLANES, RWKV_COLS)[:, SUBLANES - 1]
    if long_seq:
        c = RWKV_CHUNK
        t4 = _tri_solve(outs[9].transpose(0, 2, 3, 1)).transpose(0, 3, 1, 2)
        y, spt = _rwkv_chunk((r, w, k, v, na, kb), t4, consts["lmat"], nseq)
        sp = spt.reshape(nseq, RWKV_HEADS // 2, 2, RWKV_N, 2, RWKV_N)
        new_s_rw = jnp.stack([sp[:, :, 0, :, 0, :], sp[:, :, 1, :, 1, :]], axis=2)
        new_s_rw = new_s_rw.reshape(nseq, RWKV_HEADS, RWKV_N, RWKV_N).transpose(0, 1, 3, 2)
    else:
        tmaj = lambda t: t.reshape(nseq, seq_len, RWKV_W).transpose(1, 0, 2)
        s0 = s_rw.transpose(1, 2, 3, 0)
        yb, sf = _rwkv_lanes([tmaj(t) for t in (r, w, k, v, na, kb)], s0)
        y = yb.transpose(1, 0, 2).reshape(n, RWKV_W)
        new_s_rw = sf.transpose(3, 0, 1, 2)
    y_rw = _rwkv_post(y, bonus, g, lp["ln_w"], lp["ln_b"], ones, tm=min(n, 512))

    if long_seq:
        y_hg, new_s_hg = _hgrn_prompt(proj, consts["lb_logits"], lp["hgrn_nw"], consts["cmat"],
                                      nseq, seq_len, layer)
    else:
        proj_t = proj[:, HGRN_COL0:LRU_COL0].reshape(nseq, seq_len, 4 * HGRN_W).transpose(1, 0, 2)
        s0 = s_hg.transpose(1, 2, 3, 0)
        o_t, sf = _hgrn_lanes(proj_t, consts["lb_logits"], lp["hgrn_nw"], s0, layer)
        y_hg = o_t.transpose(1, 0, 2).reshape(n, HGRN_W)
        new_s_hg = sf.transpose(3, 0, 1, 2)

    if long_seq:
        hinit = _hist(h_lru[:, None, :])
        tm_lru = min(seq_len, 256)
    else:
        hinit = jnp.broadcast_to(h_lru[:, None, :], (nseq, seq_len, LRU_W)).reshape(n, LRU_W)
        tm_lru = min(n, 512)
    y_lru, cst, hst = _rglru(proj, _hist(buf_lru), hinit, lp, seq_len, tm_lru)
    new_buf_lru = cst.reshape(nseq, SUBLANES, LRU_W)[:, SUBLANES - 3:]
    new_h_lru = hst.reshape(nseq, SUBLANES, LRU_W)[:, SUBLANES - 1]

    w_out, w_out_layer = w_out_at
    x, w_out_bf16 = _out_proj(x, y_rw, y_hg, y_lru, w_out, w_out_layer, tm=min(n, 1024),
                              tn=1024 if w_out.dtype == BF16 else 512)
    tm_ffn = min(n, seq_len if long_seq else n, 1024)
    tf_ffn = 512 if ffn_w[0][0].dtype == BF16 else 256
    outs = _ffn(x, lp["norm_ffn"], ffn_w, lp["ffn_cw"], lp["ffn_cb"], _hist(buf_ffn), seq_len,
                tm=tm_ffn, tf=tf_ffn, final_nw=final_nw)
    x, fst = outs[:2]
    new_buf_ffn = fst.reshape(nseq, -1, SUBLANES, D_FF)[:, -1, SUBLANES - 2:]
    states = (new_s_rw, new_shift, new_s_hg, new_h_lru, new_buf_lru, new_buf_ffn)
    ffn_w_bf16 = None
    if len(outs) > 2:
        ffn_w_bf16 = ((outs[2], 0, 0), (outs[3], 0, 0), (outs[4], 0))
    return x.reshape(nseq, seq_len, d), states, (w_in_bf16, w_out_bf16, ffn_w_bf16)


def kernel(x_prompt, x_sample, state_rwkv, state_rwkv_shift, state_hgrn, state_rglru, cache_rglru_conv, cache_ffn_conv, norm_mix, w_in, rwkv_mu, rwkv_w0, rwkv_w2, rwkv_a0, rwkv_a2, rwkv_g2, rwkv_k_k, rwkv_k_a, rwkv_r_k, rwkv_ln_w, rwkv_ln_b, hgrn_lb_logits, hgrn_norm_w, rglru_conv_w, rglru_conv_b, rglru_wa, rglru_ba, rglru_wx, rglru_bx, rglru_lambda, w_out, norm_ffn, ffn_w_up, ffn_conv_w, ffn_conv_b, ffn_w_down, norm_final):
    depth = w_in.shape[0]
    nb, nt, _ = x_prompt.shape
    db = x_sample.shape[0]
    consts = {
        "ones": jnp.asarray(np.kron(np.eye(RWKV_HEADS), np.ones((RWKV_N, RWKV_N))), dtype=BF16),
        "cmat": _hgrn_level_matrix(),
        "lmat": jnp.asarray(np.tril(np.ones((RWKV_CHUNK, RWKV_CHUNK), np.float32)), dtype=BF16),
        "lb_logits": hgrn_lb_logits,
    }
    rowv = lambda a: a.reshape(1, -1)
    x_p, x_s = x_prompt, x_sample
    new_p, new_s = [], []
    for l in range(depth):
        lp = {
            "norm_mix": rowv(norm_mix[l]), "mu": rowv(rwkv_mu[l]),
            "w0": rowv(rwkv_w0[l]),
            "w2p": jnp.pad(rwkv_w2[l], ((0, 64), (0, 0))).astype(BF16),
            "a0": rowv(rwkv_a0[l]),
            "a2p": jnp.pad(rwkv_a2[l], ((64, 0), (0, 0))).astype(BF16),
            "g2": rwkv_g2[l].astype(BF16), "k_k": rowv(rwkv_k_k[l]), "k_a": rowv(rwkv_k_a[l]),
            "r_k": rowv(rwkv_r_k[l]), "ln_w": rowv(rwkv_ln_w[l]), "ln_b": rowv(rwkv_ln_b[l]),
            "hgrn_nw": rowv(hgrn_norm_w[l]),
            "lru_cw": rglru_conv_w[l], "lru_cb": rowv(rglru_conv_b[l]),
            "lru_wa": _block_diag(rglru_wa[l]).astype(BF16), "lru_ba": rowv(rglru_ba[l]),
            "lru_wx": _block_diag(rglru_wx[l]).astype(BF16), "lru_bx": rowv(rglru_bx[l]),
            "lru_lam": rowv(rglru_lambda[l]),
            "norm_ffn": rowv(norm_ffn[l]), "ffn_cw": ffn_conv_w[l],
            "ffn_cb": rowv(ffn_conv_b[l]),
        }
        zero = lambda *s: jnp.zeros(s, F32)
        st_p = (None, zero(nb, RWKV_COLS), None, zero(nb, LRU_W), zero(nb, 3, LRU_W),
                zero(nb, 2, D_FF))
        final_nw = rowv(norm_final) if l == depth - 1 else None
        st_s = (state_rwkv[l], state_rwkv_shift[l], state_hgrn[l], state_rglru[l],
                cache_rglru_conv[l], cache_ffn_conv[l])
        ffn_f32 = ((ffn_w_up, l, 0), (ffn_w_up, l, D_FF), (ffn_w_down, l))
        x_s, ss, (w_in_l, w_out_l, ffn_l) = _layer(x_s, st_s, lp, consts, l, final_nw, (w_in, l),
                                                   (w_out, l), ffn_f32)
        x_p, sp, _ = _layer(x_p, st_p, lp, consts, l, final_nw, (w_in_l, 0), (w_out_l, 0), ffn_l)
        new_p.append(sp)
        new_s.append(ss)
    stack = lambda sts: [jnp.stack(s, axis=0) for s in zip(*sts)]
    return (x_p, x_s, *stack(new_p), *stack(new_s))
```

```python
import functools

import numpy as np
import jax
import jax.numpy as jnp
from jax import lax
from jax.experimental import pallas as pl
from jax.experimental.pallas import tpu as pltpu

F32 = jnp.float32
BF16 = jnp.bfloat16

D_MODEL = 2048
RWKV_HEADS = 12
RWKV_N = 64
RWKV_W = RWKV_HEADS * RWKV_N
RWKV_COLS = 2560
RWKV_LN_EPS = 64e-5
HGRN_HEADS = 6
HGRN_D = 128
HGRN_W = HGRN_HEADS * HGRN_D
HGRN_COL0 = RWKV_COLS
LRU_W = 512
LRU_COL0 = RWKV_COLS + 4 * HGRN_W
LRU_C = 8.0
IN_COLS = 6656
D_FF = 5632
NORM_EPS = 1e-6

SUBLANES = 8
LANES = 128
MXU_WIDTH = 256
RWKV_CHUNK = 64
HGRN_CHUNK = 128
HGRN_LEVELS = 7
VMEM_LIMIT = 48 * 1024 * 1024
BIG_VMEM_LIMIT = 56 * 1024 * 1024


def _cp(n, limit=VMEM_LIMIT):
    return pltpu.CompilerParams(dimension_semantics=("arbitrary",) * n, vmem_limit_bytes=limit)


def _dot(a, b):
    return jnp.dot(a, b, preferred_element_type=F32)


def _dot_nt(a, b):
    return lax.dot_general(a, b, (((1,), (1,)), ((), ())), preferred_element_type=F32)


def _split(x):
    hi = x.astype(BF16)
    lo = (x - hi.astype(F32)).astype(BF16)
    return hi, lo


def _split_dot(x, m):
    hi, lo = _split(x)
    return _dot(hi, m) + _dot(lo, m)


def _dot3(a, b, nt=False):
    f = _dot_nt if nt else _dot
    ah, al = _split(a)
    bh, bl = _split(b)
    return f(ah, bh) + (f(ah, bl) + f(al, bh))


def _softplus(z):
    return jnp.maximum(z, 0.0) + jnp.log1p(jnp.exp(-jnp.abs(z)))


def _expm1(z):
    return jnp.tanh(0.5 * z) * (jnp.exp(z) + 1.0)


def _prev_rows(x, hist, ks, long_seq):
    tm, c = x.shape
    if long_seq:
        ext = jnp.concatenate([hist, x], axis=0)
        return [pltpu.roll(ext, k, 0)[SUBLANES:] for k in ks]
    x3 = x.reshape(tm // SUBLANES, SUBLANES, c)
    h3 = hist.reshape(tm // SUBLANES, SUBLANES, c)
    pos = lax.broadcasted_iota(jnp.int32, x3.shape, 1)
    return [jnp.where(pos >= k, pltpu.roll(x3, k, 1), pltpu.roll(h3, k, 1)).reshape(tm, c)
            for k in ks]


def _in_proj_body(x_ref, nw_ref, w_ref, o_ref, *rest):
    xn_ref = rest[-1]

    @pl.when(pl.program_id(1) == 0)
    def _():
        x = x_ref[...]
        ms = jnp.mean(x * x, axis=-1, keepdims=True)
        xn_ref[...] = (x * lax.rsqrt(ms + NORM_EPS) * nw_ref[...]).astype(BF16)
    w = w_ref[...].astype(BF16)
    if len(rest) == 2:
        rest[0][...] = w
    o_ref[...] = _dot(xn_ref[...], w)


def _in_proj(x, nw, w, layer, tm, tn):
    n, d = x.shape
    c = w.shape[2]
    emit = w.dtype != BF16
    assert not emit or n == tm
    out_specs = [pl.BlockSpec((tm, tn), lambda i, j: (i, j))]
    out_shape = [jax.ShapeDtypeStruct((n, c), F32)]
    if emit:
        out_specs.append(pl.BlockSpec((None, d, tn), lambda i, j: (0, 0, j)))
        out_shape.append(jax.ShapeDtypeStruct((1, d, c), BF16))
    outs = pl.pallas_call(
        _in_proj_body, grid=(n // tm, c // tn),
        in_specs=[pl.BlockSpec((tm, d), lambda i, j: (i, 0)),
                  pl.BlockSpec((1, d), lambda i, j: (0, 0)),
                  pl.BlockSpec((None, d, tn), lambda i, j: (layer, 0, j))],
        out_specs=out_specs, out_shape=out_shape,
        scratch_shapes=[pltpu.VMEM((tm, d), BF16)],
        compiler_params=_cp(2, BIG_VMEM_LIMIT), name="in_proj")(x, nw, w)
    return outs if emit else (outs[0], None)


def _out_proj_body(x_ref, ya_ref, yb_ref, yc_ref, w_ref, o_ref, *rest):
    y_ref = rest[-1]

    @pl.when(pl.program_id(1) == 0)
    def _():
        y_ref[:, 0:RWKV_W] = ya_ref[...].astype(BF16)
        y_ref[:, RWKV_W:RWKV_W + HGRN_W] = yb_ref[...].astype(BF16)
        y_ref[:, RWKV_W + HGRN_W:] = yc_ref[...].astype(BF16)
    w = w_ref[...].astype(BF16)
    if len(rest) == 2:
        rest[0][...] = w
    o_ref[...] = x_ref[...] + _dot(y_ref[...], w)


def _out_proj(x, ya, yb, yc, w, layer, tm, tn):
    n, d = x.shape
    emit = w.dtype != BF16
    assert not emit or n == tm
    tile = pl.BlockSpec((tm, tn), lambda i, j: (i, j))
    out_specs, out_shape = [tile], [jax.ShapeDtypeStruct((n, d), F32)]
    if emit:
        out_specs.append(pl.BlockSpec((None, d, tn), lambda i, j: (0, 0, j)))
        out_shape.append(jax.ShapeDtypeStruct((1, d, d), BF16))
    outs = pl.pallas_call(
        _out_proj_body, grid=(n // tm, d // tn),
        in_specs=[tile,
                  pl.BlockSpec((tm, RWKV_W), lambda i, j: (i, 0)),
                  pl.BlockSpec((tm, HGRN_W), lambda i, j: (i, 0)),
                  pl.BlockSpec((tm, LRU_W), lambda i, j: (i, 0)),
                  pl.BlockSpec((None, d, tn), lambda i, j: (layer, 0, j))],
        out_specs=out_specs, out_shape=out_shape,
        scratch_shapes=[pltpu.VMEM((tm, d), BF16)],
        compiler_params=_cp(2, BIG_VMEM_LIMIT), name="out_proj")(x, ya, yb, yc, w)
    return outs if emit else (outs[0], None)


def _ffn_body(*refs, long_seq, tps, final, emit):
    x_ref, nw_ref, wg_ref, wv_ref, cw_ref, cb_ref, wd_ref, hist_ref = refs[:8]
    refs = refs[8:]
    if final:
        fnw_ref, refs = refs[0], refs[1:]
    o_ref, st_ref = refs[:2]
    refs = refs[2:]
    if emit:
        wgb_ref, wvb_ref, wdb_ref = refs[:3]
        refs = refs[3:]
    hn_ref, carry_ref = refs
    i = pl.program_id(0)
    j = pl.program_id(1)

    @pl.when(j == 0)
    def _():
        x = x_ref[...]
        ms = jnp.mean(x * x, axis=-1, keepdims=True)
        hn_ref[...] = (x * lax.rsqrt(ms + NORM_EPS) * nw_ref[...]).astype(BF16)
        o_ref[...] = x

    hn = hn_ref[...]
    if long_seq:
        @pl.when(i % tps == 0)
        def _():
            carry_ref[j] = hist_ref[...]
        hist = carry_ref[j]
    else:
        hist = hist_ref[...]
    tf = wg_ref.shape[1]
    nsplit = min(2, max(1, tf // MXU_WIDTH))
    halves = [slice(q * (tf // nsplit), (q + 1) * (tf // nsplit)) for q in range(nsplit)]
    wgs = [wg_ref[:, sl].astype(BF16) for sl in halves]
    wvs = [wv_ref[:, sl].astype(BF16) for sl in halves]
    wds = [wd_ref[sl, :].astype(BF16) for sl in halves]
    if emit:
        for sl, wg, wv, wd in zip(halves, wgs, wvs, wds):
            wgb_ref[:, sl] = wg
            wvb_ref[:, sl] = wv
            wdb_ref[sl, :] = wd
    gs = [_dot(hn, wg) for wg in wgs]
    vs = [_dot(hn, wv) for wv in wvs]
    cw = cw_ref[...]
    acc = None
    for sl, g, v, wd in zip(halves, gs, vs, wds):
        p1, p2 = _prev_rows(g, hist[:, sl], (1, 2), long_seq)
        gc = cb_ref[:, sl] + cw[0:1, sl] * p2 + cw[1:2, sl] * p1 + cw[2:3, sl] * g
        h = (gc * jax.nn.sigmoid(gc)) * v
        d = _dot(h.astype(BF16), wd)
        acc = d if acc is None else acc + d
        tail = g if not long_seq else g[g.shape[0] - SUBLANES:]
        st_ref[:, sl] = tail
        if long_seq:
            carry_ref[j, :, sl] = tail
    o_ref[...] += acc

    if final:
        @pl.when(j == pl.num_programs(1) - 1)
        def _():
            y = o_ref[...]
            ms = jnp.mean(y * y, axis=-1, keepdims=True)
            o_ref[...] = y * lax.rsqrt(ms + NORM_EPS) * fnw_ref[...]


def _ffn(x, nw, weights, cw, cb, hist, seq_len, tm, tf, final_nw=None):
    n, d = x.shape
    final = final_nw is not None
    (wg, wg_l, wg_off), (wv, wv_l, wv_off), (wd, wd_l) = weights
    emit = wg.dtype != BF16
    assert not emit or n == tm
    nf = D_FF // tf
    long_seq = seq_len > SUBLANES
    tps = max(seq_len // tm, 1)
    if long_seq:
        hist_spec = pl.BlockSpec((SUBLANES, tf), lambda i, j: (i // tps, j))
        st_spec = pl.BlockSpec((SUBLANES, tf), lambda i, j: (i, j))
        st_rows = n // tm * SUBLANES
    else:
        hist_spec = pl.BlockSpec((tm, tf), lambda i, j: (i, j))
        st_spec = pl.BlockSpec((tm, tf), lambda i, j: (i, j))
        st_rows = n
    body = functools.partial(_ffn_body, long_seq=long_seq, tps=tps, final=final, emit=emit)
    row_tile = pl.BlockSpec((tm, d), lambda i, j: (i, 0))
    wide = pl.BlockSpec((1, d), lambda i, j: (0, 0))
    x_tile = (pl.BlockSpec((tm, d), lambda i, j: (i, 0), pipeline_mode=pl.Buffered(1))
              if emit else row_tile)
    in_specs = [x_tile, wide,
                pl.BlockSpec((None, d, tf), lambda i, j: (wg_l, 0, wg_off // tf + j)),
                pl.BlockSpec((None, d, tf), lambda i, j: (wv_l, 0, wv_off // tf + j)),
                pl.BlockSpec((3, tf), lambda i, j: (0, j)),
                pl.BlockSpec((1, tf), lambda i, j: (0, j)),
                pl.BlockSpec((None, tf, d), lambda i, j: (wd_l, j, 0)),
                hist_spec]
    args = [x, nw, wg, wv, cw, cb, wd, hist]
    out_specs = [row_tile, st_spec]
    out_shape = [jax.ShapeDtypeStruct((n, d), F32), jax.ShapeDtypeStruct((st_rows, D_FF), F32)]
    scratch = [pltpu.VMEM((tm, d), BF16), pltpu.VMEM((nf, SUBLANES, tf), F32)]
    if emit:
        up_tile = pl.BlockSpec((None, d, tf), lambda i, j: (0, 0, j))
        out_specs += [up_tile, up_tile, pl.BlockSpec((None, tf, d), lambda i, j: (0, j, 0))]
        out_shape += [jax.ShapeDtypeStruct((1, d, D_FF), BF16)] * 2 + [
            jax.ShapeDtypeStruct((1, D_FF, d), BF16)]
    if final:
        in_specs.append(wide)
        args.append(final_nw)
    return pl.pallas_call(
        body, grid=(n // tm, nf), in_specs=in_specs, out_specs=out_specs, out_shape=out_shape,
        scratch_shapes=scratch,
        compiler_params=pltpu.CompilerParams(dimension_semantics=("arbitrary",) * 2,
                                             vmem_limit_bytes=BIG_VMEM_LIMIT),
        name="ffn")(*args)


def _rwkv_pre_body(*refs, long_seq, tps, with_n):
    (p_ref, hist_ref, mu_ref, w0_ref, w2_ref, a0_ref, a2_ref, g2_ref, kk_ref, ka_ref, rk_ref,
     ones_ref) = refs[:12]
    refs = refs[12:]
    if with_n:
        l_ref, refs = refs[0], refs[1:]
    r_ref, w_ref, k_ref, v_ref, na_ref, kb_ref, g_ref, bonus_ref, st_ref = refs[:9]
    refs = refs[9:]
    if with_n:
        n_ref, refs = refs[0], refs[1:]
    (carry_ref,) = refs
    p = p_ref[...]
    if long_seq:
        @pl.when(pl.program_id(0) % tps == 0)
        def _():
            carry_ref[...] = hist_ref[...]
        hist = carry_ref[...]
    else:
        hist = hist_ref[...]
    (prev,) = _prev_rows(p, hist, (1,), long_seq)
    if long_seq:
        carry_ref[...] = p[p.shape[0] - SUBLANES:]
        st_ref[...] = p[p.shape[0] - SUBLANES:]
    else:
        st_ref[...] = p
    xs = p + (prev - p) * mu_ref[...]
    r = xs[:, 0:RWKV_W]
    k = xs[:, RWKV_W:2 * RWKV_W]
    v = xs[:, 2 * RWKV_W:3 * RWKV_W]
    xwa = xs[:, 3 * RWKV_W:3 * RWKV_W + 128]
    xg = xs[:, 3 * RWKV_W + 128:RWKV_COLS]
    zw = w0_ref[...] + _dot(jnp.tanh(xwa).astype(BF16), w2_ref[...])
    w_log = -_softplus(-zw) - 0.5
    decay = jnp.exp(-jnp.exp(w_log))
    a = jax.nn.sigmoid(a0_ref[...] + _dot(xwa.astype(BF16), a2_ref[...]))
    g = _dot(jax.nn.sigmoid(xg).astype(BF16), g2_ref[...])
    ones = ones_ref[...]
    kk = k * kk_ref[...]
    kk = kk / jnp.maximum(jnp.sqrt(_split_dot(kk * kk, ones)), 1e-12)
    kf = k * (1.0 + (a - 1.0) * ka_ref[...])
    r_ref[...] = r
    w_ref[...] = decay
    k_ref[...] = kf
    v_ref[...] = v
    na_ref[...] = -kk
    kb_ref[...] = kk * a
    g_ref[...] = g
    bonus_ref[...] = _split_dot(r * kf * rk_ref[...], ones) * v
    if with_n:
        c = RWKV_CHUNK
        pw = 2 * RWKV_N
        lw = jnp.log(decay)
        cw = _split_dot_left(l_ref[...], lw)
        at = _split(-kk * jnp.exp(cw - lw))
        bt = _split(kk * a * jnp.exp(-cw))
        strict = (lax.broadcasted_iota(jnp.int32, (c, c), 1)
                  < lax.broadcasted_iota(jnp.int32, (c, c), 0))
        lane = lax.broadcasted_iota(jnp.int32, (p.shape[0], pw), 1)
        for pr in range(RWKV_HEADS // 2):
            sl = slice(pr * pw, (pr + 1) * pw)
            for hh in range(2):
                m = (lane < RWKV_N) if hh == 0 else (lane >= RWKV_N)
                ah = jnp.where(m, at[0][:, sl], jnp.zeros_like(at[0][:, sl]))
                al = jnp.where(m, at[1][:, sl], jnp.zeros_like(at[1][:, sl]))
                for ci in range(p.shape[0] // c):
                    rs = slice(ci * c, (ci + 1) * c)
                    bh, bl = bt[0][rs, sl], bt[1][rs, sl]
                    nm = _dot_nt(ah[rs], bh) + (_dot_nt(ah[rs], bl) + _dot_nt(al[rs], bh))
                    n_ref[2 * pr + hh, ci] = jnp.where(strict, nm, 0.0)


def _rwkv_pre(proj, hist, lp, ones, seq_len, tm, lmat=None):
    n = proj.shape[0]
    long_seq = seq_len > SUBLANES
    tps = max(seq_len // tm, 1)
    with_n = lmat is not None
    row = lambda c: pl.BlockSpec((1, c), lambda i: (0, 0))
    full = lambda a, b: pl.BlockSpec((a, b), lambda i: (0, 0))
    if long_seq:
        hist_spec = pl.BlockSpec((SUBLANES, RWKV_COLS), lambda i: (i // tps, 0))
    else:
        hist_spec = pl.BlockSpec((tm, RWKV_COLS), lambda i: (i, 0))
    tile = pl.BlockSpec((tm, RWKV_W), lambda i: (i, 0))
    body = functools.partial(_rwkv_pre_body, long_seq=long_seq, tps=tps, with_n=with_n)
    in_specs = [pl.BlockSpec((tm, RWKV_COLS), lambda i: (i, 0)), hist_spec,
                row(RWKV_COLS), row(RWKV_W), full(128, RWKV_W), row(RWKV_W),
                full(128, RWKV_W), full(128, RWKV_W), row(RWKV_W), row(RWKV_W), row(RWKV_W),
                full(RWKV_W, RWKV_W)]
    args = [proj, hist, lp["mu"], lp["w0"], lp["w2p"], lp["a0"], lp["a2p"], lp["g2"],
            lp["k_k"], lp["k_a"], lp["r_k"], ones]
    out_specs = [tile] * 8 + [hist_spec]
    out_shape = [jax.ShapeDtypeStruct((n, RWKV_W), F32)] * 8 + [jax.ShapeDtypeStruct(hist.shape, F32)]
    if with_n:
        c = RWKV_CHUNK
        in_specs.append(full(tm, tm))
        args.append(lmat)
        out_specs.append(pl.BlockSpec((RWKV_HEADS, tm // c, c, c), lambda i: (0, i, 0, 0)))
        out_shape.append(jax.ShapeDtypeStruct((RWKV_HEADS, n // c, c, c), F32))
    return pl.pallas_call(
        body, grid=(n // tm,), in_specs=in_specs, out_specs=out_specs, out_shape=out_shape,
        scratch_shapes=[pltpu.VMEM((SUBLANES, RWKV_COLS), F32)],
        compiler_params=_cp(1), name="rwkv_pre")(*args)


def _rwkv_post_body(y_ref, bonus_ref, g_ref, lw_ref, lb_ref, ones_ref, o_ref):
    y = y_ref[...]
    ones = ones_ref[...]
    mean = _split_dot(y, ones) * (1.0 / RWKV_N)
    d = y - mean
    var = _split_dot(d * d, ones) * (1.0 / RWKV_N)
    yn = d * lax.rsqrt(var + RWKV_LN_EPS)
    o_ref[...] = (yn * lw_ref[...] + lb_ref[...] + bonus_ref[...]) * g_ref[...]


def _rwkv_post(y, bonus, g, lw, lb, ones, tm):
    n = y.shape[0]
    tile = pl.BlockSpec((tm, RWKV_W), lambda i: (i, 0))
    row = pl.BlockSpec((1, RWKV_W), lambda i: (0, 0))
    return pl.pallas_call(
        _rwkv_post_body, grid=(n // tm,),
        in_specs=[tile, tile, tile, row, row, pl.BlockSpec((RWKV_W, RWKV_W), lambda i: (0, 0))],
        out_specs=tile, out_shape=jax.ShapeDtypeStruct((n, RWKV_W), F32),
        compiler_params=_cp(1), name="rwkv_post")(y, bonus, g, lw, lb, ones)


def _rwkv_lanes_body(r_ref, w_ref, k_ref, v_ref, a_ref, b_ref, s0_ref, y_ref, sf_ref,
                     s_ref, xt_ref, y_buf):
    tb = pl.program_id(1)
    ninst = r_ref.shape[1]

    @pl.when(tb == 0)
    def _():
        for hh in range(2):
            s_ref[hh] = s0_ref[hh]

    def step(tl, carry):
        for idx, ref in enumerate((r_ref, w_ref, k_ref, v_ref, a_ref, b_ref)):
            xt_ref[idx] = ref[tl].T
        for hh in range(2):
            base = hh * RWKV_N
            heads = lambda i: xt_ref[i, base:base + RWKV_N, :]

            def row(rho, c):
                s = s_ref[hh, rho]
                sa = jnp.sum(s * heads(4), axis=0, keepdims=True)
                vrow = xt_ref[3, pl.ds(base + rho, 1), :]
                s = s * heads(1) + sa * heads(5) + vrow * heads(2)
                s_ref[hh, rho] = s
                y_buf[pl.ds(base + rho, 1), :] = jnp.sum(s * heads(0), axis=0, keepdims=True)
                return c

            lax.fori_loop(0, RWKV_N, row, 0, unroll=8)
        y_ref[tl] = y_buf[...].T
        return carry

    lax.fori_loop(0, SUBLANES, step, 0)

    @pl.when(tb == pl.num_programs(1) - 1)
    def _():
        for hh in range(2):
            sf_ref[hh] = s_ref[hh]


def _rwkv_lanes(xs, s0):
    steps, ninst, _ = xs[0].shape
    xspec = pl.BlockSpec((SUBLANES, ninst, 2 * RWKV_N), lambda h, t: (t, 0, h))
    sspec = pl.BlockSpec((2, RWKV_N, RWKV_N, ninst), lambda h, t: (h, 0, 0, 0))
    return pl.pallas_call(
        _rwkv_lanes_body, grid=(RWKV_HEADS // 2, steps // SUBLANES),
        in_specs=[xspec] * 6 + [sspec], out_specs=[xspec, sspec],
        out_shape=[jax.ShapeDtypeStruct((steps, ninst, RWKV_W), F32),
                   jax.ShapeDtypeStruct(s0.shape, F32)],
        scratch_shapes=[pltpu.VMEM((2, RWKV_N, RWKV_N, ninst), F32),
                        pltpu.VMEM((6, 2 * RWKV_N, ninst), F32),
                        pltpu.VMEM((2 * RWKV_N, ninst), F32)],
        compiler_params=_cp(2), name="rwkv_lanes")(*xs, s0)


RWKV_C_CHUNKS = 4


def _tri_solve_body(nt_ref, tt_ref):
    c = RWKV_CHUNK
    ninst = nt_ref.shape[2]
    tt_ref[...] = jnp.zeros_like(tt_ref)
    g = SUBLANES
    nb = 4
    sub = lax.broadcasted_iota(jnp.int32, (g, ninst), 0)
    for t0 in range(0, c, nb):
        ngrp = (t0 + nb - 1) // g + 1
        rs = [[jnp.where(sub + g * j == t0 + i, 1.0, 0.0).astype(F32) for j in range(ngrp)]
              for i in range(nb)]
        for sg in range(-(-t0 // g)):

            def acc(s, carry, sg=sg, t0=t0):
                ts = [tt_ref[s, j * g:(j + 1) * g, :] for j in range(sg + 1)]
                nrow = [nt_ref[t0 + i, pl.ds(s, 1), :] for i in range(nb)]
                return tuple(tuple(carry[i][j] + nrow[i] * ts[j] for j in range(sg + 1))
                             for i in range(nb))

            res = lax.fori_loop(sg * g, min(sg * g + g, t0), acc,
                                tuple(tuple(rs[i][:sg + 1]) for i in range(nb)))
            for i in range(nb):
                rs[i][:sg + 1] = list(res[i])
        for i in range(nb):
            for j in range(i):
                nij = nt_ref[t0 + i, t0 + j:t0 + j + 1, :]
                rs[i] = [x + nij * y for x, y in zip(rs[i], rs[j])]
            tt_ref[t0 + i, 0:ngrp * g, :] = jnp.concatenate(rs[i], axis=0)


def _tri_solve(nt):
    nh, c, _, ninst = nt.shape
    spec = pl.BlockSpec((None, c, c, ninst), lambda h: (h, 0, 0, 0))
    return pl.pallas_call(
        _tri_solve_body, grid=(nh,), in_specs=[spec], out_specs=spec,
        out_shape=jax.ShapeDtypeStruct(nt.shape, F32),
        compiler_params=_cp(1), name="tri_solve")(nt)


def _rwkv_chunk_body(r_ref, w_ref, k_ref, v_ref, a_ref, b_ref, t_ref, l_ref, y_ref, sf_ref, s_ref):
    c = RWKV_CHUNK
    pw = 2 * RWKV_N
    ci = pl.program_id(1)

    @pl.when(ci == 0)
    def _():
        s_ref[...] = jnp.zeros_like(s_ref)

    lmat = l_ref[...]
    ti = lax.broadcasted_iota(jnp.int32, (c, c), 0)
    si = lax.broadcasted_iota(jnp.int32, (c, c), 1)
    strict, incl = si < ti, si <= ti
    lane = lax.broadcasted_iota(jnp.int32, (c, pw), 1)
    lane2 = lax.broadcasted_iota(jnp.int32, (2 * c, pw), 1)
    r2 = lax.broadcasted_iota(jnp.int32, (pw, pw), 0)
    c2 = lax.broadcasted_iota(jnp.int32, (pw, pw), 1)
    same_head = (r2 < RWKV_N) == (c2 < RWKV_N)
    eye = r2 == c2
    bz = lambda x: jnp.zeros_like(x)
    pairs = range(RWKV_HEADS // 2)
    units = [(q, p) for q in range(RWKV_C_CHUNKS) for p in pairs]
    at = lambda ref, u: ref[u[0] * c:(u[0] + 1) * c, u[1] * pw:(u[1] + 1) * pw]
    halves = [(u, hh) for u in units for hh in range(2)]
    lws = {u: jnp.log(at(w_ref, u)) for u in units}
    cws = {u: _split_dot_left(lmat, lws[u]) for u in units}
    vs = {u: at(v_ref, u) for u in units}
    wts = {u: jnp.exp(cws[u]) for u in units}
    rts = {u: at(r_ref, u) * wts[u] for u in units}
    ats = {u: at(a_ref, u) * jnp.exp(cws[u] - lws[u]) for u in units}
    lhs = {u: _split(jnp.concatenate([ats[u], rts[u]], axis=0)) for u in units}
    bts = {u: _split(at(b_ref, u) * jnp.exp(-cws[u])) for u in units}
    kts = {u: _split(at(k_ref, u) * jnp.exp(-cws[u])) for u in units}
    a_ak, a_rk, a_rb = {}, {}, {}
    for u, hh in halves:
        m2 = (lane2 < RWKV_N) if hh == 0 else (lane2 >= RWKV_N)
        lh = jnp.where(m2, lhs[u][0], bz(lhs[u][0]))
        ll = jnp.where(m2, lhs[u][1], bz(lhs[u][1]))
        pak = (_dot_nt(lh[:c], kts[u][0])
               + (_dot_nt(lh[:c], kts[u][1]) + _dot_nt(ll[:c], kts[u][0])))
        a_ak[u, hh] = jnp.where(strict, pak, 0.0)
        a_rk[u, hh] = jnp.where(incl, _dot_nt(lh[c:], kts[u][0]), 0.0).astype(BF16)
        a_rb[u, hh] = jnp.where(incl, _dot_nt(lh[c:], bts[u][0]), 0.0).astype(BF16)
    vps = {uh: _dot3(a_ak[uh], vs[uh[0]]) for uh in halves}
    gs = {(u, hh): _dot3(t_ref[2 * u[1] + hh, u[0]], jnp.concatenate([ats[u], vps[u, hh]], axis=1))
          for u, hh in halves}
    pick = lambda x0, x1: jnp.where(lane < RWKV_N, x0, x1)
    ahats = {u: pick(gs[u, 0][:, :pw], gs[u, 1][:, :pw]) for u in units}
    vhats = {u: pick(gs[u, 0][:, pw:], gs[u, 1][:, pw:]) for u in units}
    ahb = {u: ahats[u].astype(BF16) for u in units}
    vhb = {u: vhats[u].astype(BF16) for u in units}
    vbs = {u: vs[u].astype(BF16) for u in units}
    rhats = {u: (rts[u] + pick(_dot(a_rb[u, 0], ahb[u]), _dot(a_rb[u, 1], ahb[u]))).astype(BF16)
             for u in units}
    yhats = {u: pick(_dot(a_rb[u, 0], vhb[u]) + _dot(a_rk[u, 0], vbs[u]),
                     _dot(a_rb[u, 1], vhb[u]) + _dot(a_rk[u, 1], vbs[u])) for u in units}
    wends = {u: jnp.exp(cws[u][c - 1:c] - cws[u]) for u in units}
    bkts = {u: jnp.concatenate([at(b_ref, u) * wends[u], at(k_ref, u) * wends[u]], axis=0).T
            for u in units}
    mpts = {u: jnp.where(same_head, _dot3(bkts[u], jnp.concatenate([ahats[u], bz(ahats[u])], axis=0)), 0.0)
            + jnp.where(eye, wts[u][c - 1:c], 0.0) for u in units}
    zpts = {u: jnp.where(same_head, _dot3(bkts[u], jnp.concatenate([vhats[u], vs[u]], axis=0)), 0.0)
            for u in units}
    spts = [s_ref[p] for p in pairs]
    for q in range(RWKV_C_CHUNKS):
        for p in pairs:
            y_ref[q * c:(q + 1) * c, p * pw:(p + 1) * pw] = (
                _dot(rhats[q, p], spts[p].astype(BF16)) + yhats[q, p])
        spts = [_dot3(mpts[q, p], spts[p]) + zpts[q, p] for p in pairs]
    for p in pairs:
        s_ref[p] = spts[p]

    @pl.when(ci == pl.num_programs(1) - 1)
    def _():
        sf_ref[...] = s_ref[...]


def _rwkv_chunk(xs, t4, lmat, nseq):
    n = xs[0].shape[0]
    c = RWKV_CHUNK
    q = RWKV_C_CHUNKS
    nch = n // (q * c) // nseq
    pw = 2 * RWKV_N
    tile = pl.BlockSpec((q * c, RWKV_W), lambda b, i: (b * nch + i, 0))
    return pl.pallas_call(
        _rwkv_chunk_body, grid=(nseq, nch),
        in_specs=[tile] * 6 + [pl.BlockSpec((RWKV_HEADS, q, c, c), lambda b, i: (0, b * nch + i, 0, 0)),
                               pl.BlockSpec((c, c), lambda b, i: (0, 0))],
        out_specs=[tile, pl.BlockSpec((None, RWKV_HEADS // 2, pw, pw), lambda b, i: (b, 0, 0, 0))],
        out_shape=[jax.ShapeDtypeStruct((n, RWKV_W), F32),
                   jax.ShapeDtypeStruct((nseq, RWKV_HEADS // 2, pw, pw), F32)],
        scratch_shapes=[pltpu.VMEM((RWKV_HEADS // 2, pw, pw), F32)],
        compiler_params=_cp(2), name="rwkv_chunk")(*xs, t4, lmat)


def _hgrn_lower_bound(logits, layer):
    m = jnp.max(logits, axis=0, keepdims=True)
    e = jnp.exp(logits - m)
    gam = e / jnp.sum(e, axis=0, keepdims=True)
    cs = gam[0:1]
    for i in range(1, layer + 1):
        cs = cs + gam[i:i + 1]
    return cs - gam[0:1]


def _hgrn_level_matrix():
    t = np.arange(HGRN_CHUNK)[:, None]
    s = np.arange(HGRN_CHUNK)[None, :]
    mats = []
    for upper in (False, True):
        for e in (1, 2):
            same = (t >> e) == (s >> e)
            mats.append(same & ((s > t) if upper else (s <= t)))
    mats.append(s <= t)
    return jnp.asarray(np.concatenate(mats, axis=0).astype(np.float32), dtype=BF16)


def _hgrn_prompt_body(*refs, layer):
    npair = HGRN_HEADS // 2
    q_refs, f_refs, i_refs, g_refs = (refs[i * npair:(i + 1) * npair] for i in range(4))
    lg_ref, nw_ref, cm_ref, o_ref, sf_ref, s_ref = refs[4 * npair:]
    c = pl.program_id(1)

    @pl.when(c == 0)
    def _():
        s_ref[...] = jnp.zeros_like(s_ref)

    n = HGRN_CHUNK
    heads = range(HGRN_HEADS)
    col = lambda rs, h: rs[h // 2][:, (h % 2) * HGRN_D:(h % 2 + 1) * HGRN_D]
    hsl = lambda h: slice(h * HGRN_D, (h + 1) * HGRN_D)
    t = lax.broadcasted_iota(jnp.int32, (n, n), 0)
    s_ = lax.broadcasted_iota(jnp.int32, (n, n), 1)
    lb_all = _hgrn_lower_bound(lg_ref[...], layer)
    cm = cm_ref[...]
    ones = jnp.ones((n, n), BF16)
    fgs = [lb_all[:, hsl(h)] + (1.0 - lb_all[:, hsl(h)]) * jax.nn.sigmoid(col(f_refs, h))
           for h in heads]
    lfs = [jnp.log(fg) for fg in fgs]
    css = [_split_dot_left(cm, lf) for lf in lfs]
    bs = [cs[4 * n:5 * n] for cs in css]

    def cl(h, e):
        if e <= 2:
            return css[h][(e - 1) * n:e * n]
        if e == HGRN_LEVELS:
            return bs[h]
        m = 1 << e
        b3 = bs[h].reshape(n // m, m, HGRN_D)
        ends = jnp.broadcast_to(b3[:, m - 1:m, :], b3.shape)
        before = jnp.concatenate([jnp.zeros_like(ends[:1]), ends[:-1]], axis=0)
        return (b3 - before).reshape(n, HGRN_D)

    def cu(h, e):
        if e <= 2:
            return css[h][(2 + e - 1) * n:(2 + e) * n]
        if e == HGRN_LEVELS:
            return bs[h][n - 1:n] - bs[h]
        m = 1 << e
        b3 = bs[h].reshape(n // m, m, HGRN_D)
        return (jnp.broadcast_to(b3[:, m - 1:m, :], b3.shape) - b3).reshape(n, HGRN_D)
    qs = [col(q_refs, h) * jax.nn.sigmoid(col(q_refs, h)) for h in heads]
    ks = [1.0 - fg for fg in fgs]
    vbs = [col(i_refs, h).astype(BF16) for h in heads]
    acc = [jnp.where(t == s_, _dot_nt(qs[h].astype(BF16), ks[h].astype(BF16)), 0.0) for h in heads]
    for e in range(HGRN_LEVELS):
        mask = (((t >> e) & 1) == 1) & ((s_ >> e) == ((t >> e) - 1))
        for h in heads:
            qe = qs[h] * (fgs[h] if e == 0 else jnp.exp(cl(h, e)))
            ke = ks[h] if e == 0 else ks[h] * jnp.exp(cu(h, e))
            acc[h] = acc[h] + jnp.where(mask, _dot_nt(qe.astype(BF16), ke.astype(BF16)), 0.0)
    sts = [s_ref[h] for h in heads]
    os_ = [_dot(acc[h].astype(BF16), vbs[h])
           + _dot((qs[h] * jnp.exp(cl(h, HGRN_LEVELS))).astype(BF16), sts[h].astype(BF16))
           for h in heads]
    dks = [jnp.exp(_split_dot(lfs[h].T, ones)) for h in heads]
    for h in heads:
        kf = ks[h] * jnp.exp(cu(h, HGRN_LEVELS))
        s_ref[h] = dks[h] * sts[h] + _dot(kf.T.astype(BF16), vbs[h])
    for h in heads:
        o = os_[h]
        ms = jnp.mean(o * o, axis=-1, keepdims=True)
        gr = col(g_refs, h)
        o_ref[:, hsl(h)] = (o * lax.rsqrt(ms + NORM_EPS) * nw_ref[:, hsl(h)]) * (gr * jax.nn.sigmoid(gr))

    @pl.when(c == pl.num_programs(1) - 1)
    def _():
        sf_ref[...] = s_ref[...]


def _split_dot_left(m, x):
    hi, lo = _split(x)
    return _dot(m, hi) + _dot(m, lo)


def _hgrn_prompt(proj, logits, nw, cmat, nseq, seq_len, layer):
    n = proj.shape[0]
    nch = seq_len // HGRN_CHUNK
    wb = 2 * HGRN_D
    npair = HGRN_HEADS // 2
    cols = [pl.BlockSpec((HGRN_CHUNK, wb), lambda b, c, j=(HGRN_COL0 + g * HGRN_W) // wb + p:
                         (b * nch + c, j)) for g in range(4) for p in range(npair)]
    return pl.pallas_call(
        functools.partial(_hgrn_prompt_body, layer=layer),
        grid=(nseq, nch),
        in_specs=cols + [pl.BlockSpec(logits.shape, lambda b, c: (0, 0)),
                         pl.BlockSpec((1, HGRN_W), lambda b, c: (0, 0)),
                         pl.BlockSpec(cmat.shape, lambda b, c: (0, 0))],
        out_specs=[pl.BlockSpec((HGRN_CHUNK, HGRN_W), lambda b, c: (b * nch + c, 0)),
                   pl.BlockSpec((None, HGRN_HEADS, HGRN_D, HGRN_D), lambda b, c: (b, 0, 0, 0))],
        out_shape=[jax.ShapeDtypeStruct((n, HGRN_W), F32),
                   jax.ShapeDtypeStruct((nseq, HGRN_HEADS, HGRN_D, HGRN_D), F32)],
        scratch_shapes=[pltpu.VMEM((HGRN_HEADS, HGRN_D, HGRN_D), F32)],
        compiler_params=_cp(2), name="hgrn_prompt")(*([proj] * (4 * npair)), logits, nw, cmat)


HGRN_KT = 32


def _hgrn_lanes_body(q_ref, f_ref, i_ref, g_ref, lg_ref, nw_ref, s0_ref, o_ref, sf_ref,
                     qt_ref, ft_ref, kt_ref, vt_ref, oacc_ref, *, layer):
    kt = pl.program_id(1)
    lb = _hgrn_lower_bound(lg_ref[...], layer)

    @pl.when(kt == 0)
    def _():
        oacc_ref[...] = jnp.zeros_like(oacc_ref)
        for tl in range(SUBLANES):
            qr = q_ref[tl]
            fg = lb + (1.0 - lb) * jax.nn.sigmoid(f_ref[tl])
            qt_ref[tl] = (qr * jax.nn.sigmoid(qr)).T
            ft_ref[tl] = fg.T
            kt_ref[tl] = (1.0 - fg).T
            vt_ref[tl] = i_ref[tl].T

    def step(tl, carry, src_ref=sf_ref):
        vt = vt_ref[tl]

        def krow(kk, o):
            row = kt * HGRN_KT + kk
            s = ft_ref[tl, pl.ds(row, 1), :] * src_ref[0, kk] + kt_ref[tl, pl.ds(row, 1), :] * vt
            sf_ref[0, kk] = s
            return o + qt_ref[tl, pl.ds(row, 1), :] * s

        oacc_ref[tl] = lax.fori_loop(0, HGRN_KT, krow, oacc_ref[tl], unroll=2)
        return carry

    step(0, 0, s0_ref)
    lax.fori_loop(1, SUBLANES, step, 0)

    @pl.when(kt == pl.num_programs(1) - 1)
    def _():
        for tl in range(SUBLANES):
            o = oacc_ref[tl].T
            ms = jnp.mean(o * o, axis=-1, keepdims=True)
            gr = g_ref[tl]
            o_ref[tl] = (o * lax.rsqrt(ms + NORM_EPS) * nw_ref[...]) * (gr * jax.nn.sigmoid(gr))


def _hgrn_lanes(proj_t, logits, nw, s0, layer):
    steps, ninst, _ = proj_t.shape
    col = lambda off: pl.BlockSpec((steps, ninst, HGRN_D),
                                   lambda h, k, off=off: (0, 0, off // HGRN_D + h))
    sspec = pl.BlockSpec((1, HGRN_KT, HGRN_D, ninst), lambda h, k: (h, k, 0, 0))
    tbuf = pltpu.VMEM((steps, HGRN_D, ninst), F32)
    return pl.pallas_call(
        functools.partial(_hgrn_lanes_body, layer=layer),
        grid=(HGRN_HEADS, HGRN_D // HGRN_KT),
        in_specs=[col(0), col(HGRN_W), col(2 * HGRN_W), col(3 * HGRN_W),
                  pl.BlockSpec((logits.shape[0], HGRN_D), lambda h, k: (0, h)),
                  pl.BlockSpec((1, HGRN_D), lambda h, k: (0, h)), sspec],
        out_specs=[pl.BlockSpec((steps, ninst, HGRN_D), lambda h, k: (0, 0, h)), sspec],
        out_shape=[jax.ShapeDtypeStruct((steps, ninst, HGRN_W), F32),
                   jax.ShapeDtypeStruct(s0.shape, F32)],
        scratch_shapes=[tbuf, tbuf, tbuf, tbuf, tbuf],
        compiler_params=_cp(2), name="hgrn_lanes")(
            proj_t, proj_t, proj_t, proj_t, logits, nw, s0)


def _rglru_body(xb_ref, gate_ref, chist_ref, hinit_ref, cw_ref, cb_ref, wa_ref, ba_ref, wx_ref,
                bx_ref, lam_ref, y_ref, cst_ref, hst_ref, ccarry_ref, hcarry_ref,
                *, long_seq, tps):
    x = xb_ref[...]
    tm, c = x.shape
    if long_seq:
        @pl.when(pl.program_id(0) % tps == 0)
        def _():
            ccarry_ref[...] = chist_ref[...]
            hcarry_ref[...] = hinit_ref[...]
        hist = ccarry_ref[...]
    else:
        hist = chist_ref[...]
    p1, p2, p3 = _prev_rows(x, hist, (1, 2, 3), long_seq)
    cw = cw_ref[...]
    xc = cb_ref[...] + cw[0:1] * p3 + cw[1:2] * p2 + cw[2:3] * p1 + cw[3:4] * x
    xcb = xc.astype(BF16)
    r = jax.nn.sigmoid(_dot(xcb, wa_ref[...]) + ba_ref[...])
    ig = jax.nn.sigmoid(_dot(xcb, wx_ref[...]) + bx_ref[...])
    log_a = (-LRU_C) * r * _softplus(-lam_ref[...])
    a = jnp.exp(log_a)
    h = jnp.sqrt(-_expm1(2.0 * log_a)) * (ig * xc)
    if long_seq:
        ng = tm // SUBLANES
        shp = (ng, SUBLANES, c)
        h3, a3 = h.reshape(shp), a.reshape(shp)
        pos = lax.broadcasted_iota(jnp.int32, shp, 1)
        k = 1
        while k < SUBLANES:
            keep = pos >= k
            h3 = h3 + a3 * jnp.where(keep, pltpu.roll(h3, k, 1), 0.0)
            a3 = a3 * jnp.where(keep, pltpu.roll(a3, k, 1), 1.0)
            k *= 2
        h_in = hcarry_ref[SUBLANES - 1:SUBLANES, :]
        groups = []
        for gi in range(ng):
            hg = h3[gi] + a3[gi] * h_in
            groups.append(hg)
            h_in = hg[SUBLANES - 1:SUBLANES, :]
        h = jnp.concatenate(groups, axis=0)
        hcarry_ref[...] = h[tm - SUBLANES:]
        ccarry_ref[...] = x[tm - SUBLANES:]
        cst_ref[...] = x[tm - SUBLANES:]
        hst_ref[...] = h[tm - SUBLANES:]
    else:
        shp = (tm // SUBLANES, SUBLANES, c)
        h3, a3 = h.reshape(shp), a.reshape(shp)
        pos = lax.broadcasted_iota(jnp.int32, shp, 1)
        k = 1
        while k < SUBLANES:
            keep = pos >= k
            h3 = h3 + a3 * jnp.where(keep, pltpu.roll(h3, k, 1), 0.0)
            a3 = a3 * jnp.where(keep, pltpu.roll(a3, k, 1), 1.0)
            k *= 2
        h = (h3 + a3 * hinit_ref[...].reshape(shp)).reshape(tm, c)
        cst_ref[...] = x
        hst_ref[...] = h
    y_ref[...] = h * jax.nn.gelu(gate_ref[...])


def _rglru(proj, chist, hinit, lp, seq_len, tm):
    n = proj.shape[0]
    long_seq = seq_len > SUBLANES
    tps = max(seq_len // tm, 1)
    c0 = LRU_COL0 // LRU_W
    if long_seq:
        hs = pl.BlockSpec((SUBLANES, LRU_W), lambda i: (i // tps, 0))
    else:
        hs = pl.BlockSpec((tm, LRU_W), lambda i: (i, 0))
    row = pl.BlockSpec((1, LRU_W), lambda i: (0, 0))
    sq = pl.BlockSpec((LRU_W, LRU_W), lambda i: (0, 0))
    return pl.pallas_call(
        functools.partial(_rglru_body, long_seq=long_seq, tps=tps), grid=(n // tm,),
        in_specs=[pl.BlockSpec((tm, LRU_W), lambda i: (i, c0)),
                  pl.BlockSpec((tm, LRU_W), lambda i: (i, c0 + 1)), hs, hs,
                  pl.BlockSpec((4, LRU_W), lambda i: (0, 0)), row, sq, row, sq, row, row],
        out_specs=[pl.BlockSpec((tm, LRU_W), lambda i: (i, 0)), hs, hs],
        out_shape=[jax.ShapeDtypeStruct((n, LRU_W), F32),
                   jax.ShapeDtypeStruct(chist.shape, F32),
                   jax.ShapeDtypeStruct(chist.shape, F32)],
        scratch_shapes=[pltpu.VMEM((SUBLANES, LRU_W), F32), pltpu.VMEM((SUBLANES, LRU_W), F32)],
        compiler_params=_cp(1), name="rglru")(
            proj, proj, chist, hinit, lp["lru_cw"], lp["lru_cb"], lp["lru_wa"], lp["lru_ba"],
            lp["lru_wx"], lp["lru_bx"], lp["lru_lam"])


def _hist(state):
    nseq, k, c = state.shape
    return jnp.pad(state, ((0, 0), (SUBLANES - k, 0), (0, 0))).reshape(nseq * SUBLANES, c)


def _block_diag(w):
    h, a, b = w.shape
    eye = jnp.eye(h, dtype=w.dtype)
    return (eye[:, None, :, None] * w[:, :, None, :]).reshape(h * a, h * b)


def _layer(x3, st, lp, consts, layer, final_nw, w_in_at, w_out_at, ffn_w):
    nseq, seq_len, d = x3.shape
    n = nseq * seq_len
    long_seq = seq_len > SUBLANES
    s_rw, shift_rw, s_hg, h_lru, buf_lru, buf_ffn = st
    x = x3.reshape(n, d)
    ones = consts["ones"]

    w_in, w_in_layer = w_in_at
    proj, w_in_bf16 = _in_proj(x, lp["norm_mix"], w_in, w_in_layer, tm=min(n, 1024),
                               tn=IN_COLS // 4 if w_in.dtype == BF16 else 512)

    tm_pre = min(n, seq_len if long_seq else n, 256)
    lmat_pre = None
    if long_seq:
        tril = np.tril(np.ones((RWKV_CHUNK, RWKV_CHUNK)))
        lmat_pre = jnp.asarray(np.kron(np.eye(tm_pre // RWKV_CHUNK), tril), dtype=BF16)
    outs = _rwkv_pre(proj, _hist(shift_rw[:, None, :]), lp, ones, seq_len, tm_pre, lmat_pre)
    r, w, k, v, na, kb, g, bonus, shift_out = outs[:9]
    new_shift = shift_out.reshape(nseq, SUBLANES, RWKV_COLS)[:, SUBLANES - 1]
    if long_seq:
        c = RWKV_CHUNK
        t4 = _tri_solve(outs[9].transpose(0, 2, 3, 1)).transpose(0, 3, 1, 2)
        y, spt = _rwkv_chunk((r, w, k, v, na, kb), t4, consts["lmat"], nseq)
        sp = spt.reshape(nseq, RWKV_HEADS // 2, 2, RWKV_N, 2, RWKV_N)
        new_s_rw = jnp.stack([sp[:, :, 0, :, 0, :], sp[:, :, 1, :, 1, :]], axis=2)
        new_s_rw = new_s_rw.reshape(nseq, RWKV_HEADS, RWKV_N, RWKV_N).transpose(0, 1, 3, 2)
    else:
        tmaj = lambda t: t.reshape(nseq, seq_len, RWKV_W).transpose(1, 0, 2)
        s0 = s_rw.transpose(1, 2, 3, 0)
        yb, sf = _rwkv_lanes([tmaj(t) for t in (r, w, k, v, na, kb)], s0)
        y = yb.transpose(1, 0, 2).reshape(n, RWKV_W)
        new_s_rw = sf.transpose(3, 0, 1, 2)
    y_rw = _rwkv_post(y, bonus, g, lp["ln_w"], lp["ln_b"], ones, tm=min(n, 512))

    if long_seq:
        y_hg, new_s_hg = _hgrn_prompt(proj, consts["lb_logits"], lp["hgrn_nw"], consts["cmat"],
                                      nseq, seq_len, layer)
    else:
        proj_t = proj[:, HGRN_COL0:LRU_COL0].reshape(nseq, seq_len, 4 * HGRN_W).transpose(1, 0, 2)
        s0 = s_hg.transpose(1, 2, 3, 0)
        o_t, sf = _hgrn_lanes(proj_t, consts["lb_logits"], lp["hgrn_nw"], s0, layer)
        y_hg = o_t.transpose(1, 0, 2).reshape(n, HGRN_W)
        new_s_hg = sf.transpose(3, 0, 1, 2)

    if long_seq:
        hinit = _hist(h_lru[:, None, :])
        tm_lru = min(seq_len, 256)
    else:
        hinit = jnp.broadcast_to(h_lru[:, None, :], (nseq, seq_len, LRU_W)).reshape(n, LRU_W)
        tm_lru = min(n, 512)
    y_lru, cst, hst = _rglru(proj, _hist(buf_lru), hinit, lp, seq_len, tm_lru)
    new_buf_lru = cst.reshape(nseq, SUBLANES, LRU_W)[:, SUBLANES - 3:]
    new_h_lru = hst.reshape(nseq, SUBLANES, LRU_W)[:, SUBLANES - 1]

    w_out, w_out_layer = w_out_at
    x, w_out_bf16 = _out_proj(x, y_rw, y_hg, y_lru, w_out, w_out_layer, tm=min(n, 1024),
                              tn=1024 if w_out.dtype == BF16 else 512)
    tm_ffn = min(n, seq_len if long_seq else n, 1024)
    tf_ffn = 512 if ffn_w[0][0].dtype == BF16 else 256
    outs = _ffn(x, lp["norm_ffn"], ffn_w, lp["ffn_cw"], lp["ffn_cb"], _hist(buf_ffn), seq_len,
                tm=tm_ffn, tf=tf_ffn, final_nw=final_nw)
    x, fst = outs[:2]
    new_buf_ffn = fst.reshape(nseq, -1, SUBLANES, D_FF)[:, -1, SUBLANES - 2:]
    states = (new_s_rw, new_shift, new_s_hg, new_h_lru, new_buf_lru, new_buf_ffn)
    ffn_w_bf16 = None
    if len(outs) > 2:
        ffn_w_bf16 = ((outs[2], 0, 0), (outs[3], 0, 0), (outs[4], 0))
    return x.reshape(nseq, seq_len, d), states, (w_in_bf16, w_out_bf16, ffn_w_bf16)


def kernel(x_prompt, x_sample, state_rwkv, state_rwkv_shift, state_hgrn, state_rglru, cache_rglru_conv, cache_ffn_conv, norm_mix, w_in, rwkv_mu, rwkv_w0, rwkv_w2, rwkv_a0, rwkv_a2, rwkv_g2, rwkv_k_k, rwkv_k_a, rwkv_r_k, rwkv_ln_w, rwkv_ln_b, hgrn_lb_logits, hgrn_norm_w, rglru_conv_w, rglru_conv_b, rglru_wa, rglru_ba, rglru_wx, rglru_bx, rglru_lambda, w_out, norm_ffn, ffn_w_up, ffn_conv_w, ffn_conv_b, ffn_w_down, norm_final):
    depth = w_in.shape[0]
    nb, nt, _ = x_prompt.shape
    db = x_sample.shape[0]
    consts = {
        "ones": jnp.asarray(np.kron(np.eye(RWKV_HEADS), np.ones((RWKV_N, RWKV_N))), dtype=BF16),
        "cmat": _hgrn_level_matrix(),
        "lmat": jnp.asarray(np.tril(np.ones((RWKV_CHUNK, RWKV_CHUNK), np.float32)), dtype=BF16),
        "lb_logits": hgrn_lb_logits,
    }
    rowv = lambda a: a.reshape(1, -1)
    x_p, x_s = x_prompt, x_sample
    new_p, new_s = [], []
    for l in range(depth):
        lp = {
            "norm_mix": rowv(norm_mix[l]), "mu": rowv(rwkv_mu[l]),
            "w0": rowv(rwkv_w0[l]),
            "w2p": jnp.pad(rwkv_w2[l], ((0, 64), (0, 0))).astype(BF16),
            "a0": rowv(rwkv_a0[l]),
            "a2p": jnp.pad(rwkv_a2[l], ((64, 0), (0, 0))).astype(BF16),
            "g2": rwkv_g2[l].astype(BF16), "k_k": rowv(rwkv_k_k[l]), "k_a": rowv(rwkv_k_a[l]),
            "r_k": rowv(rwkv_r_k[l]), "ln_w": rowv(rwkv_ln_w[l]), "ln_b": rowv(rwkv_ln_b[l]),
            "hgrn_nw": rowv(hgrn_norm_w[l]),
            "lru_cw": rglru_conv_w[l], "lru_cb": rowv(rglru_conv_b[l]),
            "lru_wa": _block_diag(rglru_wa[l]).astype(BF16), "lru_ba": rowv(rglru_ba[l]),
            "lru_wx": _block_diag(rglru_wx[l]).astype(BF16), "lru_bx": rowv(rglru_bx[l]),
            "lru_lam": rowv(rglru_lambda[l]),
            "norm_ffn": rowv(norm_ffn[l]), "ffn_cw": ffn_conv_w[l],
            "ffn_cb": rowv(ffn_conv_b[l]),
        }
        zero = lambda *s: jnp.zeros(s, F32)
        st_p = (None, zero(nb, RWKV_COLS), None, zero(nb, LRU_W), zero(nb, 3, LRU_W),
                zero(nb, 2, D_FF))
        final_nw = rowv(norm_final) if l == depth - 1 else None
        st_s = (state_rwkv[l], state_rwkv_shift[l], state_hgrn[l], state_rglru[l],
                cache_rglru_conv[l], cache_ffn_conv[l])
        ffn_f32 = ((ffn_w_up, l, 0), (ffn_w_up, l, D_FF), (ffn_w_down, l))
        x_s, ss, (w_in_l, w_out_l, ffn_l) = _layer(x_s, st_s, lp, consts, l, final_nw, (w_in, l),
                                                   (w_out, l), ffn_f32)
        x_p, sp, _ = _layer(x_p, st_p, lp, consts, l, final_nw, (w_in_l, 0), (w_out_l, 0), ffn_l)
        new_p.append(sp)
        new_s.append(ss)
    stack = lambda sts: [jnp.stack(s, axis=0) for s in zip(*sts)]
    return (x_p, x_s, *stack(new_p), *stack(new_s))
```
